```python
import math
import jax
import jax.numpy as jnp
from jax import lax
import numpy as np

D_MODEL = 1024
BATCH = 2
SEQ = 8192
DEPTH = 4
DEC_BATCH = 128
DEC_SEQ = 4
PAST_LEN = 8192
PAGE_SIZE = 128

N_HEADS_A = 8
N_KV_A = 2
HEAD_DIM_A = 64
GROUP_A = N_HEADS_A // N_KV_A
Q_A = N_HEADS_A * HEAD_DIM_A
KV_A = N_KV_A * HEAD_DIM_A
WINDOW = 128
Q_BLOCK = 128
NUM_BUCKETS = 32
MAX_DISTANCE = 128
NEG_INF = -1e30
C_B = 512
CONV_B = 31
N_HEADS_C = 4
HEAD_DIM_C = 128
V_C = N_HEADS_C * HEAD_DIM_C
QKV_C = 3 * V_C
CONV_C = 4
DN_CHUNK = 64
N_BRANCH = 3
IN_SIZES = (Q_A, KV_A, KV_A, 2 * C_B, QKV_C, V_C, N_HEADS_C, N_HEADS_C, N_BRANCH * D_MODEL)
N_IN = sum(IN_SIZES)
N_EXPERTS = 32
TOP_K = 4
D_FF = 1024
SWIGLU_LIMIT = 7.0
SWIGLU_ALPHA = 1.702
MOE_BLOCK = 128
PLE_DIM = 256
DN_ALPHA = (2 * DEPTH) ** 0.25
DN_BETA = (8 * DEPTH) ** -0.25
LN_EPS = 1e-5
RMS_EPS = 1e-6

kernel_name = 'hybrid_gated_branch_decoder_step'


def layer_norm(x, g, b):
    xf = x.astype(jnp.float32)
    mu = jnp.mean(xf, -1, keepdims=True)
    var = jnp.mean(jnp.square(xf - mu), -1, keepdims=True)
    return ((xf - mu) * lax.rsqrt(var + LN_EPS) * g + b).astype(x.dtype)


def l2norm(x):
    return x * lax.rsqrt(jnp.sum(x * x, -1, keepdims=True) + RMS_EPS)


def causal_dwconv(xp, w):
    return lax.conv_general_dilated(xp, w[:, None, :].astype(xp.dtype), (1,), 'VALID',
                                    dimension_numbers=('NWC', 'WIO', 'NWC'),
                                    feature_group_count=xp.shape[-1])


def rel_bucket(n):
    max_exact = NUM_BUCKETS // 2
    nf = jnp.maximum(n, 1).astype(jnp.float32)
    large = max_exact + (jnp.log(nf / max_exact) / math.log(MAX_DISTANCE / max_exact)
                         * (NUM_BUCKETS - max_exact)).astype(jnp.int32)
    return jnp.where(n < max_exact, jnp.maximum(n, 0), jnp.minimum(large, NUM_BUCKETS - 1))


def window_attention(q, k, v, q_pos, k_pos, rel_bias, sinks):
    dist = q_pos[..., :, None] - k_pos[..., None, :]
    valid = (dist >= 0) & (dist < WINDOW) & (k_pos[..., None, :] >= 0)
    bias = rel_bias.astype(jnp.float32)[rel_bucket(dist)]
    bias = jnp.moveaxis(bias.reshape(bias.shape[:-1] + (N_KV_A, GROUP_A)), (-2, -1), (-4, -3))
    s = jnp.einsum('b...qhgd,b...khd->b...hgqk', q, k).astype(jnp.float32) * HEAD_DIM_A ** -0.5 + bias
    s = jnp.where(valid[..., None, None, :, :], s, NEG_INF)
    sink = sinks.astype(jnp.float32).reshape(N_KV_A, GROUP_A, 1, 1)
    m = jnp.maximum(jnp.max(s, -1, keepdims=True), sink)
    e = jnp.exp(s - m)
    pr = e / (jnp.sum(e, -1, keepdims=True) + jnp.exp(sink - m))
    return jnp.einsum('b...hgqk,b...khd->b...qhgd', pr.astype(v.dtype), v)


def chunk_gated_delta(q, k, v, g, beta, s0):
    n, L, H, dk = q.shape
    dv = v.shape[-1]
    c = DN_CHUNK
    nc = L // c

    def blk(t):
        return jnp.swapaxes(t.reshape((n, nc, c) + t.shape[2:]), 2, 3)

    q, k, v, g, beta = blk(q), blk(k), blk(v), blk(g), blk(beta)
    gc = jnp.cumsum(g, -1)
    tri = jnp.tril(jnp.ones((c, c), bool))
    stri = jnp.tril(jnp.ones((c, c), bool), -1)
    decay = jnp.where(tri, jnp.exp(jnp.where(tri, gc[..., :, None] - gc[..., None, :], 0.0)), 0.0)
    kk = jnp.einsum('bnhid,bnhjd->bnhij', k, k)
    lmat = jnp.where(stri, beta[..., :, None] * kk * decay, 0.0) + jnp.eye(c, dtype=jnp.float32)
    rhs = jnp.concatenate([v * beta[..., None], k * (beta * jnp.exp(gc))[..., None]], -1)
    sol = lax.linalg.triangular_solve(lmat, rhs, left_side=True, lower=True, unit_diagonal=True)
    u, w = sol[..., :dv], sol[..., dv:]
    qk = jnp.where(tri, jnp.einsum('bnhid,bnhjd->bnhij', q, k) * decay, 0.0)
    q_dec = q * jnp.exp(gc)[..., None]
    k_dec = k * jnp.exp(gc[..., -1:] - gc)[..., None]
    g_last = jnp.exp(gc[..., -1])

    def step(s, xs):
        u_i, w_i, qk_i, qd_i, kd_i, gl_i = xs
        v_new = u_i - jnp.einsum('bhck,bhkv->bhcv', w_i, s)
        o_i = jnp.einsum('bhck,bhkv->bhcv', qd_i, s) + jnp.einsum('bhij,bhjv->bhiv', qk_i, v_new)
        s = s * gl_i[..., None, None] + jnp.einsum('bhck,bhcv->bhkv', kd_i, v_new)
        return s, o_i

    s, o = lax.scan(step, s0, tuple(jnp.moveaxis(t, 1, 0) for t in (u, w, qk, q_dec, k_dec, g_last)))
    o = jnp.swapaxes(jnp.moveaxis(o, 0, 1), 2, 3).reshape(n, L, H, dv)
    return o, s


def recurrent_gated_delta(q, k, v, g, beta, s0):
    def step(s, xs):
        q_t, k_t, v_t, g_t, b_t = xs
        s = s * jnp.exp(g_t)[..., None, None]
        kv = jnp.einsum('bhk,bhkv->bhv', k_t, s)
        delta = (v_t - kv) * b_t[..., None]
        s = s + jnp.einsum('bhk,bhv->bhkv', k_t, delta)
        return s, jnp.einsum('bhk,bhkv->bhv', q_t, s)

    s, o = lax.scan(step, s0, tuple(jnp.moveaxis(t, 1, 0) for t in (q, k, v, g, beta)))
    return jnp.moveaxis(o, 0, 1), s


def moe_ffn(x2, w_router, b_router, w_up, b_up, w_down, b_down):
    t, d = x2.shape
    logits = (x2 @ w_router + b_router).astype(jnp.float32)
    top_val, top_idx = lax.top_k(logits, TOP_K)
    gate = jax.nn.softmax(top_val, axis=-1)
    a = t * TOP_K
    e_flat = top_idx.reshape(a)
    order = jnp.argsort(e_flat, stable=True)
    e_sorted = e_flat[order]
    counts = jnp.bincount(e_flat, length=N_EXPERTS)
    starts = jnp.cumsum(counts) - counts
    padded = (counts + MOE_BLOCK - 1) // MOE_BLOCK * MOE_BLOCK
    pad_end = jnp.cumsum(padded)
    dest = (pad_end - padded)[e_sorted] + jnp.arange(a) - starts[e_sorted]
    n_blk = -(-a // MOE_BLOCK) + N_EXPERTS
    rows = jnp.full((n_blk * MOE_BLOCK,), t, jnp.int32).at[dest].set((order // TOP_K).astype(jnp.int32))
    xb = jnp.concatenate([x2, jnp.zeros((1, d), x2.dtype)])[rows].reshape(n_blk, MOE_BLOCK, d)
    blk_e = jnp.minimum(jnp.searchsorted(pad_end, jnp.arange(n_blk) * MOE_BLOCK, side='right'), N_EXPERTS - 1)

    def expert_block(args):
        xe, e = args
        hu = xe @ w_up[e] + b_up[e]
        gt = jnp.minimum(hu[:, :D_FF], SWIGLU_LIMIT)
        up = jnp.clip(hu[:, D_FF:], -SWIGLU_LIMIT, SWIGLU_LIMIT)
        return ((up + 1) * gt * jax.nn.sigmoid(SWIGLU_ALPHA * gt)) @ w_down[e] + b_down[e]

    yb = lax.map(expert_block, (xb, blk_e)).reshape(n_blk * MOE_BLOCK, d)
    slot = jnp.zeros((a,), jnp.int32).at[order].set(dest.astype(jnp.int32))
    y = yb[slot].reshape(t, TOP_K, d)
    return jnp.einsum('tkd,tk->td', y, gate.astype(y.dtype))


def trunk_layer(x, pl, rel_bias, lw, past, pos0, w_buf):
    f32 = jnp.float32
    n, L, _ = x.shape
    cuts = [int(c) for c in np.cumsum(IN_SIZES)[:-1]]
    h = x @ lw['w_in'] + lw['b_in']
    q_a, k_a, v_a, glu_b, qkv_c, z_c, beta_c, a_c, gates = jnp.split(h, cuts, axis=-1)

    q = q_a.reshape(n, L, N_KV_A, GROUP_A, HEAD_DIM_A)
    k = k_a.reshape(n, L, N_KV_A, HEAD_DIM_A)
    v = v_a.reshape(n, L, N_KV_A, HEAD_DIM_A)
    if past is None:
        nb = L // Q_BLOCK

        def band(t):
            tp = jnp.pad(t, ((0, 0), (Q_BLOCK, 0), (0, 0), (0, 0))).reshape(n, nb + 1, Q_BLOCK, N_KV_A, HEAD_DIM_A)
            return jnp.concatenate([tp[:, :-1], tp[:, 1:]], axis=2)

        q_pos = pos0 + jnp.arange(L).reshape(nb, Q_BLOCK)
        k_pos = pos0 + jnp.arange(nb)[:, None] * Q_BLOCK + jnp.arange(-Q_BLOCK, Q_BLOCK)[None, :]
        o_a = window_attention(q.reshape(n, nb, Q_BLOCK, N_KV_A, GROUP_A, HEAD_DIM_A), band(k), band(v),
                               q_pos, k_pos, rel_bias, lw['sink_a'])
        new_k, new_v = k[:, L - w_buf:], v[:, L - w_buf:]
        conv_b0 = jnp.zeros((n, CONV_B - 1, C_B), x.dtype)
        conv_c0 = jnp.zeros((n, CONV_C - 1, QKV_C), x.dtype)
        s0 = jnp.zeros((n, N_HEADS_C, HEAD_DIM_C, HEAD_DIM_C), f32)
    else:
        k_buf, v_buf, conv_b0, conv_c0, s0 = past
        k_all = jnp.concatenate([k_buf.astype(k.dtype), k], 1)
        v_all = jnp.concatenate([v_buf.astype(v.dtype), v], 1)
        q_pos = pos0 + jnp.arange(L)
        k_pos = pos0 - w_buf + jnp.arange(w_buf + L)
        o_a = window_attention(q, k_all, v_all, q_pos, k_pos, rel_bias, lw['sink_a'])
        new_k, new_v = k_all[:, L:], v_all[:, L:]
        s0 = s0.astype(f32)
    br_a = o_a.reshape(n, L, Q_A) @ lw['w_o_a']

    u = glu_b[..., :C_B] * jax.nn.sigmoid(glu_b[..., C_B:])
    u_ext = jnp.concatenate([conv_b0.astype(u.dtype), u], 1)
    cb = causal_dwconv(u_ext, lw['w_dw_b']) + lw['b_dw_b']
    cb = jax.nn.silu(layer_norm(cb, lw['ln_b_g'], lw['ln_b_b']))
    br_b = cb @ lw['w_o_b']
    new_conv_b = u_ext[:, -(CONV_B - 1):]

    qkv_ext = jnp.concatenate([conv_c0.astype(qkv_c.dtype), qkv_c], 1)
    qkv = jax.nn.silu(causal_dwconv(qkv_ext, lw['w_conv_c'])).astype(f32).reshape(n, L, 3, N_HEADS_C, HEAD_DIM_C)
    qc = l2norm(qkv[:, :, 0]) * HEAD_DIM_C ** -0.5
    kc = l2norm(qkv[:, :, 1])
    vc = qkv[:, :, 2]
    beta = jax.nn.sigmoid(beta_c.astype(f32))
    g = -jnp.exp(lw['a_log_c'].astype(f32)) * jax.nn.softplus(a_c.astype(f32) + lw['dt_bias_c'])
    if past is None:
        o_c, new_s = chunk_gated_delta(qc, kc, vc, g, beta, s0)
    else:
        o_c, new_s = recurrent_gated_delta(qc, kc, vc, g, beta, s0)
    zc = z_c.astype(f32).reshape(n, L, N_HEADS_C, HEAD_DIM_C)
    o_c = o_c * lax.rsqrt(jnp.mean(o_c * o_c, -1, keepdims=True) + RMS_EPS) * lw['norm_c'] * jax.nn.silu(zc)
    br_c = o_c.reshape(n, L, V_C).astype(x.dtype) @ lw['w_o_c']
    new_conv_c = qkv_ext[:, -(CONV_C - 1):]

    g_a, g_b, g_c = jnp.split(jax.nn.sigmoid(gates), N_BRANCH, axis=-1)
    mix = (g_a * br_a + g_b * br_b + g_c * br_c) @ lw['w_out']
    x = layer_norm(DN_ALPHA * x + mix, lw['ln1_g'], lw['ln1_b'])

    ffn = moe_ffn(x.reshape(n * L, D_MODEL), lw['w_router'], lw['b_router'], lw['w_up'], lw['b_up'],
                  lw['w_down'], lw['b_down']).reshape(n, L, D_MODEL)
    x = layer_norm(DN_ALPHA * x + ffn, lw['ln2_g'], lw['ln2_b'])

    x = x + jax.nn.sigmoid(x @ lw['w_ple_gate']) * (pl @ lw['w_ple_proj'])
    return x, (new_k, new_v, new_conv_b, new_conv_c, new_s)


def setup_inputs(seed: int = 0) -> dict:
    key = jax.random.key(seed)
    ks = iter(list(jax.random.split(key, 48)))
    f32 = jnp.float32

    def nrm(shape, scale):
        return jax.random.normal(next(ks), shape, f32) * scale

    w_buf = min(WINDOW, PAST_LEN)
    dt = jnp.exp(jax.random.uniform(next(ks), (DEPTH, N_HEADS_C), f32, math.log(1e-3), math.log(1e-1)))
    return {
        'x_prompt': nrm((BATCH, SEQ, D_MODEL), 1.0),
        'x_sample': nrm((DEC_BATCH, DEC_SEQ, D_MODEL), 1.0),
        'cache_k_a': nrm((DEPTH, DEC_BATCH, w_buf, N_KV_A, HEAD_DIM_A), 1.0),
        'cache_v_a': nrm((DEPTH, DEC_BATCH, w_buf, N_KV_A, HEAD_DIM_A), 1.0),
        'state_conv_b': nrm((DEPTH, DEC_BATCH, CONV_B - 1, C_B), 0.5),
        'state_conv_c': nrm((DEPTH, DEC_BATCH, CONV_C - 1, QKV_C), 1.0),
        'state_s_c': nrm((DEPTH, DEC_BATCH, N_HEADS_C, HEAD_DIM_C, HEAD_DIM_C), 0.1),
        'p_prompt': nrm((DEPTH, BATCH, SEQ, PLE_DIM), 1.0),
        'p_sample': nrm((DEPTH, DEC_BATCH, DEC_SEQ, PLE_DIM), 1.0),
        'rel_bias': nrm((NUM_BUCKETS, N_HEADS_A), 0.5),
        'w_in': nrm((DEPTH, D_MODEL, N_IN), D_MODEL ** -0.5),
        'b_in': nrm((DEPTH, N_IN), 0.02),
        'sink_a': nrm((DEPTH, N_HEADS_A), 1.0),
        'w_o_a': nrm((DEPTH, Q_A, D_MODEL), Q_A ** -0.5 * DN_BETA),
        'w_dw_b': nrm((DEPTH, CONV_B, C_B), CONV_B ** -0.5),
        'b_dw_b': nrm((DEPTH, C_B), 0.02),
        'ln_b_g': 1.0 + nrm((DEPTH, C_B), 0.02),
        'ln_b_b': nrm((DEPTH, C_B), 0.02),
        'w_o_b': nrm((DEPTH, C_B, D_MODEL), C_B ** -0.5 * DN_BETA),
        'w_conv_c': nrm((DEPTH, CONV_C, QKV_C), CONV_C ** -0.5),
        'a_log_c': jnp.log(jax.random.uniform(next(ks), (DEPTH, N_HEADS_C), f32, 1.0, 16.0)),
        'dt_bias_c': dt + jnp.log(-jnp.expm1(-dt)),
        'norm_c': 1.0 + nrm((DEPTH, HEAD_DIM_C), 0.02),
        'w_o_c': nrm((DEPTH, V_C, D_MODEL), V_C ** -0.5 * DN_BETA),
        'w_out': nrm((DEPTH, D_MODEL, D_MODEL), D_MODEL ** -0.5 * DN_BETA),
        'ln1_g': 1.0 + nrm((DEPTH, D_MODEL), 0.02),
        'ln1_b': nrm((DEPTH, D_MODEL), 0.02),
        'w_router': nrm((DEPTH, D_MODEL, N_EXPERTS), D_MODEL ** -0.5),
        'b_router': nrm((DEPTH, N_EXPERTS), 0.01),
        'w_up': nrm((DEPTH, N_EXPERTS, D_MODEL, 2 * D_FF), D_MODEL ** -0.5),
        'b_up': nrm((DEPTH, N_EXPERTS, 2 * D_FF), 0.02),
        'w_down': nrm((DEPTH, N_EXPERTS, D_FF, D_MODEL), D_FF ** -0.5 * DN_BETA),
        'b_down': nrm((DEPTH, N_EXPERTS, D_MODEL), 0.02),
        'ln2_g': 1.0 + nrm((DEPTH, D_MODEL), 0.02),
        'ln2_b': nrm((DEPTH, D_MODEL), 0.02),
        'w_ple_gate': nrm((DEPTH, D_MODEL, D_MODEL), D_MODEL ** -0.5),
        'w_ple_proj': nrm((DEPTH, PLE_DIM, D_MODEL), PLE_DIM ** -0.5 * DN_BETA),
    }


def reference(x_prompt, x_sample, cache_k_a, cache_v_a, state_conv_b, state_conv_c, state_s_c,
              p_prompt, p_sample, rel_bias, w_in, b_in, sink_a, w_o_a, w_dw_b, b_dw_b, ln_b_g, ln_b_b,
              w_o_b, w_conv_c, a_log_c, dt_bias_c, norm_c, w_o_c, w_out, ln1_g, ln1_b, w_router, b_router,
              w_up, b_up, w_down, b_down, ln2_g, ln2_b, w_ple_gate, w_ple_proj):
    w_buf = cache_k_a.shape[2]
    yp, ys = x_prompt, x_sample
    st_p, st_s = [], []
    for i in range(DEPTH):
        lw = {'w_in': w_in[i], 'b_in': b_in[i], 'sink_a': sink_a[i], 'w_o_a': w_o_a[i],
              'w_dw_b': w_dw_b[i], 'b_dw_b': b_dw_b[i], 'ln_b_g': ln_b_g[i], 'ln_b_b': ln_b_b[i],
              'w_o_b': w_o_b[i], 'w_conv_c': w_conv_c[i], 'a_log_c': a_log_c[i], 'dt_bias_c': dt_bias_c[i],
              'norm_c': norm_c[i], 'w_o_c': w_o_c[i], 'w_out': w_out[i], 'ln1_g': ln1_g[i], 'ln1_b': ln1_b[i],
              'w_router': w_router[i], 'b_router': b_router[i], 'w_up': w_up[i], 'b_up': b_up[i],
              'w_down': w_down[i], 'b_down': b_down[i], 'ln2_g': ln2_g[i], 'ln2_b': ln2_b[i],
              'w_ple_gate': w_ple_gate[i], 'w_ple_proj': w_ple_proj[i]}
        yp, sp = trunk_layer(yp, p_prompt[i], rel_bias, lw, None, 0, w_buf)
        ys, ss = trunk_layer(ys, p_sample[i], rel_bias, lw,
                             (cache_k_a[i], cache_v_a[i], state_conv_b[i], state_conv_c[i], state_s_c[i]),
                             PAST_LEN, w_buf)
        st_p.append(sp)
        st_s.append(ss)
    kp, vp, cbp, ccp, scp = [jnp.stack(z) for z in zip(*st_p)]
    ks_, vs_, cbs, ccs, scs = [jnp.stack(z) for z in zip(*st_s)]
    return (yp, ys, kp, vp, cbp, ccp, scp, ks_, vs_, cbs, ccs, scs)
```

```python
import functools
import math

import numpy as np
import jax
import jax.numpy as jnp
from jax import lax
from jax.experimental import pallas as pl
from jax.experimental.pallas import tpu as pltpu

F32 = jnp.float32
BF16 = jnp.bfloat16
HIGHEST = lax.Precision.HIGHEST

D_MODEL = 1024
N_HEADS_A, N_KV_A, HEAD_DIM_A = 8, 2, 64
GROUP_A = N_HEADS_A // N_KV_A
Q_A = N_HEADS_A * HEAD_DIM_A
KV_A = N_KV_A * HEAD_DIM_A
WINDOW = 128
Q_BLOCK = 128
NUM_BUCKETS = 32
MAX_DISTANCE = 128
NEG_INF = -1e30
C_B = 512
CONV_B = 31
N_HEADS_C, HEAD_DIM_C = 4, 128
V_C = N_HEADS_C * HEAD_DIM_C
QKV_C = 3 * V_C
CONV_C = 4
DN_CHUNK = 64
N_BRANCH = 3
IN_SIZES = (Q_A, KV_A, KV_A, 2 * C_B, QKV_C, V_C, N_HEADS_C, N_HEADS_C, N_BRANCH * D_MODEL)
N_EXPERTS = 32
TOP_K = 4
D_FF = 1024
SWIGLU_LIMIT = 7.0
SWIGLU_ALPHA = 1.702
PLE_DIM = 256
LN_EPS = 1e-5
RMS_EPS = 1e-6

LANE = 128
SUBLANE = 8
VMEM_LIMIT = 56 * 1024 * 1024

COL_GATE = 0
COL_GLU = 3 * D_MODEL
COL_QKVC = COL_GLU + 2 * C_B
COL_Z = COL_QKVC + QKV_C
COL_QA = COL_Z + V_C
COL_KA = COL_QA + Q_A
COL_VA = COL_KA + KV_A
COL_SMALL = COL_VA + KV_A
N_IN_PAD = COL_SMALL + LANE
MOE_ROWS = 256
HALO_B = 32
HALO_C = 8


def _tile(n, target, mult=SUBLANE):
    best = None
    for t in range(mult, min(n, target) + 1, mult):
        if n % t == 0:
            best = t
    return best if best is not None else n


def _cparams(sem):
    return pltpu.CompilerParams(dimension_semantics=sem, vmem_limit_bytes=VMEM_LIMIT)


def _layer_norm(x, g, b):
    mu = jnp.mean(x, -1, keepdims=True)
    xc = x - mu
    var = jnp.mean(xc * xc, -1, keepdims=True)
    return xc * lax.rsqrt(var + LN_EPS) * g + b


def _sigmoid(x):
    return 1.0 / (1.0 + jnp.exp(-x))


def _mm_bias_kernel(x_ref, w_ref, b_ref, o_ref):
    o_ref[...] = jnp.dot(x_ref[...], w_ref[...], preferred_element_type=F32) + b_ref[...]


def _in_proj(x_bf, w_bf, b):
    m, k = x_bf.shape
    n = w_bf.shape[1]
    tm = _tile(m, 1536)
    tn = _tile(n, 1408, LANE)
    return pl.pallas_call(
        _mm_bias_kernel,
        grid=(n // tn, m // tm),
        in_specs=[pl.BlockSpec((tm, k), lambda j, i: (i, 0)),
                  pl.BlockSpec((k, tn), lambda j, i: (0, j)),
                  pl.BlockSpec((1, tn), lambda j, i: (0, j))],
        out_specs=pl.BlockSpec((tm, tn), lambda j, i: (i, j)),
        out_shape=jax.ShapeDtypeStruct((m, n), F32),
        compiler_params=_cparams(("arbitrary", "arbitrary")),
        name="in_proj",
    )(x_bf, w_bf, b)


def _attn_prompt_kernel(sink_ref, q_ref, kp_ref, kc_ref, vp_ref, vc_ref, bias_ref, o_ref):
    nt = (((1,), (1,)), ((), ()))
    q = (q_ref[...] * (HEAD_DIM_A ** -0.5)).astype(BF16)
    kp = kp_ref[...].astype(BF16)
    kc = kc_ref[...].astype(BF16)
    vp = vp_ref[...].astype(BF16)
    vc = vc_ref[...].astype(BF16)
    for kv in range(N_KV_A):
        sl = slice(kv * HEAD_DIM_A, (kv + 1) * HEAD_DIM_A)
        for g in range(GROUP_A):
            h = kv * GROUP_A + g
            qh = q[:, h * HEAD_DIM_A:(h + 1) * HEAD_DIM_A]
            sp = lax.dot_general(qh, kp[:, sl], nt, preferred_element_type=F32) + bias_ref[0, h, :, :Q_BLOCK]
            sc = lax.dot_general(qh, kc[:, sl], nt, preferred_element_type=F32) + bias_ref[0, h, :, Q_BLOCK:]
            sink = sink_ref[h]
            m = jnp.maximum(jnp.maximum(jnp.max(sp, -1, keepdims=True), jnp.max(sc, -1, keepdims=True)), sink)
            ep = jnp.exp(sp - m)
            ec = jnp.exp(sc - m)
            den = jnp.sum(ep, -1, keepdims=True) + jnp.sum(ec, -1, keepdims=True) + jnp.exp(sink - m)
            oh = (jnp.dot(ep.astype(BF16), vp[:, sl], preferred_element_type=F32)
                  + jnp.dot(ec.astype(BF16), vc[:, sl], preferred_element_type=F32))
            o_ref[:, h * HEAD_DIM_A:(h + 1) * HEAD_DIM_A] = (oh / den).astype(o_ref.dtype)


def _attn_prompt(h, sinks, bias_tab, nbatch, seq):
    nb = seq // Q_BLOCK
    qcol = COL_QA // Q_A
    kcol = COL_KA // KV_A
    vcol = COL_VA // KV_A

    def cur(c):
        return lambda b, i: (b * nb + i, c)

    def prev(c):
        return lambda b, i: (jnp.maximum(b * nb + i - 1, 0), c)

    return pl.pallas_call(
        _attn_prompt_kernel,
        grid=(nbatch, nb),
        in_specs=[pl.BlockSpec(memory_space=pltpu.SMEM),
                  pl.BlockSpec((Q_BLOCK, Q_A), cur(qcol)),
                  pl.BlockSpec((Q_BLOCK, KV_A), prev(kcol)),
                  pl.BlockSpec((Q_BLOCK, KV_A), cur(kcol)),
                  pl.BlockSpec((Q_BLOCK, KV_A), prev(vcol)),
                  pl.BlockSpec((Q_BLOCK, KV_A), cur(vcol)),
                  pl.BlockSpec((1, N_HEADS_A, Q_BLOCK, 2 * Q_BLOCK), lambda b, i: (jnp.minimum(i, 1), 0, 0, 0))],
        out_specs=pl.BlockSpec((Q_BLOCK, Q_A), lambda b, i: (b * nb + i, 0)),
        out_shape=jax.ShapeDtypeStruct((nbatch * seq, Q_A), BF16),
        compiler_params=_cparams(("arbitrary", "arbitrary")),
        name="attn_prompt",
    )(sinks, h, h, h, h, h, bias_tab)


def _attn_sample_kernel(q_ref, k_ref, v_ref, bias_ref, sink_ref, o_ref):
    q = (q_ref[...] * (HEAD_DIM_A ** -0.5)).astype(BF16)
    s = jnp.einsum("bqd,bkd->bqk", q, k_ref[...].astype(BF16), preferred_element_type=F32) + bias_ref[...]
    sink = sink_ref[...]
    m = jnp.maximum(jnp.max(s, -1, keepdims=True), sink)
    e = jnp.exp(s - m)
    den = jnp.sum(e, -1, keepdims=True) + jnp.exp(sink - m)
    o = jnp.einsum("bqk,bkd->bqd", e.astype(BF16), v_ref[...].astype(BF16), preferred_element_type=F32)
    o_ref[...] = o / den


def _attn_sample(q, k, v, bias, sink):
    g, r, hd = q.shape
    kk = k.shape[1]
    gb = bias.shape[0]
    return pl.pallas_call(
        _attn_sample_kernel,
        grid=(g // gb,),
        in_specs=[pl.BlockSpec((gb, r, hd), lambda i: (i, 0, 0)),
                  pl.BlockSpec((gb, kk, hd), lambda i: (i, 0, 0)),
                  pl.BlockSpec((gb, kk, hd), lambda i: (i, 0, 0)),
                  pl.BlockSpec((gb, r, kk), lambda i: (0, 0, 0)),
                  pl.BlockSpec((gb, r, 1), lambda i: (0, 0, 0))],
        out_specs=pl.BlockSpec((gb, r, hd), lambda i: (i, 0, 0)),
        out_shape=jax.ShapeDtypeStruct((g, r, hd), F32),
        compiler_params=_cparams(("arbitrary",)),
        name="attn_sample",
    )(q, k, v, bias, sink)


CONV_ROWS = 64


def _convb_prompt_kernel(a_ref, b_ref, w_ref, bias_ref, g_ref, beta_ref, cb_ref, tail_ref, ubuf, shbuf, *, tl):
    @pl.when(pl.program_id(1) == 0)
    def _():
        ubuf[0:HALO_B, :] = jnp.zeros((HALO_B, C_B), F32)

    ubuf[HALO_B:HALO_B + tl, :] = a_ref[...] * _sigmoid(b_ref[...])
    off = HALO_B - (CONV_B - 1)
    span = tl + HALO_B - SUBLANE
    for s in range(1, SUBLANE):
        shbuf[s - 1, 0:span, :] = ubuf[s:s + span, :]

    def body(r, carry):
        base = pl.multiple_of(r * CONV_ROWS, CONV_ROWS)
        accs = []
        for c in range(C_B // LANE):
            cs = slice(c * LANE, (c + 1) * LANE)
            acc = jnp.zeros((CONV_ROWS, LANE), F32)
            for w in range(CONV_B):
                d = off + w
                start = pl.multiple_of(base + (d // SUBLANE) * SUBLANE, SUBLANE)
                if d % SUBLANE == 0:
                    win = ubuf[pl.ds(start, CONV_ROWS), cs]
                else:
                    win = shbuf[d % SUBLANE - 1, pl.ds(start, CONV_ROWS), cs]
                acc = acc + win * w_ref[w:w + 1, cs]
            accs.append(acc + bias_ref[:, cs])
        y = _layer_norm(jnp.concatenate(accs, -1), g_ref[...], beta_ref[...])
        cb_ref[pl.ds(base, CONV_ROWS), :] = (y * _sigmoid(y)).astype(cb_ref.dtype)
        return carry

    lax.fori_loop(0, tl // CONV_ROWS, body, 0)
    tail = ubuf[tl:tl + HALO_B, :]
    tail_ref[0] = tail
    ubuf[0:HALO_B, :] = tail


def _convb_prompt(h, w_dw, b_dw, ln_g, ln_b, nbatch, seq):
    tl = _tile(seq, 512, CONV_ROWS)
    nt = seq // tl
    acol = COL_GLU // C_B
    return pl.pallas_call(
        functools.partial(_convb_prompt_kernel, tl=tl),
        grid=(nbatch, nt),
        in_specs=[pl.BlockSpec((tl, C_B), lambda b, t: (b * nt + t, acol)),
                  pl.BlockSpec((tl, C_B), lambda b, t: (b * nt + t, acol + 1)),
                  pl.BlockSpec((HALO_B, C_B), lambda b, t: (0, 0)),
                  pl.BlockSpec((1, C_B), lambda b, t: (0, 0)),
                  pl.BlockSpec((1, C_B), lambda b, t: (0, 0)),
                  pl.BlockSpec((1, C_B), lambda b, t: (0, 0))],
        out_specs=[pl.BlockSpec((tl, C_B), lambda b, t: (b * nt + t, 0)),
                   pl.BlockSpec((1, HALO_B, C_B), lambda b, t: (b, 0, 0))],
        out_shape=[jax.ShapeDtypeStruct((nbatch * seq, C_B), BF16),
                   jax.ShapeDtypeStruct((nbatch, HALO_B, C_B), F32)],
        scratch_shapes=[pltpu.VMEM((HALO_B + tl, C_B), F32),
                        pltpu.VMEM((SUBLANE - 1, HALO_B + tl, C_B), F32)],
        compiler_params=_cparams(("arbitrary", "arbitrary")),
        name="convb_prompt",
    )(h, h, w_dw, b_dw, ln_g, ln_b)


def _convb_sample_kernel(a_ref, b_ref, st_ref, wt_ref, bias_ref, g_ref, beta_ref, cb_ref, nst_ref, ext, *, ls, wrows):
    ns = CONV_B - 1
    ext[:, ns - ns % SUBLANE:wrows, :] = jnp.zeros((ext.shape[0], wrows - (ns - ns % SUBLANE), C_B), F32)
    ext[:, 0:ns, :] = st_ref[...]
    ext[:, ns:ns + ls, :] = a_ref[...] * _sigmoid(b_ref[...])
    nst_ref[...] = ext[:, ls:ls + ns, :]
    win = ext[...]
    for t in range(ls):
        y = jnp.sum(win * wt_ref[t][None], axis=1) + bias_ref[...]
        y = _layer_norm(y, g_ref[...], beta_ref[...])
        cb_ref[t] = (y * _sigmoid(y)).astype(cb_ref.dtype)


def _convb_sample(hs, state, wt, b_dw, ln_g, ln_b):
    s, ls, _ = hs.shape
    wrows = wt.shape[1]
    sb = _tile(s, 16, 1)
    acol = COL_GLU // C_B
    ns = CONV_B - 1
    return pl.pallas_call(
        functools.partial(_convb_sample_kernel, ls=ls, wrows=wrows),
        grid=(s // sb,),
        in_specs=[pl.BlockSpec((sb, ls, C_B), lambda i: (i, 0, acol)),
                  pl.BlockSpec((sb, ls, C_B), lambda i: (i, 0, acol + 1)),
                  pl.BlockSpec((sb, ns, C_B), lambda i: (i, 0, 0)),
                  pl.BlockSpec((ls, wrows, C_B), lambda i: (0, 0, 0)),
                  pl.BlockSpec((1, C_B), lambda i: (0, 0)),
                  pl.BlockSpec((1, C_B), lambda i: (0, 0)),
                  pl.BlockSpec((1, C_B), lambda i: (0, 0))],
        out_specs=[pl.BlockSpec((ls, sb, C_B), lambda i: (0, i, 0)),
                   pl.BlockSpec((sb, ns, C_B), lambda i: (i, 0, 0))],
        out_shape=[jax.ShapeDtypeStruct((ls, s, C_B), BF16),
                   jax.ShapeDtypeStruct((s, ns, C_B), F32)],
        scratch_shapes=[pltpu.VMEM((sb, wrows, C_B), F32)],
        compiler_params=_cparams(("arbitrary",)),
        name="convb_sample",
    )(hs, hs, state, wt, b_dw, ln_g, ln_b)


def _softplus(x):
    return jnp.maximum(x, 0.0) + jnp.log(1.0 + jnp.exp(-jnp.abs(x)))


def _dn_kernel(q_ref, k_ref, v_ref, z_ref, sm_ref, cst_ref, s0_ref, wc_ref, alog_ref, dtb_ref, nw_ref,
               o_ref, sout_ref, cbuf, sbuf, smbuf, *, c, lr):
    n = pl.program_id(1)
    hd = HEAD_DIM_C

    @pl.when(n == 0)
    def _():
        cbuf[0:HALO_C, :] = cst_ref[0]
        sbuf[...] = s0_ref[0]

    def rows(ref):
        x = ref[...]
        return x.reshape(x.shape[-2:])

    cbuf[HALO_C:HALO_C + lr, 0:V_C] = rows(q_ref)
    cbuf[HALO_C:HALO_C + lr, V_C:2 * V_C] = rows(k_ref)
    cbuf[HALO_C:HALO_C + lr, 2 * V_C:3 * V_C] = rows(v_ref)
    if lr < c:
        cbuf[HALO_C + lr:HALO_C + c, :] = jnp.zeros((c - lr, QKV_C), F32)
        smbuf[...] = jnp.zeros((c, LANE), F32)
    smbuf[0:lr, :] = rows(sm_ref)

    off = HALO_C - (CONV_C - 1)
    y = cbuf[off:off + c, :] * wc_ref[0:1, :]
    for w in range(1, CONV_C):
        y = y + cbuf[off + w:off + w + c, :] * wc_ref[w:w + 1, :]
    y = y * _sigmoid(y)
    if lr == c:
        cbuf[0:HALO_C, :] = cbuf[c:c + HALO_C, :]

    small = smbuf[...]
    beta_f = _sigmoid(small)
    g_f = -jnp.exp(alog_ref[...]) * _softplus(small + dtb_ref[...])
    if lr < c:
        live = lax.broadcasted_iota(jnp.int32, (c, LANE), 0) < lr
        beta_f = jnp.where(live, beta_f, 0.0)
        g_f = jnp.where(live, g_f, 0.0)

    ri = lax.broadcasted_iota(jnp.int32, (c, c), 0)
    ci = lax.broadcasted_iota(jnp.int32, (c, c), 1)
    tri = ri >= ci
    stri = ri > ci
    eye = jnp.where(ri == ci, 1.0, 0.0).astype(F32)
    gc_all = jnp.dot(jnp.where(tri, 1.0, 0.0).astype(F32), g_f, precision=HIGHEST, preferred_element_type=F32)
    sel = jnp.where(lax.broadcasted_iota(jnp.int32, (SUBLANE, LANE), 1)
                    == lax.broadcasted_iota(jnp.int32, (SUBLANE, LANE), 0) + N_HEADS_C, 1.0, 0.0).astype(F32)
    nt = (((1,), (1,)), ((), ()))
    gc_rows = lax.dot_general(sel, gc_all, nt, precision=HIGHEST, preferred_element_type=F32)

    for h in range(N_HEADS_C):
        hs = slice(h * hd, (h + 1) * hd)
        qh = y[:, hs]
        kh = y[:, V_C + h * hd:V_C + (h + 1) * hd]
        vh = y[:, 2 * V_C + h * hd:2 * V_C + (h + 1) * hd]
        qn = qh * lax.rsqrt(jnp.sum(qh * qh, -1, keepdims=True) + RMS_EPS) * (hd ** -0.5)
        kn = kh * lax.rsqrt(jnp.sum(kh * kh, -1, keepdims=True) + RMS_EPS)
        beta = beta_f[:, h:h + 1]
        gcol = gc_all[:, N_HEADS_C + h:N_HEADS_C + h + 1]
        grow = gc_rows[h:h + 1, :]
        glast = gcol[c - 1:c, :]
        decay = jnp.where(tri, jnp.exp(jnp.where(tri, gcol - grow, 0.0)), 0.0)
        kb = kn.astype(BF16)
        kk = lax.dot_general(kb, kb, nt, preferred_element_type=F32)
        nmat = jnp.where(stri, -(beta * kk * decay), 0.0)
        inv = eye + nmat
        npow = nmat
        for _ in range(int(math.log2(c)) - 1):
            npow = jnp.dot(npow, npow, precision=HIGHEST, preferred_element_type=F32)
            inv = inv + jnp.dot(inv, npow, precision=HIGHEST, preferred_element_type=F32)
        egc = jnp.exp(gcol)
        rhs = jnp.concatenate([vh * beta, kn * (beta * egc)], -1)
        sol = jnp.dot(inv, rhs, precision=HIGHEST, preferred_element_type=F32)
        u = sol[:, :hd]
        wm = sol[:, hd:]
        qk = jnp.where(tri, lax.dot_general(qn.astype(BF16), kb, nt, preferred_element_type=F32) * decay, 0.0)
        q_dec = qn * egc
        k_dec = kn * jnp.exp(glast - gcol)
        s_h = sbuf[h]
        s_b = s_h.astype(BF16)
        v_new = u - jnp.dot(wm.astype(BF16), s_b, preferred_element_type=F32)
        vb = v_new.astype(BF16)
        o_h = (jnp.dot(q_dec.astype(BF16), s_b, preferred_element_type=F32)
               + jnp.dot(qk.astype(BF16), vb, preferred_element_type=F32))
        sbuf[h] = s_h * jnp.exp(glast) + jnp.dot(k_dec.T.astype(BF16), vb, preferred_element_type=F32)
        on = o_h * lax.rsqrt(jnp.mean(o_h * o_h, -1, keepdims=True) + RMS_EPS) * nw_ref[...]
        zh = rows(z_ref)[:, hs]
        res = (on[0:lr, :] * (zh * _sigmoid(zh))).astype(o_ref.dtype)
        if len(o_ref.shape) == 3:
            o_ref[0, :, hs] = res
        else:
            o_ref[:, hs] = res

    @pl.when(n == pl.num_programs(1) - 1)
    def _():
        sout_ref[0] = sbuf[...]


def _deltanet(h, cstate, s0, wc, alog_row, dtb_row, nw, *, nbatch, nchunk, c, lr, three_d):
    qc = COL_QKVC // V_C
    zc = COL_Z // V_C
    smc = COL_SMALL // LANE
    if three_d:
        def spec(w, col):
            return pl.BlockSpec((1, lr, w), lambda b, n: (b, 0, col))
        o_spec = pl.BlockSpec((1, lr, V_C), lambda b, n: (b, 0, 0))
        o_shape = jax.ShapeDtypeStruct((nbatch, lr, V_C), F32)
    else:
        def spec(w, col):
            return pl.BlockSpec((lr, w), lambda b, n: (b * nchunk + n, col))
        o_spec = pl.BlockSpec((lr, V_C), lambda b, n: (b * nchunk + n, 0))
        o_shape = jax.ShapeDtypeStruct((nbatch * nchunk * lr, V_C), BF16)
    return pl.pallas_call(
        functools.partial(_dn_kernel, c=c, lr=lr),
        grid=(nbatch, nchunk),
        in_specs=[spec(V_C, qc), spec(V_C, qc + 1), spec(V_C, qc + 2), spec(V_C, zc), spec(LANE, smc),
                  pl.BlockSpec((1, HALO_C, QKV_C), lambda b, n: (b, 0, 0)),
                  pl.BlockSpec((1, N_HEADS_C, HEAD_DIM_C, HEAD_DIM_C), lambda b, n: (b, 0, 0, 0)),
                  pl.BlockSpec((SUBLANE, QKV_C), lambda b, n: (0, 0)),
                  pl.BlockSpec((1, LANE), lambda b, n: (0, 0)),
                  pl.BlockSpec((1, LANE), lambda b, n: (0, 0)),
                  pl.BlockSpec((1, HEAD_DIM_C), lambda b, n: (0, 0))],
        out_specs=[o_spec,
                   pl.BlockSpec((1, N_HEADS_C, HEAD_DIM_C, HEAD_DIM_C), lambda b, n: (b, 0, 0, 0))],
        out_shape=[o_shape,
                   jax.ShapeDtypeStruct((nbatch, N_HEADS_C, HEAD_DIM_C, HEAD_DIM_C), F32)],
        scratch_shapes=[pltpu.VMEM((HALO_C + c, QKV_C), F32),
                        pltpu.VMEM((N_HEADS_C, HEAD_DIM_C, HEAD_DIM_C), F32),
                        pltpu.VMEM((c, LANE), F32)],
        compiler_params=_cparams(("arbitrary", "arbitrary")),
        name="deltanet_3d" if three_d else "deltanet_2d",
    )(h, h, h, h, h, cstate, s0, wc, alog_row, dtb_row, nw)


def _merge_kernel(oa_ref, cb_ref, oc_ref, ga_ref, gb_ref, gc_ref, x_ref, woa_ref, wob_ref, woc_ref, wout_ref,
                  g1_ref, b1_ref, wr_ref, br_ref, x1_ref, x1b_ref, idx_ref, gate_ref, *, alpha):
    br_a = jnp.dot(oa_ref[...], woa_ref[...], preferred_element_type=F32)
    br_b = jnp.dot(cb_ref[...], wob_ref[...], preferred_element_type=F32)
    br_c = jnp.dot(oc_ref[...], woc_ref[...], preferred_element_type=F32)
    mixin = _sigmoid(ga_ref[...]) * br_a + _sigmoid(gb_ref[...]) * br_b + _sigmoid(gc_ref[...]) * br_c
    mix = jnp.dot(mixin.astype(BF16), wout_ref[...], preferred_element_type=F32)
    x1 = _layer_norm(alpha * x_ref[...] + mix, g1_ref[...], b1_ref[...])
    x1_ref[...] = x1
    x1b_ref[...] = x1.astype(BF16)
    logits = jnp.dot(x1, wr_ref[...], precision=HIGHEST, preferred_element_type=F32) + br_ref[...]
    lane = lax.broadcasted_iota(jnp.int32, logits.shape, 1)
    logits = jnp.where(lane < N_EXPERTS, logits, -jnp.inf)
    idx_out = jnp.zeros(logits.shape, jnp.int32)
    val_out = jnp.zeros(logits.shape, F32)
    top0 = None
    den = None
    for k in range(TOP_K):
        m = jnp.max(logits, -1, keepdims=True)
        sel = jnp.min(jnp.where(logits == m, lane, LANE), -1, keepdims=True)
        if k == 0:
            top0 = m
            e = jnp.ones_like(m)
            den = e
        else:
            e = jnp.exp(m - top0)
            den = den + e
        idx_out = jnp.where(lane == k, sel, idx_out)
        val_out = jnp.where(lane == k, e, val_out)
        logits = jnp.where(lane == sel, -jnp.inf, logits)
    idx_ref[...] = idx_out
    gate_ref[...] = val_out / den


def _merge(oa, cb, oc, h, x, lw, alpha):
    m = x.shape[0]
    tm = _tile(m, 512)
    row = lambda i: (i, 0)
    full = lambda i: (0, 0)
    gcol = COL_GATE // D_MODEL
    return pl.pallas_call(
        functools.partial(_merge_kernel, alpha=alpha),
        grid=(m // tm,),
        in_specs=[pl.BlockSpec((tm, Q_A), row), pl.BlockSpec((tm, C_B), row), pl.BlockSpec((tm, V_C), row),
                  pl.BlockSpec((tm, D_MODEL), lambda i: (i, gcol)),
                  pl.BlockSpec((tm, D_MODEL), lambda i: (i, gcol + 1)),
                  pl.BlockSpec((tm, D_MODEL), lambda i: (i, gcol + 2)),
                  pl.BlockSpec((tm, D_MODEL), row),
                  pl.BlockSpec((Q_A, D_MODEL), full), pl.BlockSpec((C_B, D_MODEL), full),
                  pl.BlockSpec((V_C, D_MODEL), full), pl.BlockSpec((D_MODEL, D_MODEL), full),
                  pl.BlockSpec((1, D_MODEL), full), pl.BlockSpec((1, D_MODEL), full),
                  pl.BlockSpec((D_MODEL, LANE), full), pl.BlockSpec((1, LANE), full)],
        out_specs=[pl.BlockSpec((tm, D_MODEL), row), pl.BlockSpec((tm, D_MODEL), row),
                   pl.BlockSpec((tm, LANE), row), pl.BlockSpec((tm, LANE), row)],
        out_shape=[jax.ShapeDtypeStruct((m, D_MODEL), F32), jax.ShapeDtypeStruct((m, D_MODEL), BF16),
                   jax.ShapeDtypeStruct((m, LANE), jnp.int32), jax.ShapeDtypeStruct((m, LANE), F32)],
        compiler_params=_cparams(("arbitrary",)),
        name="merge_router",
    )(oa, cb, oc, h, h, h, x, lw["w_o_a"], lw["w_o_b"], lw["w_o_c"], lw["w_out"], lw["ln1_g"], lw["ln1_b"],
      lw["w_router"], lw["b_router"])


def _moe_kernel(be_ref, na_ref, x_ref, wu_ref, bu_ref, wd_ref, bd_ref, o_ref):
    i = pl.program_id(0)

    @pl.when(i < na_ref[0])
    def _():
        hu = jnp.dot(x_ref[...], wu_ref[0, 0], preferred_element_type=F32) + bu_ref[0, 0]
        gt = jnp.minimum(hu[:, :D_FF], SWIGLU_LIMIT)
        up = jnp.clip(hu[:, D_FF:], -SWIGLU_LIMIT, SWIGLU_LIMIT)
        act = (up + 1.0) * gt * _sigmoid(SWIGLU_ALPHA * gt)
        o_ref[...] = jnp.dot(act.astype(BF16), wd_ref[0, 0], preferred_element_type=F32) + bd_ref[0, 0]

    @pl.when(i >= na_ref[0])
    def _():
        o_ref[...] = jnp.zeros(o_ref.shape, F32)


def _moe(xs, blk_e, n_active, w_up, b_up, w_down, b_down, layer):
    nblk = xs.shape[0] // MOE_ROWS

    def blk(i, be, na):
        return jnp.minimum(i, na[0] - 1)

    grid_spec = pltpu.PrefetchScalarGridSpec(
        num_scalar_prefetch=2,
        grid=(nblk,),
        in_specs=[pl.BlockSpec((MOE_ROWS, D_MODEL), lambda i, be, na: (blk(i, be, na), 0)),
                  pl.BlockSpec((1, 1, D_MODEL, 2 * D_FF), lambda i, be, na: (layer, be[blk(i, be, na)], 0, 0)),
                  pl.BlockSpec((1, 1, 1, 2 * D_FF), lambda i, be, na: (layer, be[blk(i, be, na)], 0, 0)),
                  pl.BlockSpec((1, 1, D_FF, D_MODEL), lambda i, be, na: (layer, be[blk(i, be, na)], 0, 0)),
                  pl.BlockSpec((1, 1, 1, D_MODEL), lambda i, be, na: (layer, be[blk(i, be, na)], 0, 0))],
        out_specs=pl.BlockSpec((MOE_ROWS, D_MODEL), lambda i, be, na: (i, 0)),
    )
    return pl.pallas_call(
        _moe_kernel,
        grid_spec=grid_spec,
        out_shape=jax.ShapeDtypeStruct((nblk * MOE_ROWS, D_MODEL), F32),
        compiler_params=_cparams(("arbitrary",)),
        name="moe_experts",
    )(blk_e, n_active, xs, w_up, b_up, w_down, b_down)


def _post_kernel(yg_ref, gate_ref, x1_ref, p_ref, wg_ref, wp_ref, g2_ref, b2_ref, x_ref, xb_ref, *, alpha):
    gate = gate_ref[...]
    ffn = gate[:, 0:1] * yg_ref[:, 0:D_MODEL]
    for k in range(1, TOP_K):
        ffn = ffn + gate[:, k:k + 1] * yg_ref[:, k * D_MODEL:(k + 1) * D_MODEL]
    x2 = _layer_norm(alpha * x1_ref[...] + ffn, g2_ref[...], b2_ref[...])
    gl = jnp.dot(x2.astype(BF16), wg_ref[...], preferred_element_type=F32)
    pe = jnp.dot(p_ref[...].astype(BF16), wp_ref[...], preferred_element_type=F32)
    xo = x2 + _sigmoid(gl) * pe
    x_ref[...] = xo
    xb_ref[...] = xo.astype(BF16)


def _post(yg, gate, x1, p, lw, alpha):
    m = x1.shape[0]
    tm = _tile(m, 512)
    row = lambda i: (i, 0)
    full = lambda i: (0, 0)
    return pl.pallas_call(
        functools.partial(_post_kernel, alpha=alpha),
        grid=(m // tm,),
        in_specs=[pl.BlockSpec((tm, TOP_K * D_MODEL), row), pl.BlockSpec((tm, LANE), row),
                  pl.BlockSpec((tm, D_MODEL), row), pl.BlockSpec((tm, PLE_DIM), row),
                  pl.BlockSpec((D_MODEL, D_MODEL), full), pl.BlockSpec((PLE_DIM, D_MODEL), full),
                  pl.BlockSpec((1, D_MODEL), full), pl.BlockSpec((1, D_MODEL), full)],
        out_specs=[pl.BlockSpec((tm, D_MODEL), row), pl.BlockSpec((tm, D_MODEL), row)],
        out_shape=[jax.ShapeDtypeStruct((m, D_MODEL), F32), jax.ShapeDtypeStruct((m, D_MODEL), BF16)],
        compiler_params=_cparams(("arbitrary",)),
        name="combine_ple",
    )(yg, gate, x1, p, lw["w_ple_gate"], lw["w_ple_proj"], lw["ln2_g"], lw["ln2_b"])


def _rel_bucket(n):
    max_exact = NUM_BUCKETS // 2
    nf = jnp.maximum(n, 1).astype(F32)
    large = max_exact + (jnp.log(nf / max_exact) / math.log(MAX_DISTANCE / max_exact)
                         * (NUM_BUCKETS - max_exact)).astype(jnp.int32)
    return jnp.where(n < max_exact, jnp.maximum(n, 0), jnp.minimum(large, NUM_BUCKETS - 1))


def _bias_table(rel_bias, dist, valid):
    b = rel_bias.astype(F32)[_rel_bucket(dist)]
    b = jnp.where(valid[..., None], b, NEG_INF)
    return jnp.moveaxis(b, -1, 0)


def _permute_in(w):
    o = np.concatenate([[0], np.cumsum(IN_SIZES)])
    seg = lambda i: w[..., o[i]:o[i + 1]]
    small = jnp.concatenate([seg(6), seg(7)], -1)
    small = jnp.pad(small, [(0, 0)] * (w.ndim - 1) + [(0, LANE - 2 * N_HEADS_C)])
    return jnp.concatenate([seg(8), seg(3), seg(4), seg(5), seg(0), seg(1), seg(2), small], -1)


def _route(idx, n_tok):
    a = n_tok * TOP_K
    e_flat = idx.reshape(a)
    order = jnp.argsort(e_flat, stable=True)
    e_sorted = e_flat[order]
    counts = jnp.bincount(e_flat, length=N_EXPERTS)
    starts = jnp.cumsum(counts) - counts
    padded = (counts + MOE_ROWS - 1) // MOE_ROWS * MOE_ROWS
    pad_end = jnp.cumsum(padded)
    dest = ((pad_end - padded)[e_sorted] + jnp.arange(a) - starts[e_sorted]).astype(jnp.int32)
    n_blk = -(-a // MOE_ROWS) + N_EXPERTS
    rows = jnp.full((n_blk * MOE_ROWS,), n_tok, jnp.int32).at[dest].set((order // TOP_K).astype(jnp.int32))
    blk_e = jnp.minimum(jnp.searchsorted(pad_end, jnp.arange(n_blk) * MOE_ROWS, side="right"),
                        N_EXPERTS - 1).astype(jnp.int32)
    n_active = (pad_end[-1:] // MOE_ROWS).astype(jnp.int32)
    slot = jnp.zeros((a,), jnp.int32).at[order].set(dest)
    return rows, blk_e, n_active, slot


def kernel(x_prompt, x_sample, cache_k_a, cache_v_a, state_conv_b, state_conv_c, state_s_c, p_prompt, p_sample,
           rel_bias, w_in, b_in, sink_a, w_o_a, w_dw_b, b_dw_b, ln_b_g, ln_b_b, w_o_b, w_conv_c, a_log_c,
           dt_bias_c, norm_c, w_o_c, w_out, ln1_g, ln1_b, w_router, b_router, w_up, b_up, w_down, b_down,
           ln2_g, ln2_b, w_ple_gate, w_ple_proj):
    depth = w_in.shape[0]
    nb_p, seq, _ = x_prompt.shape
    ns, ls, _ = x_sample.shape
    w_buf = cache_k_a.shape[2]
    alpha = (2 * depth) ** 0.25
    tp = nb_p * seq
    tsm = ns * ls
    n_tok = tp + tsm
    assert seq % DN_CHUNK == 0 and seq % Q_BLOCK == 0 and w_buf == WINDOW and ls <= 16

    w_in_p = _permute_in(w_in).astype(BF16)
    b_in_p = _permute_in(b_in)[:, None, :]
    w_o_a_b, w_o_b_b, w_o_c_b, w_out_b = (t.astype(BF16) for t in (w_o_a, w_o_b, w_o_c, w_out))
    w_ple_gate_b, w_ple_proj_b = w_ple_gate.astype(BF16), w_ple_proj.astype(BF16)
    w_up_b, w_down_b = w_up.astype(BF16), w_down.astype(BF16)
    b_up4, b_down4 = b_up[:, :, None, :], b_down[:, :, None, :]
    w_router_p = jnp.pad(w_router, ((0, 0), (0, 0), (0, LANE - N_EXPERTS)))
    b_router_p = jnp.pad(b_router, ((0, 0), (0, LANE - N_EXPERTS)))[:, None, :]
    w_dw_p = jnp.pad(w_dw_b, ((0, 0), (0, HALO_B - CONV_B), (0, 0)))
    w_cc_p = jnp.pad(w_conv_c, ((0, 0), (0, SUBLANE - CONV_C), (0, 0)))
    lane_pad = lambda t: jnp.pad(t, ((0, 0), (N_HEADS_C, LANE - 2 * N_HEADS_C)))[:, None, :]
    alog_rows, dtb_rows = lane_pad(a_log_c), lane_pad(dt_bias_c)

    qi = jnp.arange(Q_BLOCK)[:, None]
    kj = jnp.arange(2 * Q_BLOCK)[None, :]
    dist = qi - kj + Q_BLOCK
    valid = (dist >= 0) & (dist < WINDOW)
    tab_rest = _bias_table(rel_bias, dist, valid)
    tab_first = _bias_table(rel_bias, dist, valid & (kj >= Q_BLOCK))
    bias_prompt = jnp.stack([tab_first, tab_rest])

    kwin = 2 * WINDOW
    ti = jnp.arange(ls)[:, None]
    kjs = jnp.arange(kwin)[None, :]
    dist_s = w_buf + ti - kjs
    valid_s = (dist_s >= 0) & (dist_s < WINDOW) & (kjs < w_buf + ls)
    tab_s = _bias_table(rel_bias, dist_s, valid_s)
    tab_s = tab_s.reshape(N_KV_A, GROUP_A * ls, kwin)
    gb = 2 * _tile(ns, 16, 1)
    bias_s = jnp.tile(tab_s, (gb // N_KV_A, 1, 1))

    wrows = -(-(CONV_B - 1 + ls) // SUBLANE) * SUBLANE
    jj = jnp.arange(wrows)[None, :] - jnp.arange(ls)[:, None]
    tap_ok = (jj >= 0) & (jj < CONV_B)

    x_all = jnp.concatenate([x_prompt.reshape(tp, D_MODEL), x_sample.reshape(tsm, D_MODEL)], 0)
    xb_all = x_all.astype(BF16)
    p_all = jnp.concatenate([p_prompt.reshape(depth, tp, PLE_DIM), p_sample.reshape(depth, tsm, PLE_DIM)], 1)
    zero_cst = jnp.zeros((nb_p, HALO_C, QKV_C), F32)
    zero_s = jnp.zeros((nb_p, N_HEADS_C, HEAD_DIM_C, HEAD_DIM_C), F32)
    c_s = 16

    st_p, st_s = [], []
    for i in range(depth):
        lw = {"w_o_a": w_o_a_b[i], "w_o_b": w_o_b_b[i], "w_o_c": w_o_c_b[i], "w_out": w_out_b[i],
              "ln1_g": ln1_g[i][None], "ln1_b": ln1_b[i][None], "w_router": w_router_p[i],
              "b_router": b_router_p[i], "w_ple_gate": w_ple_gate_b[i], "w_ple_proj": w_ple_proj_b[i],
              "ln2_g": ln2_g[i][None], "ln2_b": ln2_b[i][None]}
        h = _in_proj(xb_all, w_in_p[i], b_in_p[i])
        hs = h[tp:].reshape(ns, ls, N_IN_PAD)

        sinks = sink_a[i].astype(F32)
        oa_p = _attn_prompt(h, sinks, bias_prompt, nb_p, seq)
        q_s = hs[:, :, COL_QA:COL_QA + Q_A].reshape(ns, ls, N_KV_A, GROUP_A, HEAD_DIM_A)
        q_s = q_s.transpose(0, 2, 3, 1, 4).reshape(ns * N_KV_A, GROUP_A * ls, HEAD_DIM_A)
        k_new = hs[:, :, COL_KA:COL_KA + KV_A].reshape(ns, ls, N_KV_A, HEAD_DIM_A)
        v_new = hs[:, :, COL_VA:COL_VA + KV_A].reshape(ns, ls, N_KV_A, HEAD_DIM_A)
        k_all = jnp.concatenate([cache_k_a[i], k_new], 1)
        v_all = jnp.concatenate([cache_v_a[i], v_new], 1)

        def keywin(t):
            t = jnp.pad(t, ((0, 0), (0, kwin - w_buf - ls), (0, 0), (0, 0)))
            return t.transpose(0, 2, 1, 3).reshape(ns * N_KV_A, kwin, HEAD_DIM_A)

        sink_s = jnp.tile(jnp.repeat(sinks.reshape(N_KV_A, GROUP_A), ls, axis=1)[:, :, None], (gb // N_KV_A, 1, 1))
        oa_s = _attn_sample(q_s, keywin(k_all), keywin(v_all), bias_s, sink_s)
        oa_s = oa_s.reshape(ns, N_KV_A, GROUP_A, ls, HEAD_DIM_A).transpose(0, 3, 1, 2, 4).reshape(tsm, Q_A)
        oa = jnp.concatenate([oa_p, oa_s.astype(BF16)], 0)

        cb_p, tail_p = _convb_prompt(h, w_dw_p[i], b_dw_b[i][None], ln_b_g[i][None], ln_b_b[i][None], nb_p, seq)
        wt = jnp.where(tap_ok[:, :, None], w_dw_b[i][jnp.clip(jj, 0, CONV_B - 1)], 0.0)
        cb_s, nconv_b_s = _convb_sample(hs, state_conv_b[i], wt, b_dw_b[i][None], ln_b_g[i][None], ln_b_b[i][None])
        cb = jnp.concatenate([cb_p, cb_s.transpose(1, 0, 2).reshape(tsm, C_B)], 0)

        nw = norm_c[i][None]
        oc_p, s_p = _deltanet(h, zero_cst, zero_s, w_cc_p[i], alog_rows[i], dtb_rows[i], nw,
                              nbatch=nb_p, nchunk=seq // DN_CHUNK, c=DN_CHUNK, lr=DN_CHUNK, three_d=False)
        cst_s = jnp.pad(state_conv_c[i], ((0, 0), (HALO_C - (CONV_C - 1), 0), (0, 0)))
        oc_s, s_s = _deltanet(hs, cst_s, state_s_c[i].astype(F32), w_cc_p[i], alog_rows[i], dtb_rows[i], nw,
                              nbatch=ns, nchunk=1, c=c_s, lr=ls, three_d=True)
        oc = jnp.concatenate([oc_p, oc_s.reshape(tsm, V_C).astype(BF16)], 0)

        x1, x1b, idx, gate = _merge(oa, cb, oc, h, x_all, lw, alpha)

        rows, blk_e, n_active, slot = _route(idx[:, :TOP_K], n_tok)
        xs = jnp.concatenate([x1b, jnp.zeros((1, D_MODEL), BF16)], 0)[rows]
        yb = _moe(xs, blk_e, n_active, w_up_b, b_up4, w_down_b, b_down4, i)
        yg = yb[slot].reshape(n_tok, TOP_K * D_MODEL)

        x_all, xb_all = _post(yg, gate, x1, p_all[i], lw, alpha)

        def tail_rows(nrows, col, width):
            return jnp.stack([h[(b + 1) * seq - nrows:(b + 1) * seq, col:col + width] for b in range(nb_p)])

        kp_new = tail_rows(w_buf, COL_KA, KV_A)
        vp_new = tail_rows(w_buf, COL_VA, KV_A)
        ccp_new = tail_rows(CONV_C - 1, COL_QKVC, QKV_C)
        st_p.append((kp_new.reshape(nb_p, w_buf, N_KV_A, HEAD_DIM_A), vp_new.reshape(nb_p, w_buf, N_KV_A, HEAD_DIM_A),
                     tail_p[:, HALO_B - (CONV_B - 1):], ccp_new, s_p))
        ccs_ext = jnp.concatenate([state_conv_c[i], hs[:, :, COL_QKVC:COL_QKVC + QKV_C]], 1)
        st_s.append((k_all[:, ls:], v_all[:, ls:], nconv_b_s, ccs_ext[:, -(CONV_C - 1):], s_s))

    yp = x_all[:tp].reshape(nb_p, seq, D_MODEL)
    ys = x_all[tp:].reshape(ns, ls, D_MODEL)
    kp, vp, cbp, ccp, scp = [jnp.stack(z) for z in zip(*st_p)]
    ks_, vs_, cbs, ccs, scs = [jnp.stack(z) for z in zip(*st_s)]
    return (yp, ys, kp, vp, cbp, ccp, scp, ks_, vs_, cbs, ccs, scs)
```

```python
import functools
import math

import numpy as np
import jax
import jax.numpy as jnp
from jax import lax
from jax.experimental import pallas as pl
from jax.experimental.pallas import tpu as pltpu

F32 = jnp.float32
BF16 = jnp.bfloat16
HIGHEST = lax.Precision.HIGHEST

D_MODEL = 1024
N_HEADS_A, N_KV_A, HEAD_DIM_A = 8, 2, 64
GROUP_A = N_HEADS_A // N_KV_A
Q_A = N_HEADS_A * HEAD_DIM_A
KV_A = N_KV_A * HEAD_DIM_A
WINDOW = 128
Q_BLOCK = 128
NUM_BUCKETS = 32
MAX_DISTANCE = 128
NEG_INF = -1e30
C_B = 512
CONV_B = 31
N_HEADS_C, HEAD_DIM_C = 4, 128
V_C = N_HEADS_C * HEAD_DIM_C
QKV_C = 3 * V_C
CONV_C = 4
DN_CHUNK = 64
N_BRANCH = 3
IN_SIZES = (Q_A, KV_A, KV_A, 2 * C_B, QKV_C, V_C, N_HEADS_C, N_HEADS_C, N_BRANCH * D_MODEL)
N_EXPERTS = 32
TOP_K = 4
D_FF = 1024
SWIGLU_LIMIT = 7.0
SWIGLU_ALPHA = 1.702
PLE_DIM = 256
LN_EPS = 1e-5
RMS_EPS = 1e-6

LANE = 128
SUBLANE = 8
VMEM_LIMIT = 56 * 1024 * 1024

COL_GATE = 0
COL_GLU = 3 * D_MODEL
COL_QKVC = COL_GLU + 2 * C_B
COL_Z = COL_QKVC + QKV_C
COL_QA = COL_Z + V_C
COL_KA = COL_QA + Q_A
COL_VA = COL_KA + KV_A
COL_SMALL = COL_VA + KV_A
N_IN_PAD = COL_SMALL + LANE
MOE_ROWS = 256
HALO_B = 32
HALO_C = 8


def _tile(n, target, mult=SUBLANE):
    best = None
    for t in range(mult, min(n, target) + 1, mult):
        if n % t == 0:
            best = t
    return best if best is not None else n


def _cparams(sem):
    return pltpu.CompilerParams(dimension_semantics=sem, vmem_limit_bytes=VMEM_LIMIT)


def _layer_norm(x, g, b):
    mu = jnp.mean(x, -1, keepdims=True)
    xc = x - mu
    var = jnp.mean(xc * xc, -1, keepdims=True)
    return xc * lax.rsqrt(var + LN_EPS) * g + b


def _sigmoid(x):
    return 1.0 / (1.0 + jnp.exp(-x))


def _mm_bias_kernel(x_ref, w_ref, b_ref, o_ref):
    o_ref[...] = jnp.dot(x_ref[...], w_ref[...], preferred_element_type=F32) + b_ref[...]


def _in_proj(x_bf, w_bf, b):
    m, k = x_bf.shape
    n = w_bf.shape[1]
    tm = _tile(m, 1536)
    tn = _tile(n, 1408, LANE)
    return pl.pallas_call(
        _mm_bias_kernel,
        grid=(n // tn, m // tm),
        in_specs=[pl.BlockSpec((tm, k), lambda j, i: (i, 0)),
                  pl.BlockSpec((k, tn), lambda j, i: (0, j)),
                  pl.BlockSpec((1, tn), lambda j, i: (0, j))],
        out_specs=pl.BlockSpec((tm, tn), lambda j, i: (i, j)),
        out_shape=jax.ShapeDtypeStruct((m, n), F32),
        compiler_params=_cparams(("arbitrary", "arbitrary")),
        name="in_proj",
    )(x_bf, w_bf, b)


def _attn_prompt_kernel(sink_ref, q_ref, kp_ref, kc_ref, vp_ref, vc_ref, bias_ref, o_ref):
    nt = (((1,), (1,)), ((), ()))
    q = (q_ref[...] * (HEAD_DIM_A ** -0.5)).astype(BF16)
    kp = kp_ref[...].astype(BF16)
    kc = kc_ref[...].astype(BF16)
    vp = vp_ref[...].astype(BF16)
    vc = vc_ref[...].astype(BF16)
    for kv in range(N_KV_A):
        sl = slice(kv * HEAD_DIM_A, (kv + 1) * HEAD_DIM_A)
        for g in range(GROUP_A):
            h = kv * GROUP_A + g
            qh = q[:, h * HEAD_DIM_A:(h + 1) * HEAD_DIM_A]
            sp = lax.dot_general(qh, kp[:, sl], nt, preferred_element_type=F32) + bias_ref[0, h, :, :Q_BLOCK]
            sc = lax.dot_general(qh, kc[:, sl], nt, preferred_element_type=F32) + bias_ref[0, h, :, Q_BLOCK:]
            sink = sink_ref[h]
            m = jnp.maximum(jnp.maximum(jnp.max(sp, -1, keepdims=True), jnp.max(sc, -1, keepdims=True)), sink)
            ep = jnp.exp(sp - m)
            ec = jnp.exp(sc - m)
            inv = 1.0 / (jnp.sum(ep, -1, keepdims=True) + jnp.sum(ec, -1, keepdims=True) + jnp.exp(sink - m))
            oh = (jnp.dot((ep * inv).astype(BF16), vp[:, sl], preferred_element_type=F32)
                  + jnp.dot((ec * inv).astype(BF16), vc[:, sl], preferred_element_type=F32))
            o_ref[:, h * HEAD_DIM_A:(h + 1) * HEAD_DIM_A] = oh.astype(o_ref.dtype)


def _attn_prompt(h, sinks, bias_tab, nbatch, seq):
    nb = seq // Q_BLOCK
    qcol = COL_QA // Q_A
    kcol = COL_KA // KV_A
    vcol = COL_VA // KV_A

    def cur(c):
        return lambda b, i: (b * nb + i, c)

    def prev(c):
        return lambda b, i: (jnp.maximum(b * nb + i - 1, 0), c)

    return pl.pallas_call(
        _attn_prompt_kernel,
        grid=(nbatch, nb),
        in_specs=[pl.BlockSpec(memory_space=pltpu.SMEM),
                  pl.BlockSpec((Q_BLOCK, Q_A), cur(qcol)),
                  pl.BlockSpec((Q_BLOCK, KV_A), prev(kcol)),
                  pl.BlockSpec((Q_BLOCK, KV_A), cur(kcol)),
                  pl.BlockSpec((Q_BLOCK, KV_A), prev(vcol)),
                  pl.BlockSpec((Q_BLOCK, KV_A), cur(vcol)),
                  pl.BlockSpec((1, N_HEADS_A, Q_BLOCK, 2 * Q_BLOCK), lambda b, i: (jnp.minimum(i, 1), 0, 0, 0))],
        out_specs=pl.BlockSpec((Q_BLOCK, Q_A), lambda b, i: (b * nb + i, 0)),
        out_shape=jax.ShapeDtypeStruct((nbatch * seq, Q_A), BF16),
        compiler_params=_cparams(("arbitrary", "arbitrary")),
        name="attn_prompt",
    )(sinks, h, h, h, h, h, bias_tab)


def _attn_sample_kernel(q_ref, k_ref, v_ref, bias_ref, sink_ref, o_ref):
    q = (q_ref[...] * (HEAD_DIM_A ** -0.5)).astype(BF16)
    s = jnp.einsum("bqd,bkd->bqk", q, k_ref[...].astype(BF16), preferred_element_type=F32) + bias_ref[...]
    sink = sink_ref[...]
    m = jnp.maximum(jnp.max(s, -1, keepdims=True), sink)
    e = jnp.exp(s - m)
    pr = e / (jnp.sum(e, -1, keepdims=True) + jnp.exp(sink - m))
    o_ref[...] = jnp.einsum("bqk,bkd->bqd", pr.astype(BF16), v_ref[...].astype(BF16), preferred_element_type=F32)


def _attn_sample(q, k, v, bias, sink):
    g, r, hd = q.shape
    kk = k.shape[1]
    gb = bias.shape[0]
    return pl.pallas_call(
        _attn_sample_kernel,
        grid=(g // gb,),
        in_specs=[pl.BlockSpec((gb, r, hd), lambda i: (i, 0, 0)),
                  pl.BlockSpec((gb, kk, hd), lambda i: (i, 0, 0)),
                  pl.BlockSpec((gb, kk, hd), lambda i: (i, 0, 0)),
                  pl.BlockSpec((gb, r, kk), lambda i: (0, 0, 0)),
                  pl.BlockSpec((gb, r, 1), lambda i: (0, 0, 0))],
        out_specs=pl.BlockSpec((gb, r, hd), lambda i: (i, 0, 0)),
        out_shape=jax.ShapeDtypeStruct((g, r, hd), F32),
        compiler_params=_cparams(("arbitrary",)),
        name="attn_sample",
    )(q, k, v, bias, sink)


CONV_ROWS = 64


def _convb_prompt_kernel(a_ref, b_ref, w_ref, bias_ref, g_ref, beta_ref, cb_ref, tail_ref, ubuf, shbuf, *, tl):
    @pl.when(pl.program_id(1) == 0)
    def _():
        ubuf[0:HALO_B, :] = jnp.zeros((HALO_B, C_B), F32)

    ubuf[HALO_B:HALO_B + tl, :] = a_ref[...] * _sigmoid(b_ref[...])
    off = HALO_B - (CONV_B - 1)
    span = tl + HALO_B - SUBLANE
    for s in range(1, SUBLANE):
        shbuf[s - 1, 0:span, :] = ubuf[s:s + span, :]

    def body(r, carry):
        base = pl.multiple_of(r * CONV_ROWS, CONV_ROWS)
        accs = []
        for c in range(C_B // LANE):
            cs = slice(c * LANE, (c + 1) * LANE)
            acc = jnp.zeros((CONV_ROWS, LANE), F32)
            for w in range(CONV_B):
                d = off + w
                start = pl.multiple_of(base + (d // SUBLANE) * SUBLANE, SUBLANE)
                if d % SUBLANE == 0:
                    win = ubuf[pl.ds(start, CONV_ROWS), cs]
                else:
                    win = shbuf[d % SUBLANE - 1, pl.ds(start, CONV_ROWS), cs]
                acc = acc + win * w_ref[w:w + 1, cs]
            accs.append(acc + bias_ref[:, cs])
        y = _layer_norm(jnp.concatenate(accs, -1), g_ref[...], beta_ref[...])
        cb_ref[pl.ds(base, CONV_ROWS), :] = (y * _sigmoid(y)).astype(cb_ref.dtype)
        return carry

    lax.fori_loop(0, tl // CONV_ROWS, body, 0)
    tail = ubuf[tl:tl + HALO_B, :]
    tail_ref[0] = tail
    ubuf[0:HALO_B, :] = tail


def _convb_prompt(h, w_dw, b_dw, ln_g, ln_b, nbatch, seq):
    tl = _tile(seq, 512, CONV_ROWS)
    nt = seq // tl
    acol = COL_GLU // C_B
    return pl.pallas_call(
        functools.partial(_convb_prompt_kernel, tl=tl),
        grid=(nbatch, nt),
        in_specs=[pl.BlockSpec((tl, C_B), lambda b, t: (b * nt + t, acol)),
                  pl.BlockSpec((tl, C_B), lambda b, t: (b * nt + t, acol + 1)),
                  pl.BlockSpec((HALO_B, C_B), lambda b, t: (0, 0)),
                  pl.BlockSpec((1, C_B), lambda b, t: (0, 0)),
                  pl.BlockSpec((1, C_B), lambda b, t: (0, 0)),
                  pl.BlockSpec((1, C_B), lambda b, t: (0, 0))],
        out_specs=[pl.BlockSpec((tl, C_B), lambda b, t: (b * nt + t, 0)),
                   pl.BlockSpec((1, HALO_B, C_B), lambda b, t: (b, 0, 0))],
        out_shape=[jax.ShapeDtypeStruct((nbatch * seq, C_B), BF16),
                   jax.ShapeDtypeStruct((nbatch, HALO_B, C_B), F32)],
        scratch_shapes=[pltpu.VMEM((HALO_B + tl, C_B), F32),
                        pltpu.VMEM((SUBLANE - 1, HALO_B + tl, C_B), F32)],
        compiler_params=_cparams(("arbitrary", "arbitrary")),
        name="convb_prompt",
    )(h, h, w_dw, b_dw, ln_g, ln_b)


def _convb_sample_kernel(a_ref, b_ref, st_ref, wt_ref, bias_ref, g_ref, beta_ref, cb_ref, nst_ref, ext, *, ls, wrows):
    ns = CONV_B - 1
    ext[:, ns - ns % SUBLANE:wrows, :] = jnp.zeros((ext.shape[0], wrows - (ns - ns % SUBLANE), C_B), F32)
    ext[:, 0:ns, :] = st_ref[...]
    ext[:, ns:ns + ls, :] = a_ref[...] * _sigmoid(b_ref[...])
    nst_ref[...] = ext[:, ls:ls + ns, :]
    win = ext[...]
    for t in range(ls):
        y = jnp.sum(win * wt_ref[t][None], axis=1) + bias_ref[...]
        y = _layer_norm(y, g_ref[...], beta_ref[...])
        cb_ref[t] = (y * _sigmoid(y)).astype(cb_ref.dtype)


def _convb_sample(hs, state, wt, b_dw, ln_g, ln_b):
    s, ls, _ = hs.shape
    wrows = wt.shape[1]
    sb = _tile(s, 16, 1)
    acol = COL_GLU // C_B
    ns = CONV_B - 1
    return pl.pallas_call(
        functools.partial(_convb_sample_kernel, ls=ls, wrows=wrows),
        grid=(s // sb,),
        in_specs=[pl.BlockSpec((sb, ls, C_B), lambda i: (i, 0, acol)),
                  pl.BlockSpec((sb, ls, C_B), lambda i: (i, 0, acol + 1)),
                  pl.BlockSpec((sb, ns, C_B), lambda i: (i, 0, 0)),
                  pl.BlockSpec((ls, wrows, C_B), lambda i: (0, 0, 0)),
                  pl.BlockSpec((1, C_B), lambda i: (0, 0)),
                  pl.BlockSpec((1, C_B), lambda i: (0, 0)),
                  pl.BlockSpec((1, C_B), lambda i: (0, 0))],
        out_specs=[pl.BlockSpec((ls, sb, C_B), lambda i: (0, i, 0)),
                   pl.BlockSpec((sb, ns, C_B), lambda i: (i, 0, 0))],
        out_shape=[jax.ShapeDtypeStruct((ls, s, C_B), BF16),
                   jax.ShapeDtypeStruct((s, ns, C_B), F32)],
        scratch_shapes=[pltpu.VMEM((sb, wrows, C_B), F32)],
        compiler_params=_cparams(("arbitrary",)),
        name="convb_sample",
    )(hs, hs, state, wt, b_dw, ln_g, ln_b)


def _softplus(x):
    return jnp.maximum(x, 0.0) + jnp.log(1.0 + jnp.exp(-jnp.abs(x)))


def _dn_kernel(q_ref, k_ref, v_ref, z_ref, sm_ref, cst_ref, s0_ref, wc_ref, alog_ref, dtb_ref, nw_ref,
               o_ref, sout_ref, cbuf, sbuf, smbuf, *, c, lr):
    n = pl.program_id(1)
    hd = HEAD_DIM_C

    @pl.when(n == 0)
    def _():
        cbuf[0:HALO_C, :] = cst_ref[0]
        sbuf[...] = s0_ref[0]

    def rows(ref):
        x = ref[...]
        return x.reshape(x.shape[-2:])

    cbuf[HALO_C:HALO_C + lr, 0:V_C] = rows(q_ref)
    cbuf[HALO_C:HALO_C + lr, V_C:2 * V_C] = rows(k_ref)
    cbuf[HALO_C:HALO_C + lr, 2 * V_C:3 * V_C] = rows(v_ref)
    if lr < c:
        cbuf[HALO_C + lr:HALO_C + c, :] = jnp.zeros((c - lr, QKV_C), F32)
        smbuf[...] = jnp.zeros((c, LANE), F32)
    smbuf[0:lr, :] = rows(sm_ref)

    off = HALO_C - (CONV_C - 1)
    y = cbuf[off:off + c, :] * wc_ref[0:1, :]
    for w in range(1, CONV_C):
        y = y + cbuf[off + w:off + w + c, :] * wc_ref[w:w + 1, :]
    y = y * _sigmoid(y)
    if lr == c:
        cbuf[0:HALO_C, :] = cbuf[c:c + HALO_C, :]

    small = smbuf[...]
    beta_f = _sigmoid(small)
    g_f = -jnp.exp(alog_ref[...]) * _softplus(small + dtb_ref[...])
    if lr < c:
        live = lax.broadcasted_iota(jnp.int32, (c, LANE), 0) < lr
        beta_f = jnp.where(live, beta_f, 0.0)
        g_f = jnp.where(live, g_f, 0.0)

    ri = lax.broadcasted_iota(jnp.int32, (c, c), 0)
    ci = lax.broadcasted_iota(jnp.int32, (c, c), 1)
    tri = ri >= ci
    stri = ri > ci
    eye = jnp.where(ri == ci, 1.0, 0.0).astype(F32)
    gc_all = jnp.dot(jnp.where(tri, 1.0, 0.0).astype(F32), g_f, precision=HIGHEST, preferred_element_type=F32)
    sel = jnp.where(lax.broadcasted_iota(jnp.int32, (SUBLANE, LANE), 1)
                    == lax.broadcasted_iota(jnp.int32, (SUBLANE, LANE), 0) + N_HEADS_C, 1.0, 0.0).astype(F32)
    nt = (((1,), (1,)), ((), ()))
    gc_rows = lax.dot_general(sel, gc_all, nt, precision=HIGHEST, preferred_element_type=F32)

    for h in range(N_HEADS_C):
        hs = slice(h * hd, (h + 1) * hd)
        qh = y[:, hs]
        kh = y[:, V_C + h * hd:V_C + (h + 1) * hd]
        vh = y[:, 2 * V_C + h * hd:2 * V_C + (h + 1) * hd]
        qn = qh * lax.rsqrt(jnp.sum(qh * qh, -1, keepdims=True) + RMS_EPS) * (hd ** -0.5)
        kn = kh * lax.rsqrt(jnp.sum(kh * kh, -1, keepdims=True) + RMS_EPS)
        beta = beta_f[:, h:h + 1]
        gcol = gc_all[:, N_HEADS_C + h:N_HEADS_C + h + 1]
        grow = gc_rows[h:h + 1, :]
        glast = gcol[c - 1:c, :]
        decay = jnp.where(tri, jnp.exp(jnp.where(tri, gcol - grow, 0.0)), 0.0)
        kb = kn.astype(BF16)
        kk = lax.dot_general(kb, kb, nt, preferred_element_type=F32)
        nmat = jnp.where(stri, -(beta * kk * decay), 0.0)
        inv = eye + nmat
        npow = nmat
        for _ in range(int(math.log2(c)) - 1):
            npow = jnp.dot(npow, npow, precision=HIGHEST, preferred_element_type=F32)
            inv = inv + jnp.dot(inv, npow, precision=HIGHEST, preferred_element_type=F32)
        egc = jnp.exp(gcol)
        rhs = jnp.concatenate([vh * beta, kn * (beta * egc)], -1)
        sol = jnp.dot(inv, rhs, precision=HIGHEST, preferred_element_type=F32)
        u = sol[:, :hd]
        wm = sol[:, hd:]
        qk = jnp.where(tri, lax.dot_general(qn.astype(BF16), kb, nt, preferred_element_type=F32) * decay, 0.0)
        q_dec = qn * egc
        k_dec = kn * jnp.exp(glast - gcol)
        s_h = sbuf[h]
        s_b = s_h.astype(BF16)
        v_new = u - jnp.dot(wm.astype(BF16), s_b, preferred_element_type=F32)
        vb = v_new.astype(BF16)
        o_h = (jnp.dot(q_dec.astype(BF16), s_b, preferred_element_type=F32)
               + jnp.dot(qk.astype(BF16), vb, preferred_element_type=F32))
        sbuf[h] = s_h * jnp.exp(glast) + jnp.dot(k_dec.T.astype(BF16), vb, preferred_element_type=F32)
        on = o_h * lax.rsqrt(jnp.mean(o_h * o_h, -1, keepdims=True) + RMS_EPS) * nw_ref[...]
        zh = rows(z_ref)[:, hs]
        res = (on[0:lr, :] * (zh * _sigmoid(zh))).astype(o_ref.dtype)
        if len(o_ref.shape) == 3:
            o_ref[0, :, hs] = res
        else:
            o_ref[:, hs] = res

    @pl.when(n == pl.num_programs(1) - 1)
    def _():
        sout_ref[0] = sbuf[...]


def _deltanet(h, cstate, s0, wc, alog_row, dtb_row, nw, *, nbatch, nchunk, c, lr, three_d):
    qc = COL_QKVC // V_C
    zc = COL_Z // V_C
    smc = COL_SMALL // LANE
    if three_d:
        def spec(w, col):
            return pl.BlockSpec((1, lr, w), lambda b, n: (b, 0, col))
        o_spec = pl.BlockSpec((1, lr, V_C), lambda b, n: (b, 0, 0))
        o_shape = jax.ShapeDtypeStruct((nbatch, lr, V_C), F32)
    else:
        def spec(w, col):
            return pl.BlockSpec((lr, w), lambda b, n: (b * nchunk + n, col))
        o_spec = pl.BlockSpec((lr, V_C), lambda b, n: (b * nchunk + n, 0))
        o_shape = jax.ShapeDtypeStruct((nbatch * nchunk * lr, V_C), BF16)
    return pl.pallas_call(
        functools.partial(_dn_kernel, c=c, lr=lr),
        grid=(nbatch, nchunk),
        in_specs=[spec(V_C, qc), spec(V_C, qc + 1), spec(V_C, qc + 2), spec(V_C, zc), spec(LANE, smc),
                  pl.BlockSpec((1, HALO_C, QKV_C), lambda b, n: (b, 0, 0)),
                  pl.BlockSpec((1, N_HEADS_C, HEAD_DIM_C, HEAD_DIM_C), lambda b, n: (b, 0, 0, 0)),
                  pl.BlockSpec((SUBLANE, QKV_C), lambda b, n: (0, 0)),
                  pl.BlockSpec((1, LANE), lambda b, n: (0, 0)),
                  pl.BlockSpec((1, LANE), lambda b, n: (0, 0)),
                  pl.BlockSpec((1, HEAD_DIM_C), lambda b, n: (0, 0))],
        out_specs=[o_spec,
                   pl.BlockSpec((1, N_HEADS_C, HEAD_DIM_C, HEAD_DIM_C), lambda b, n: (b, 0, 0, 0))],
        out_shape=[o_shape,
                   jax.ShapeDtypeStruct((nbatch, N_HEADS_C, HEAD_DIM_C, HEAD_DIM_C), F32)],
        scratch_shapes=[pltpu.VMEM((HALO_C + c, QKV_C), F32),
                        pltpu.VMEM((N_HEADS_C, HEAD_DIM_C, HEAD_DIM_C), F32),
                        pltpu.VMEM((c, LANE), F32)],
        compiler_params=_cparams(("arbitrary", "arbitrary")),
        name="deltanet_3d" if three_d else "deltanet_2d",
    )(h, h, h, h, h, cstate, s0, wc, alog_row, dtb_row, nw)


DN_UNIT = DN_CHUNK
DN_STACK = N_HEADS_C * DN_UNIT
DN_GL_ROWS = 16


def _stack_heads(x, first=0):
    return jnp.concatenate([x[:, first + h * HEAD_DIM_C:first + (h + 1) * HEAD_DIM_C] for h in range(N_HEADS_C)], 0)


def _stack_cols(x, first):
    return jnp.concatenate([x[:, first + h:first + h + 1] for h in range(N_HEADS_C)], 0)


def _dn_prep_kernel(q_ref, k_ref, v_ref, qp_ref, kp_ref, vp_ref, sm_ref, wc_ref, alog_ref, dtb_ref,
                    u_ref, w_ref, qd_ref, kdt_ref, qk_ref, gl_ref, cbuf, *, c, lr):
    nt = (((1,), (1,)), ((), ()))
    halo = jnp.concatenate([qp_ref[...], kp_ref[...], vp_ref[...]], -1)
    cbuf[0:HALO_C, :] = jnp.where(pl.program_id(1) == 0, 0.0, halo)
    cbuf[HALO_C:HALO_C + DN_UNIT, 0:V_C] = q_ref[...]
    cbuf[HALO_C:HALO_C + DN_UNIT, V_C:2 * V_C] = k_ref[...]
    cbuf[HALO_C:HALO_C + DN_UNIT, 2 * V_C:3 * V_C] = v_ref[...]
    off = HALO_C - (CONV_C - 1)
    y = cbuf[off:off + DN_UNIT, :] * wc_ref[0:1, :]
    for w in range(1, CONV_C):
        y = y + cbuf[off + w:off + w + DN_UNIT, :] * wc_ref[w:w + 1, :]
    y = y * _sigmoid(y)
    qs = _stack_heads(y, 0)
    ks = _stack_heads(y, V_C)
    vs = _stack_heads(y, 2 * V_C)
    qn = qs * lax.rsqrt(jnp.sum(qs * qs, -1, keepdims=True) + RMS_EPS) * (HEAD_DIM_C ** -0.5)
    kn = ks * lax.rsqrt(jnp.sum(ks * ks, -1, keepdims=True) + RMS_EPS)

    small = sm_ref[...]
    beta_f = _sigmoid(small)
    g_f = -jnp.exp(alog_ref[...]) * _softplus(small + dtb_ref[...])
    if lr < c:
        live = lax.broadcasted_iota(jnp.int32, (DN_UNIT, LANE), 0) % c >= c - lr
        beta_f = jnp.where(live, beta_f, 0.0)
        g_f = jnp.where(live, g_f, 0.0)

    r2 = lax.broadcasted_iota(jnp.int32, (2 * DN_UNIT, DN_UNIT), 0)
    c2 = lax.broadcasted_iota(jnp.int32, (2 * DN_UNIT, DN_UNIT), 1)
    rr = jnp.where(r2 >= DN_UNIT, r2 - DN_UNIT, r2)
    same2 = (rr // c) == (c2 // c)
    summer = jnp.where(same2 & ((r2 >= DN_UNIT) | (rr >= c2)), 1.0, 0.0).astype(F32)
    cg = jnp.dot(summer, g_f, precision=HIGHEST, preferred_element_type=F32)
    beta_s = _stack_cols(beta_f, 0)
    gcs = _stack_cols(cg[0:DN_UNIT], N_HEADS_C)
    gls = _stack_cols(cg[DN_UNIT:2 * DN_UNIT], N_HEADS_C)
    grow = jnp.broadcast_to(gcs, (DN_STACK, LANE)).T[0:1, :]

    ri = lax.broadcasted_iota(jnp.int32, (DN_STACK, DN_STACK), 0)
    ci = lax.broadcasted_iota(jnp.int32, (DN_STACK, DN_STACK), 1)
    same = (ri // c) == (ci // c)
    tri = same & (ri >= ci)
    stri = same & (ri > ci)
    decay = jnp.where(tri, jnp.exp(jnp.where(tri, gcs - grow, 0.0)), 0.0)
    kb = kn.astype(BF16)
    kk = lax.dot_general(kb, kb, nt, preferred_element_type=F32)
    nmat = jnp.where(stri, -(beta_s * kk * decay), 0.0)
    inv = jnp.where(ri == ci, 1.0, 0.0).astype(F32) + nmat
    npow = nmat
    for _ in range(int(math.log2(c)) - 1):
        nb = npow.astype(BF16)
        npow = jnp.dot(nb, nb, preferred_element_type=F32)
        inv = inv + jnp.dot(inv.astype(BF16), npow.astype(BF16), preferred_element_type=F32)
    egc = jnp.exp(gcs)
    rhs = jnp.concatenate([vs * beta_s, kn * (beta_s * egc)], -1)
    sol = jnp.dot(inv.astype(BF16), rhs.astype(BF16), preferred_element_type=F32)
    u_ref[0] = sol[:, :HEAD_DIM_C]
    w_ref[0] = sol[:, HEAD_DIM_C:].astype(BF16)
    qk = jnp.where(tri, lax.dot_general(qn.astype(BF16), kb, nt, preferred_element_type=F32) * decay, 0.0)
    qk_ref[0] = qk.astype(BF16)
    qd_ref[0] = (qn * egc).astype(BF16)
    kdt_ref[0] = (kn * jnp.exp(gls - gcs)).T.astype(BF16)
    glb = jnp.broadcast_to(jnp.exp(gls), (DN_STACK, LANE))
    groups = DN_STACK // c
    gl_rows = [glb[g * c:g * c + 1, :] for g in range(groups)]
    if groups < DN_GL_ROWS:
        gl_rows.append(jnp.ones((DN_GL_ROWS - groups, LANE), F32))
    gl_ref[0] = jnp.concatenate(gl_rows, 0)


def _dn_prep(src, small_src, wc, alog_row, dtb_row, *, qcol, smcol, nbatch, nchunk, c, lr):
    nu = nbatch * nchunk
    per8 = DN_UNIT // SUBLANE

    def cur(col):
        return pl.BlockSpec((DN_UNIT, V_C), lambda b, n: (b * nchunk + n, col))

    def prev(col):
        return pl.BlockSpec((HALO_C, V_C), lambda b, n: (jnp.maximum((b * nchunk + n) * per8 - 1, 0), col))

    unit = lambda r, cdim: pl.BlockSpec((1, r, cdim), lambda b, n: (b * nchunk + n, 0, 0))
    return pl.pallas_call(
        functools.partial(_dn_prep_kernel, c=c, lr=lr),
        grid=(nbatch, nchunk),
        in_specs=[cur(qcol), cur(qcol + 1), cur(qcol + 2), prev(qcol), prev(qcol + 1), prev(qcol + 2),
                  pl.BlockSpec((DN_UNIT, LANE), lambda b, n: (b * nchunk + n, smcol)),
                  pl.BlockSpec((SUBLANE, QKV_C), lambda b, n: (0, 0)),
                  pl.BlockSpec((1, LANE), lambda b, n: (0, 0)),
                  pl.BlockSpec((1, LANE), lambda b, n: (0, 0))],
        out_specs=[unit(DN_STACK, HEAD_DIM_C), unit(DN_STACK, HEAD_DIM_C), unit(DN_STACK, HEAD_DIM_C),
                   unit(HEAD_DIM_C, DN_STACK), unit(DN_STACK, DN_STACK), unit(DN_GL_ROWS, LANE)],
        out_shape=[jax.ShapeDtypeStruct((nu, DN_STACK, HEAD_DIM_C), F32),
                   jax.ShapeDtypeStruct((nu, DN_STACK, HEAD_DIM_C), BF16),
                   jax.ShapeDtypeStruct((nu, DN_STACK, HEAD_DIM_C), BF16),
                   jax.ShapeDtypeStruct((nu, HEAD_DIM_C, DN_STACK), BF16),
                   jax.ShapeDtypeStruct((nu, DN_STACK, DN_STACK), BF16),
                   jax.ShapeDtypeStruct((nu, DN_GL_ROWS, LANE), F32)],
        scratch_shapes=[pltpu.VMEM((HALO_C + DN_UNIT, QKV_C), F32)],
        compiler_params=_cparams(("arbitrary", "arbitrary")),
        name="deltanet_prep",
    )(src, src, src, src, src, src, small_src, wc, alog_row, dtb_row)


def _dn_seq_kernel(u_ref, w_ref, qd_ref, kdt_ref, qk_ref, gl_ref, z_ref, nw_ref, s0_ref, o_ref, sout_ref, sbuf,
                   *, c, nsq):
    groups = DN_STACK // c
    hd = HEAD_DIM_C

    @pl.when(pl.program_id(1) == 0)
    def _():
        for g in range(groups):
            sbuf[g * hd:(g + 1) * hd, :] = s0_ref[g % nsq, g // nsq]

    u = u_ref[0]
    w = w_ref[0]
    qd = qd_ref[0]
    v_parts, o_parts = [], []
    for g in range(groups):
        rows = slice(g * c, (g + 1) * c)
        s_b = sbuf[g * hd:(g + 1) * hd, :].astype(BF16)
        r = jnp.dot(jnp.concatenate([w[rows], qd[rows]], 0), s_b, preferred_element_type=F32)
        v_parts.append(u[rows] - r[:c])
        o_parts.append(r[c:])
    vb = jnp.concatenate(v_parts, 0).astype(BF16)
    o = jnp.concatenate(o_parts, 0) + jnp.dot(qk_ref[0], vb, preferred_element_type=F32)

    kdt = kdt_ref[0]
    col_group = lax.broadcasted_iota(jnp.int32, (hd, DN_STACK), 1) // c
    k_big = jnp.concatenate([jnp.where(col_group == g, kdt, jnp.zeros_like(kdt)) for g in range(groups)], 0)
    gl = gl_ref[0]
    scale = jnp.concatenate([jnp.broadcast_to(gl[g:g + 1, :], (hd, LANE)) for g in range(groups)], 0)
    sbuf[...] = sbuf[...] * scale + jnp.dot(k_big, vb, preferred_element_type=F32)

    for h in range(N_HEADS_C):
        oh = o[h * DN_UNIT:(h + 1) * DN_UNIT]
        on = oh * lax.rsqrt(jnp.mean(oh * oh, -1, keepdims=True) + RMS_EPS) * nw_ref[...]
        zh = z_ref[:, h * hd:(h + 1) * hd]
        o_ref[:, h * hd:(h + 1) * hd] = (on * (zh * _sigmoid(zh))).astype(o_ref.dtype)

    @pl.when(pl.program_id(1) == pl.num_programs(1) - 1)
    def _():
        for g in range(groups):
            sout_ref[g % nsq, g // nsq] = sbuf[g * hd:(g + 1) * hd, :]


def _dn_seq(prep, z_src, nw, s0, *, zcol, nbatch, nchunk, c, nsq):
    u, w, qd, kdt, qk, gl = prep
    groups = DN_STACK // c
    unit = lambda r, cdim: pl.BlockSpec((1, r, cdim), lambda b, n: (b * nchunk + n, 0, 0))
    state = pl.BlockSpec((nsq, N_HEADS_C, HEAD_DIM_C, HEAD_DIM_C), lambda b, n: (b, 0, 0, 0))
    return pl.pallas_call(
        functools.partial(_dn_seq_kernel, c=c, nsq=nsq),
        grid=(nbatch, nchunk),
        in_specs=[unit(DN_STACK, HEAD_DIM_C), unit(DN_STACK, HEAD_DIM_C), unit(DN_STACK, HEAD_DIM_C),
                  unit(HEAD_DIM_C, DN_STACK), unit(DN_STACK, DN_STACK), unit(DN_GL_ROWS, LANE),
                  pl.BlockSpec((DN_UNIT, V_C), lambda b, n: (b * nchunk + n, zcol)),
                  pl.BlockSpec((1, HEAD_DIM_C), lambda b, n: (0, 0)),
                  state],
        out_specs=[pl.BlockSpec((DN_UNIT, V_C), lambda b, n: (b * nchunk + n, 0)), state],
        out_shape=[jax.ShapeDtypeStruct((nbatch * nchunk * DN_UNIT, V_C), BF16),
                   jax.ShapeDtypeStruct((nbatch * nsq, N_HEADS_C, HEAD_DIM_C, HEAD_DIM_C), F32)],
        scratch_shapes=[pltpu.VMEM((groups * HEAD_DIM_C, HEAD_DIM_C), F32)],
        compiler_params=_cparams(("arbitrary", "arbitrary")),
        name="deltanet_seq",
    )(u, w, qd, kdt, qk, gl, z_src, nw, s0)


def _merge_kernel(oa_ref, cb_ref, oc_ref, ga_ref, gb_ref, gc_ref, x_ref, woa_ref, wob_ref, woc_ref, wout_ref,
                  g1_ref, b1_ref, wr_ref, br_ref, x1_ref, route_ref, gate_ref, cnt_ref, carry, *, alpha):
    @pl.when(pl.program_id(0) == 0)
    def _():
        carry[...] = jnp.zeros(carry.shape, F32)

    br_a = jnp.dot(oa_ref[...], woa_ref[...], preferred_element_type=F32)
    br_b = jnp.dot(cb_ref[...], wob_ref[...], preferred_element_type=F32)
    br_c = jnp.dot(oc_ref[...], woc_ref[...], preferred_element_type=F32)
    mixin = _sigmoid(ga_ref[...]) * br_a + _sigmoid(gb_ref[...]) * br_b + _sigmoid(gc_ref[...]) * br_c
    mix = jnp.dot(mixin.astype(BF16), wout_ref[...], preferred_element_type=F32)
    x1 = _layer_norm(alpha * x_ref[...] + mix, g1_ref[...], b1_ref[...])
    x1_ref[...] = x1
    logits = jnp.dot(x1.astype(BF16), wr_ref[...], preferred_element_type=F32) + br_ref[...]
    lane = lax.broadcasted_iota(jnp.int32, logits.shape, 1)
    logits = jnp.where(lane < N_EXPERTS, logits, -jnp.inf)
    route = jnp.zeros(logits.shape, jnp.int32)
    val_out = jnp.zeros(logits.shape, F32)
    member = jnp.zeros(logits.shape, F32)
    sels = []
    top0 = None
    den = None
    for k in range(TOP_K):
        m = jnp.max(logits, -1, keepdims=True)
        sel = jnp.min(jnp.where(logits == m, lane, LANE), -1, keepdims=True)
        if k == 0:
            top0 = m
            e = jnp.ones_like(m)
            den = e
        else:
            e = jnp.exp(m - top0)
            den = den + e
        hit = lane == sel
        route = jnp.where(lane == k, sel, route)
        val_out = jnp.where(lane == k, e, val_out)
        member = jnp.where(hit, 1.0, member)
        logits = jnp.where(hit, -jnp.inf, logits)
        sels.append(sel)
    gate_ref[...] = val_out / den
    tm = logits.shape[0]
    earlier = lax.broadcasted_iota(jnp.int32, (tm, tm), 0) > lax.broadcasted_iota(jnp.int32, (tm, tm), 1)
    cum = jnp.dot(jnp.where(earlier, 1.0, 0.0).astype(BF16), member.astype(BF16),
                  preferred_element_type=F32) + carry[...]
    for k in range(TOP_K):
        rank = jnp.sum(jnp.where(lane == sels[k], cum, 0.0), -1, keepdims=True)
        route = jnp.where(lane == TOP_K + k, rank.astype(jnp.int32), route)
    route_ref[...] = route
    carry[...] = carry[...] + jnp.sum(member, axis=0, keepdims=True)
    cnt_ref[...] = carry[...]


def _merge(oa, cb, oc, h, x, lw, alpha):
    m = x.shape[0]
    tm = _tile(m, 512)
    row = lambda i: (i, 0)
    full = lambda i: (0, 0)
    gcol = COL_GATE // D_MODEL
    return pl.pallas_call(
        functools.partial(_merge_kernel, alpha=alpha),
        grid=(m // tm,),
        in_specs=[pl.BlockSpec((tm, Q_A), row), pl.BlockSpec((tm, C_B), row), pl.BlockSpec((tm, V_C), row),
                  pl.BlockSpec((tm, D_MODEL), lambda i: (i, gcol)),
                  pl.BlockSpec((tm, D_MODEL), lambda i: (i, gcol + 1)),
                  pl.BlockSpec((tm, D_MODEL), lambda i: (i, gcol + 2)),
                  pl.BlockSpec((tm, D_MODEL), row),
                  pl.BlockSpec((Q_A, D_MODEL), full), pl.BlockSpec((C_B, D_MODEL), full),
                  pl.BlockSpec((V_C, D_MODEL), full), pl.BlockSpec((D_MODEL, D_MODEL), full),
                  pl.BlockSpec((1, D_MODEL), full), pl.BlockSpec((1, D_MODEL), full),
                  pl.BlockSpec((D_MODEL, LANE), full), pl.BlockSpec((1, LANE), full)],
        out_specs=[pl.BlockSpec((tm, D_MODEL), row), pl.BlockSpec((tm, LANE), row),
                   pl.BlockSpec((tm, LANE), row), pl.BlockSpec((1, LANE), full)],
        out_shape=[jax.ShapeDtypeStruct((m, D_MODEL), F32), jax.ShapeDtypeStruct((m, LANE), jnp.int32),
                   jax.ShapeDtypeStruct((m, LANE), F32), jax.ShapeDtypeStruct((1, LANE), F32)],
        scratch_shapes=[pltpu.VMEM((1, LANE), F32)],
        compiler_params=_cparams(("arbitrary",)),
        name="merge_router",
    )(oa, cb, oc, h, h, h, x, lw["w_o_a"], lw["w_o_b"], lw["w_o_c"], lw["w_out"], lw["ln1_g"], lw["ln1_b"],
      lw["w_router"], lw["b_router"])


def _row_copy(src, src_row, dst, dst_row, sem):
    return pltpu.make_async_copy(src.at[pl.ds(src_row, 1)], dst.at[pl.ds(dst_row, 1)], sem)


def _dispatch_kernel(dest_ref, x_ref, xs_init_ref, xs_ref, sem, *, tt):
    del xs_init_ref

    def body(t, carry):
        for k in range(TOP_K):
            _row_copy(x_ref, t, xs_ref, dest_ref[t * TOP_K + k], sem).start()
        return carry

    lax.fori_loop(0, tt, body, 0, unroll=4)
    for k in range(TOP_K):
        pltpu.make_async_copy(x_ref, xs_ref.at[pl.ds(0, tt)], sem).wait()


def _dispatch(dest, x1, n_rows):
    m = x1.shape[0]
    tt = _tile(m, 512)
    xs0 = jnp.zeros((n_rows, D_MODEL), F32)
    return pl.pallas_call(
        functools.partial(_dispatch_kernel, tt=tt),
        grid=(m // tt,),
        in_specs=[pl.BlockSpec((tt * TOP_K,), lambda i: (i,), memory_space=pltpu.SMEM),
                  pl.BlockSpec((tt, D_MODEL), lambda i: (i, 0)),
                  pl.BlockSpec(memory_space=pl.ANY)],
        out_specs=pl.BlockSpec(memory_space=pl.ANY),
        out_shape=jax.ShapeDtypeStruct((n_rows, D_MODEL), F32),
        scratch_shapes=[pltpu.SemaphoreType.DMA(())],
        input_output_aliases={2: 0},
        compiler_params=pltpu.CompilerParams(dimension_semantics=("arbitrary",), vmem_limit_bytes=VMEM_LIMIT,
                                             disable_bounds_checks=True),
        name="moe_dispatch",
    )(dest, x1, xs0)


def _moe_kernel(be_ref, na_ref, x_ref, wu_ref, bu_ref, wd_ref, bd_ref, o_ref, wu_b, wd_b):
    i = pl.program_id(0)
    active = i < na_ref[0]
    new_expert = jnp.logical_or(i == 0, be_ref[i] != be_ref[jnp.maximum(i - 1, 0)])

    @pl.when(jnp.logical_and(active, new_expert))
    def _():
        wu_b[...] = wu_ref[0, 0].astype(BF16)
        wd_b[...] = wd_ref[0, 0].astype(BF16)

    @pl.when(active)
    def _():
        hu = jnp.dot(x_ref[...].astype(BF16), wu_b[...], preferred_element_type=F32) + bu_ref[0, 0]
        gt = jnp.minimum(hu[:, :D_FF], SWIGLU_LIMIT)
        up = jnp.clip(hu[:, D_FF:], -SWIGLU_LIMIT, SWIGLU_LIMIT)
        act = (up + 1.0) * gt * _sigmoid(SWIGLU_ALPHA * gt)
        o_ref[...] = jnp.dot(act.astype(BF16), wd_b[...], preferred_element_type=F32) + bd_ref[0, 0]

    @pl.when(jnp.logical_not(active))
    def _():
        o_ref[...] = jnp.zeros(o_ref.shape, F32)


def _moe(xs, blk_e, n_active, w_up, b_up, w_down, b_down, layer):
    nblk = xs.shape[0] // MOE_ROWS

    def blk(i, be, na):
        return jnp.minimum(i, na[0] - 1)

    grid_spec = pltpu.PrefetchScalarGridSpec(
        num_scalar_prefetch=2,
        grid=(nblk,),
        in_specs=[pl.BlockSpec((MOE_ROWS, D_MODEL), lambda i, be, na: (blk(i, be, na), 0)),
                  pl.BlockSpec((1, 1, D_MODEL, 2 * D_FF), lambda i, be, na: (layer, be[blk(i, be, na)], 0, 0)),
                  pl.BlockSpec((1, 1, 1, 2 * D_FF), lambda i, be, na: (layer, be[blk(i, be, na)], 0, 0)),
                  pl.BlockSpec((1, 1, D_FF, D_MODEL), lambda i, be, na: (layer, be[blk(i, be, na)], 0, 0)),
                  pl.BlockSpec((1, 1, 1, D_MODEL), lambda i, be, na: (layer, be[blk(i, be, na)], 0, 0))],
        out_specs=pl.BlockSpec((MOE_ROWS, D_MODEL), lambda i, be, na: (i, 0)),
        scratch_shapes=[pltpu.VMEM((D_MODEL, 2 * D_FF), BF16), pltpu.VMEM((D_FF, D_MODEL), BF16)],
    )
    return pl.pallas_call(
        _moe_kernel,
        grid_spec=grid_spec,
        out_shape=jax.ShapeDtypeStruct((nblk * MOE_ROWS, D_MODEL), F32),
        compiler_params=_cparams(("arbitrary",)),
        name="moe_experts",
    )(blk_e, n_active, xs, w_up, b_up, w_down, b_down)

def _post_kernel(dest_ref, yb_ref, gate_ref, x1_ref, p_ref, wg_ref, wp_ref, g2_ref, b2_ref, x_ref, xb_ref,
                 ybuf, sem, *, alpha, tm):
    def body(t, carry):
        for k in range(TOP_K):
            _row_copy(yb_ref, dest_ref[t * TOP_K + k], ybuf.at[k], t, sem).start()
        return carry

    lax.fori_loop(0, tm, body, 0, unroll=4)
    for k in range(TOP_K):
        pltpu.make_async_copy(yb_ref.at[pl.ds(0, tm)], ybuf.at[k], sem).wait()

    gate = gate_ref[...]
    ffn = gate[:, 0:1] * ybuf[0]
    for k in range(1, TOP_K):
        ffn = ffn + gate[:, k:k + 1] * ybuf[k]
    x2 = _layer_norm(alpha * x1_ref[...] + ffn, g2_ref[...], b2_ref[...])
    gl = jnp.dot(x2.astype(BF16), wg_ref[...], preferred_element_type=F32)
    pe = jnp.dot(p_ref[...].astype(BF16), wp_ref[...], preferred_element_type=F32)
    xo = x2 + _sigmoid(gl) * pe
    x_ref[...] = xo
    xb_ref[...] = xo.astype(BF16)


def _post(dest, yb, gate, x1, p, lw, alpha):
    m = x1.shape[0]
    tm = _tile(m, 512)
    row = lambda i: (i, 0)
    full = lambda i: (0, 0)
    return pl.pallas_call(
        functools.partial(_post_kernel, alpha=alpha, tm=tm),
        grid=(m // tm,),
        in_specs=[pl.BlockSpec((tm * TOP_K,), lambda i: (i,), memory_space=pltpu.SMEM),
                  pl.BlockSpec(memory_space=pl.ANY),
                  pl.BlockSpec((tm, LANE), row),
                  pl.BlockSpec((tm, D_MODEL), row), pl.BlockSpec((tm, PLE_DIM), row),
                  pl.BlockSpec((D_MODEL, D_MODEL), full), pl.BlockSpec((PLE_DIM, D_MODEL), full),
                  pl.BlockSpec((1, D_MODEL), full), pl.BlockSpec((1, D_MODEL), full)],
        out_specs=[pl.BlockSpec((tm, D_MODEL), row), pl.BlockSpec((tm, D_MODEL), row)],
        out_shape=[jax.ShapeDtypeStruct((m, D_MODEL), F32), jax.ShapeDtypeStruct((m, D_MODEL), BF16)],
        scratch_shapes=[pltpu.VMEM((TOP_K, tm, D_MODEL), F32), pltpu.SemaphoreType.DMA(())],
        compiler_params=pltpu.CompilerParams(dimension_semantics=("arbitrary",), vmem_limit_bytes=VMEM_LIMIT,
                                             disable_bounds_checks=True),
        name="combine_ple",
    )(dest, yb, gate, x1, p, lw["w_ple_gate"], lw["w_ple_proj"], lw["ln2_g"], lw["ln2_b"])


def _rel_bucket(n):
    max_exact = NUM_BUCKETS // 2
    nf = jnp.maximum(n, 1).astype(F32)
    large = max_exact + (jnp.log(nf / max_exact) / math.log(MAX_DISTANCE / max_exact)
                         * (NUM_BUCKETS - max_exact)).astype(jnp.int32)
    return jnp.where(n < max_exact, jnp.maximum(n, 0), jnp.minimum(large, NUM_BUCKETS - 1))


def _bias_table(rel_bias, dist, valid):
    b = rel_bias.astype(F32)[_rel_bucket(dist)]
    b = jnp.where(valid[..., None], b, NEG_INF)
    return jnp.moveaxis(b, -1, 0)


def _permute_in(w):
    o = np.concatenate([[0], np.cumsum(IN_SIZES)])
    seg = lambda i: w[..., o[i]:o[i + 1]]
    small = jnp.concatenate([seg(6), seg(7)], -1)
    small = jnp.pad(small, [(0, 0)] * (w.ndim - 1) + [(0, LANE - 2 * N_HEADS_C)])
    return jnp.concatenate([seg(8), seg(3), seg(4), seg(5), seg(0), seg(1), seg(2), small], -1)


def _route(route, counts, n_tok):
    n_blk = -(-n_tok * TOP_K // MOE_ROWS) + N_EXPERTS
    cnt = counts[0, :N_EXPERTS].astype(jnp.int32)
    padded = (cnt + MOE_ROWS - 1) // MOE_ROWS * MOE_ROWS
    pad_end = jnp.cumsum(padded)
    pad_start = pad_end - padded
    idx = route[:, 0:TOP_K]
    rank = route[:, TOP_K:2 * TOP_K]
    start_of = jnp.sum(jnp.where(idx[:, :, None] == jnp.arange(N_EXPERTS)[None, None, :],
                                 pad_start[None, None, :], 0), -1)
    dest = (start_of + rank).reshape(n_tok * TOP_K).astype(jnp.int32)
    blk_first = (jnp.arange(n_blk) * MOE_ROWS)[:, None]
    blk_e = jnp.minimum(jnp.sum((pad_end[None, :] <= blk_first).astype(jnp.int32), -1), N_EXPERTS - 1)
    n_active = (pad_end[-1:] // MOE_ROWS).astype(jnp.int32)
    return dest, blk_e.astype(jnp.int32), n_active, n_blk


def kernel(x_prompt, x_sample, cache_k_a, cache_v_a, state_conv_b, state_conv_c, state_s_c, p_prompt, p_sample,
           rel_bias, w_in, b_in, sink_a, w_o_a, w_dw_b, b_dw_b, ln_b_g, ln_b_b, w_o_b, w_conv_c, a_log_c,
           dt_bias_c, norm_c, w_o_c, w_out, ln1_g, ln1_b, w_router, b_router, w_up, b_up, w_down, b_down,
           ln2_g, ln2_b, w_ple_gate, w_ple_proj):
    depth = w_in.shape[0]
    nb_p, seq, _ = x_prompt.shape
    ns, ls, _ = x_sample.shape
    w_buf = cache_k_a.shape[2]
    alpha = (2 * depth) ** 0.25
    tp = nb_p * seq
    tsm = ns * ls
    n_tok = tp + tsm
    assert seq % DN_CHUNK == 0 and seq % Q_BLOCK == 0 and w_buf == WINDOW and ls <= 16

    w_in_p = _permute_in(w_in).astype(BF16)
    b_in_p = _permute_in(b_in)[:, None, :]
    w_o_a_b, w_o_b_b, w_o_c_b, w_out_b = (t.astype(BF16) for t in (w_o_a, w_o_b, w_o_c, w_out))
    w_ple_gate_b, w_ple_proj_b = w_ple_gate.astype(BF16), w_ple_proj.astype(BF16)
    b_up4, b_down4 = b_up[:, :, None, :], b_down[:, :, None, :]
    w_router_p = jnp.pad(w_router, ((0, 0), (0, 0), (0, LANE - N_EXPERTS))).astype(BF16)
    b_router_p = jnp.pad(b_router, ((0, 0), (0, LANE - N_EXPERTS)))[:, None, :]
    w_dw_p = jnp.pad(w_dw_b, ((0, 0), (0, HALO_B - CONV_B), (0, 0)))
    w_cc_p = jnp.pad(w_conv_c, ((0, 0), (0, SUBLANE - CONV_C), (0, 0)))
    lane_pad = lambda t: jnp.pad(t, ((0, 0), (N_HEADS_C, LANE - 2 * N_HEADS_C)))[:, None, :]
    alog_rows, dtb_rows = lane_pad(a_log_c), lane_pad(dt_bias_c)

    qi = jnp.arange(Q_BLOCK)[:, None]
    kj = jnp.arange(2 * Q_BLOCK)[None, :]
    dist = qi - kj + Q_BLOCK
    valid = (dist >= 0) & (dist < WINDOW)
    tab_rest = _bias_table(rel_bias, dist, valid)
    tab_first = _bias_table(rel_bias, dist, valid & (kj >= Q_BLOCK))
    bias_prompt = jnp.stack([tab_first, tab_rest])

    kwin = 2 * WINDOW
    ti = jnp.arange(ls)[:, None]
    kjs = jnp.arange(kwin)[None, :]
    dist_s = w_buf + ti - kjs
    valid_s = (dist_s >= 0) & (dist_s < WINDOW) & (kjs < w_buf + ls)
    tab_s = _bias_table(rel_bias, dist_s, valid_s)
    tab_s = tab_s.reshape(N_KV_A, GROUP_A * ls, kwin)
    gb = 2 * _tile(ns, 16, 1)
    bias_s = jnp.tile(tab_s, (gb // N_KV_A, 1, 1))

    wrows = -(-(CONV_B - 1 + ls) // SUBLANE) * SUBLANE
    jj = jnp.arange(wrows)[None, :] - jnp.arange(ls)[:, None]
    tap_ok = (jj >= 0) & (jj < CONV_B)

    x_all = jnp.concatenate([x_prompt.reshape(tp, D_MODEL), x_sample.reshape(tsm, D_MODEL)], 0)
    xb_all = x_all.astype(BF16)
    p_all = jnp.concatenate([p_prompt.reshape(depth, tp, PLE_DIM), p_sample.reshape(depth, tsm, PLE_DIM)], 1)
    zero_s = jnp.zeros((nb_p, N_HEADS_C, HEAD_DIM_C, HEAD_DIM_C), F32)
    c_s = 16
    nsq_s = DN_UNIT // c_s
    assert ls + CONV_C - 1 <= c_s and ns % nsq_s == 0

    st_p, st_s = [], []
    for i in range(depth):
        lw = {"w_o_a": w_o_a_b[i], "w_o_b": w_o_b_b[i], "w_o_c": w_o_c_b[i], "w_out": w_out_b[i],
              "ln1_g": ln1_g[i][None], "ln1_b": ln1_b[i][None], "w_router": w_router_p[i],
              "b_router": b_router_p[i], "w_ple_gate": w_ple_gate_b[i], "w_ple_proj": w_ple_proj_b[i],
              "ln2_g": ln2_g[i][None], "ln2_b": ln2_b[i][None]}
        h = _in_proj(xb_all, w_in_p[i], b_in_p[i])
        hs = h[tp:].reshape(ns, ls, N_IN_PAD)

        sinks = sink_a[i].astype(F32)
        oa_p = _attn_prompt(h, sinks, bias_prompt, nb_p, seq)
        q_s = hs[:, :, COL_QA:COL_QA + Q_A].reshape(ns, ls, N_KV_A, GROUP_A, HEAD_DIM_A)
        q_s = q_s.transpose(0, 2, 3, 1, 4).reshape(ns * N_KV_A, GROUP_A * ls, HEAD_DIM_A)
        k_new = hs[:, :, COL_KA:COL_KA + KV_A].reshape(ns, ls, N_KV_A, HEAD_DIM_A)
        v_new = hs[:, :, COL_VA:COL_VA + KV_A].reshape(ns, ls, N_KV_A, HEAD_DIM_A)
        k_all = jnp.concatenate([cache_k_a[i], k_new], 1)
        v_all = jnp.concatenate([cache_v_a[i], v_new], 1)

        def keywin(t):
            t = jnp.pad(t, ((0, 0), (0, kwin - w_buf - ls), (0, 0), (0, 0)))
            return t.transpose(0, 2, 1, 3).reshape(ns * N_KV_A, kwin, HEAD_DIM_A)

        sink_s = jnp.tile(jnp.repeat(sinks.reshape(N_KV_A, GROUP_A), ls, axis=1)[:, :, None], (gb // N_KV_A, 1, 1))
        oa_s = _attn_sample(q_s, keywin(k_all), keywin(v_all), bias_s, sink_s)
        oa_s = oa_s.reshape(ns, N_KV_A, GROUP_A, ls, HEAD_DIM_A).transpose(0, 3, 1, 2, 4).reshape(tsm, Q_A)
        oa = jnp.concatenate([oa_p, oa_s.astype(BF16)], 0)

        cb_p, tail_p = _convb_prompt(h, w_dw_p[i], b_dw_b[i][None], ln_b_g[i][None], ln_b_b[i][None], nb_p, seq)
        wt = jnp.where(tap_ok[:, :, None], w_dw_b[i][jnp.clip(jj, 0, CONV_B - 1)], 0.0)
        cb_s, nconv_b_s = _convb_sample(hs, state_conv_b[i], wt, b_dw_b[i][None], ln_b_g[i][None], ln_b_b[i][None])
        cb = jnp.concatenate([cb_p, cb_s.transpose(1, 0, 2).reshape(tsm, C_B)], 0)

        nw = norm_c[i][None]
        nchunk = seq // DN_CHUNK
        prep_p = _dn_prep(h, h, w_cc_p[i], alog_rows[i], dtb_rows[i], qcol=COL_QKVC // V_C,
                          smcol=COL_SMALL // LANE, nbatch=nb_p, nchunk=nchunk, c=DN_CHUNK, lr=DN_CHUNK)
        oc_p, s_p = _dn_seq(prep_p, h, nw, zero_s, zcol=COL_Z // V_C, nbatch=nb_p, nchunk=nchunk,
                            c=DN_CHUNK, nsq=1)
        lead = c_s - ls
        hist = jnp.concatenate([jnp.zeros((ns, lead - (CONV_C - 1), QKV_C), F32), state_conv_c[i],
                                hs[:, :, COL_QKVC:COL_QKVC + QKV_C]], 1).reshape(ns * c_s, QKV_C)
        pad_s = lambda col, wid: jnp.pad(hs[:, :, col:col + wid], ((0, 0), (lead, 0), (0, 0))).reshape(ns * c_s, wid)
        prep_s = _dn_prep(hist, pad_s(COL_SMALL, LANE), w_cc_p[i], alog_rows[i], dtb_rows[i], qcol=0, smcol=0,
                          nbatch=ns // nsq_s, nchunk=1, c=c_s, lr=ls)
        oc_s, s_s = _dn_seq(prep_s, pad_s(COL_Z, V_C), nw, state_s_c[i].astype(F32), zcol=0,
                            nbatch=ns // nsq_s, nchunk=1, c=c_s, nsq=nsq_s)
        oc = jnp.concatenate([oc_p, oc_s.reshape(ns, c_s, V_C)[:, lead:].reshape(tsm, V_C)], 0)

        x1, route, gate, counts = _merge(oa, cb, oc, h, x_all, lw, alpha)

        dest, blk_e, n_active, n_blk = _route(route, counts, n_tok)
        xs = _dispatch(dest, x1, n_blk * MOE_ROWS)
        yb = _moe(xs, blk_e, n_active, w_up, b_up4, w_down, b_down4, i)

        x_all, xb_all = _post(dest, yb, gate, x1, p_all[i], lw, alpha)

        def tail_rows(nrows, col, width):
            return jnp.stack([h[(b + 1) * seq - nrows:(b + 1) * seq, col:col + width] for b in range(nb_p)])

        kp_new = tail_rows(w_buf, COL_KA, KV_A)
        vp_new = tail_rows(w_buf, COL_VA, KV_A)
        ccp_new = tail_rows(CONV_C - 1, COL_QKVC, QKV_C)
        st_p.append((kp_new.reshape(nb_p, w_buf, N_KV_A, HEAD_DIM_A), vp_new.reshape(nb_p, w_buf, N_KV_A, HEAD_DIM_A),
                     tail_p[:, HALO_B - (CONV_B - 1):], ccp_new, s_p))
        ccs_ext = jnp.concatenate([state_conv_c[i], hs[:, :, COL_QKVC:COL_QKVC + QKV_C]], 1)
        st_s.append((k_all[:, ls:], v_all[:, ls:], nconv_b_s, ccs_ext[:, -(CONV_C - 1):], s_s))

    yp = x_all[:tp].reshape(nb_p, seq, D_MODEL)
    ys = x_all[tp:].reshape(ns, ls, D_MODEL)
    kp, vp, cbp, ccp, scp = [jnp.stack(z) for z in zip(*st_p)]
    ks_, vs_, cbs, ccs, scs = [jnp.stack(z) for z in zip(*st_s)]
    return (yp, ys, kp, vp, cbp, ccp, scp, ks_, vs_, cbs, ccs, scs)
```

```python
import functools
import math

import numpy as np
import jax
import jax.numpy as jnp
from jax import lax
from jax.experimental import pallas as pl
from jax.experimental.pallas import tpu as pltpu

F32 = jnp.float32
BF16 = jnp.bfloat16
HIGHEST = lax.Precision.HIGHEST

D_MODEL = 1024
N_HEADS_A, N_KV_A, HEAD_DIM_A = 8, 2, 64
GROUP_A = N_HEADS_A // N_KV_A
Q_A = N_HEADS_A * HEAD_DIM_A
KV_A = N_KV_A * HEAD_DIM_A
WINDOW = 128
Q_BLOCK = 128
NUM_BUCKETS = 32
MAX_DISTANCE = 128
NEG_INF = -1e30
C_B = 512
CONV_B = 31
N_HEADS_C, HEAD_DIM_C = 4, 128
V_C = N_HEADS_C * HEAD_DIM_C
QKV_C = 3 * V_C
CONV_C = 4
DN_CHUNK = 64
N_BRANCH = 3
IN_SIZES = (Q_A, KV_A, KV_A, 2 * C_B, QKV_C, V_C, N_HEADS_C, N_HEADS_C, N_BRANCH * D_MODEL)
N_EXPERTS = 32
TOP_K = 4
D_FF = 1024
SWIGLU_LIMIT = 7.0
SWIGLU_ALPHA = 1.702
PLE_DIM = 256
LN_EPS = 1e-5
RMS_EPS = 1e-6

LANE = 128
SUBLANE = 8
VMEM_LIMIT = 56 * 1024 * 1024

COL_GATE = 0
COL_GLU = 3 * D_MODEL
COL_QKVC = COL_GLU + 2 * C_B
COL_Z = COL_QKVC + QKV_C
COL_QA = COL_Z + V_C
COL_KA = COL_QA + Q_A
COL_VA = COL_KA + KV_A
COL_SMALL = COL_VA + KV_A
N_IN_PAD = COL_SMALL + LANE
MOE_ROWS = 512
HALO_B = 32
HALO_C = 8


def _tile(n, target, mult=SUBLANE):
    best = None
    for t in range(mult, min(n, target) + 1, mult):
        if n % t == 0:
            best = t
    return best if best is not None else n


def _cparams(sem):
    return pltpu.CompilerParams(dimension_semantics=sem, vmem_limit_bytes=VMEM_LIMIT)


def _layer_norm(x, g, b):
    mu = jnp.mean(x, -1, keepdims=True)
    xc = x - mu
    var = jnp.mean(xc * xc, -1, keepdims=True)
    return xc * lax.rsqrt(var + LN_EPS) * g + b


def _sigmoid(x):
    return 1.0 / (1.0 + jnp.exp(-x))


def _mm_bias_kernel(x_ref, w_ref, b_ref, o_ref):
    o_ref[...] = jnp.dot(x_ref[...], w_ref[...], preferred_element_type=F32) + b_ref[...]


def _in_proj(x_bf, w_bf, b):
    m, k = x_bf.shape
    n = w_bf.shape[1]
    tm = _tile(m, 1536)
    tn = _tile(n, 1408, LANE)
    return pl.pallas_call(
        _mm_bias_kernel,
        grid=(n // tn, m // tm),
        in_specs=[pl.BlockSpec((tm, k), lambda j, i: (i, 0)),
                  pl.BlockSpec((k, tn), lambda j, i: (0, j)),
                  pl.BlockSpec((1, tn), lambda j, i: (0, j))],
        out_specs=pl.BlockSpec((tm, tn), lambda j, i: (i, j)),
        out_shape=jax.ShapeDtypeStruct((m, n), F32),
        compiler_params=_cparams(("arbitrary", "arbitrary")),
        name="in_proj",
    )(x_bf, w_bf, b)


def _attn_prompt_kernel(sink_ref, q_ref, kp_ref, kc_ref, vp_ref, vc_ref, bias_ref, o_ref):
    nt = (((1,), (1,)), ((), ()))
    q = (q_ref[...] * (HEAD_DIM_A ** -0.5)).astype(BF16)
    lo = lax.broadcasted_iota(jnp.int32, (2 * Q_BLOCK, KV_A), 1) < HEAD_DIM_A

    def placed(prev_ref, cur_ref):
        band = jnp.concatenate([prev_ref[...], cur_ref[...]], 0)
        own = (jnp.where(lo, band, 0.0), jnp.where(lo, 0.0, band))
        moved = (pltpu.roll(own[0], HEAD_DIM_A, 1), pltpu.roll(own[1], HEAD_DIM_A, 1))
        return ((own[0].astype(BF16), moved[0].astype(BF16)), (moved[1].astype(BF16), own[1].astype(BF16)))

    kz = placed(kp_ref, kc_ref)
    vz = placed(vp_ref, vc_ref)
    pairs = []
    for pair in range(N_HEADS_A // 2):
        kv = (2 * pair) // GROUP_A
        qp = q[:, pair * LANE:(pair + 1) * LANE]
        acc = None
        for half in range(2):
            h = 2 * pair + half
            s = lax.dot_general(qp, kz[kv][half], nt, preferred_element_type=F32) + bias_ref[0, h]
            sink = sink_ref[h]
            m = jnp.maximum(jnp.max(s, -1, keepdims=True), sink)
            e = jnp.exp(s - m)
            inv = 1.0 / (jnp.sum(e, -1, keepdims=True) + jnp.exp(sink - m))
            pv = jnp.dot((e * inv).astype(BF16), vz[kv][half], preferred_element_type=F32)
            acc = pv if acc is None else acc + pv
        pairs.append(acc)
    o_ref[...] = jnp.concatenate(pairs, -1).astype(o_ref.dtype)


def _attn_prompt(h, sinks, bias_tab, nbatch, seq):
    nb = seq // Q_BLOCK
    qcol = COL_QA // Q_A
    kcol = COL_KA // KV_A
    vcol = COL_VA // KV_A

    def cur(c):
        return lambda b, i: (b * nb + i, c)

    def prev(c):
        return lambda b, i: (jnp.maximum(b * nb + i - 1, 0), c)

    return pl.pallas_call(
        _attn_prompt_kernel,
        grid=(nbatch, nb),
        in_specs=[pl.BlockSpec(memory_space=pltpu.SMEM),
                  pl.BlockSpec((Q_BLOCK, Q_A), cur(qcol)),
                  pl.BlockSpec((Q_BLOCK, KV_A), prev(kcol)),
                  pl.BlockSpec((Q_BLOCK, KV_A), cur(kcol)),
                  pl.BlockSpec((Q_BLOCK, KV_A), prev(vcol)),
                  pl.BlockSpec((Q_BLOCK, KV_A), cur(vcol)),
                  pl.BlockSpec((1, N_HEADS_A, Q_BLOCK, 2 * Q_BLOCK), lambda b, i: (jnp.minimum(i, 1), 0, 0, 0))],
        out_specs=pl.BlockSpec((Q_BLOCK, Q_A), lambda b, i: (b * nb + i, 0)),
        out_shape=jax.ShapeDtypeStruct((nbatch * seq, Q_A), BF16),
        compiler_params=_cparams(("arbitrary", "arbitrary")),
        name="attn_prompt",
    )(sinks, h, h, h, h, h, bias_tab)


def _attn_sample_kernel(q_ref, k_ref, v_ref, bias_ref, sink_ref, o_ref):
    q = (q_ref[...] * (HEAD_DIM_A ** -0.5)).astype(BF16)
    s = jnp.einsum("bqd,bkd->bqk", q, k_ref[...].astype(BF16), preferred_element_type=F32) + bias_ref[...]
    sink = sink_ref[...]
    m = jnp.maximum(jnp.max(s, -1, keepdims=True), sink)
    e = jnp.exp(s - m)
    pr = e / (jnp.sum(e, -1, keepdims=True) + jnp.exp(sink - m))
    o_ref[...] = jnp.einsum("bqk,bkd->bqd", pr.astype(BF16), v_ref[...].astype(BF16), preferred_element_type=F32)


def _attn_sample(q, k, v, bias, sink):
    g, r, hd = q.shape
    kk = k.shape[1]
    gb = bias.shape[0]
    return pl.pallas_call(
        _attn_sample_kernel,
        grid=(g // gb,),
        in_specs=[pl.BlockSpec((gb, r, hd), lambda i: (i, 0, 0)),
                  pl.BlockSpec((gb, kk, hd), lambda i: (i, 0, 0)),
                  pl.BlockSpec((gb, kk, hd), lambda i: (i, 0, 0)),
                  pl.BlockSpec((gb, r, kk), lambda i: (0, 0, 0)),
                  pl.BlockSpec((gb, r, 1), lambda i: (0, 0, 0))],
        out_specs=pl.BlockSpec((gb, r, hd), lambda i: (i, 0, 0)),
        out_shape=jax.ShapeDtypeStruct((g, r, hd), F32),
        compiler_params=_cparams(("arbitrary",)),
        name="attn_sample",
    )(q, k, v, bias, sink)


CONV_ROWS = 64


def _convb_prompt_kernel(a_ref, b_ref, w_ref, bias_ref, g_ref, beta_ref, cb_ref, tail_ref, ubuf, shbuf, *, tl):
    @pl.when(pl.program_id(1) == 0)
    def _():
        ubuf[0:HALO_B, :] = jnp.zeros((HALO_B, C_B), F32)

    ubuf[HALO_B:HALO_B + tl, :] = a_ref[...] * _sigmoid(b_ref[...])
    off = HALO_B - (CONV_B - 1)
    span = tl + HALO_B - SUBLANE
    for s in range(1, SUBLANE):
        shbuf[s - 1, 0:span, :] = ubuf[s:s + span, :]

    def body(r, carry):
        base = pl.multiple_of(r * CONV_ROWS, CONV_ROWS)
        accs = []
        for c in range(C_B // LANE):
            cs = slice(c * LANE, (c + 1) * LANE)
            acc = jnp.zeros((CONV_ROWS, LANE), F32)
            for w in range(CONV_B):
                d = off + w
                start = pl.multiple_of(base + (d // SUBLANE) * SUBLANE, SUBLANE)
                if d % SUBLANE == 0:
                    win = ubuf[pl.ds(start, CONV_ROWS), cs]
                else:
                    win = shbuf[d % SUBLANE - 1, pl.ds(start, CONV_ROWS), cs]
                acc = acc + win * w_ref[w:w + 1, cs]
            accs.append(acc + bias_ref[:, cs])
        y = _layer_norm(jnp.concatenate(accs, -1), g_ref[...], beta_ref[...])
        cb_ref[pl.ds(base, CONV_ROWS), :] = (y * _sigmoid(y)).astype(cb_ref.dtype)
        return carry

    lax.fori_loop(0, tl // CONV_ROWS, body, 0)
    tail = ubuf[tl:tl + HALO_B, :]
    tail_ref[0] = tail
    ubuf[0:HALO_B, :] = tail


def _convb_prompt(h, w_dw, b_dw, ln_g, ln_b, nbatch, seq):
    tl = _tile(seq, 512, CONV_ROWS)
    nt = seq // tl
    acol = COL_GLU // C_B
    return pl.pallas_call(
        functools.partial(_convb_prompt_kernel, tl=tl),
        grid=(nbatch, nt),
        in_specs=[pl.BlockSpec((tl, C_B), lambda b, t: (b * nt + t, acol)),
                  pl.BlockSpec((tl, C_B), lambda b, t: (b * nt + t, acol + 1)),
                  pl.BlockSpec((HALO_B, C_B), lambda b, t: (0, 0)),
                  pl.BlockSpec((1, C_B), lambda b, t: (0, 0)),
                  pl.BlockSpec((1, C_B), lambda b, t: (0, 0)),
                  pl.BlockSpec((1, C_B), lambda b, t: (0, 0))],
        out_specs=[pl.BlockSpec((tl, C_B), lambda b, t: (b * nt + t, 0)),
                   pl.BlockSpec((1, HALO_B, C_B), lambda b, t: (b, 0, 0))],
        out_shape=[jax.ShapeDtypeStruct((nbatch * seq, C_B), BF16),
                   jax.ShapeDtypeStruct((nbatch, HALO_B, C_B), F32)],
        scratch_shapes=[pltpu.VMEM((HALO_B + tl, C_B), F32),
                        pltpu.VMEM((SUBLANE - 1, HALO_B + tl, C_B), F32)],
        compiler_params=_cparams(("arbitrary", "arbitrary")),
        name="convb_prompt",
    )(h, h, w_dw, b_dw, ln_g, ln_b)


def _convb_sample_kernel(a_ref, b_ref, st_ref, wt_ref, bias_ref, g_ref, beta_ref, cb_ref, nst_ref, ext, *, ls, wrows):
    ns = CONV_B - 1
    ext[:, ns - ns % SUBLANE:wrows, :] = jnp.zeros((ext.shape[0], wrows - (ns - ns % SUBLANE), C_B), F32)
    ext[:, 0:ns, :] = st_ref[...]
    ext[:, ns:ns + ls, :] = a_ref[...] * _sigmoid(b_ref[...])
    nst_ref[...] = ext[:, ls:ls + ns, :]
    win = ext[...]
    for t in range(ls):
        y = jnp.sum(win * wt_ref[t][None], axis=1) + bias_ref[...]
        y = _layer_norm(y, g_ref[...], beta_ref[...])
        cb_ref[t] = (y * _sigmoid(y)).astype(cb_ref.dtype)


def _convb_sample(hs, state, wt, b_dw, ln_g, ln_b):
    s, ls, _ = hs.shape
    wrows = wt.shape[1]
    sb = _tile(s, 16, 1)
    acol = COL_GLU // C_B
    ns = CONV_B - 1
    return pl.pallas_call(
        functools.partial(_convb_sample_kernel, ls=ls, wrows=wrows),
        grid=(s // sb,),
        in_specs=[pl.BlockSpec((sb, ls, C_B), lambda i: (i, 0, acol)),
                  pl.BlockSpec((sb, ls, C_B), lambda i: (i, 0, acol + 1)),
                  pl.BlockSpec((sb, ns, C_B), lambda i: (i, 0, 0)),
                  pl.BlockSpec((ls, wrows, C_B), lambda i: (0, 0, 0)),
                  pl.BlockSpec((1, C_B), lambda i: (0, 0)),
                  pl.BlockSpec((1, C_B), lambda i: (0, 0)),
                  pl.BlockSpec((1, C_B), lambda i: (0, 0))],
        out_specs=[pl.BlockSpec((ls, sb, C_B), lambda i: (0, i, 0)),
                   pl.BlockSpec((sb, ns, C_B), lambda i: (i, 0, 0))],
        out_shape=[jax.ShapeDtypeStruct((ls, s, C_B), BF16),
                   jax.ShapeDtypeStruct((s, ns, C_B), F32)],
        scratch_shapes=[pltpu.VMEM((sb, wrows, C_B), F32)],
        compiler_params=_cparams(("arbitrary",)),
        name="convb_sample",
    )(hs, hs, state, wt, b_dw, ln_g, ln_b)


def _softplus(x):
    return jnp.maximum(x, 0.0) + jnp.log(1.0 + jnp.exp(-jnp.abs(x)))


def _dn_kernel(q_ref, k_ref, v_ref, z_ref, sm_ref, cst_ref, s0_ref, wc_ref, alog_ref, dtb_ref, nw_ref,
               o_ref, sout_ref, cbuf, sbuf, smbuf, *, c, lr):
    n = pl.program_id(1)
    hd = HEAD_DIM_C

    @pl.when(n == 0)
    def _():
        cbuf[0:HALO_C, :] = cst_ref[0]
        sbuf[...] = s0_ref[0]

    def rows(ref):
        x = ref[...]
        return x.reshape(x.shape[-2:])

    cbuf[HALO_C:HALO_C + lr, 0:V_C] = rows(q_ref)
    cbuf[HALO_C:HALO_C + lr, V_C:2 * V_C] = rows(k_ref)
    cbuf[HALO_C:HALO_C + lr, 2 * V_C:3 * V_C] = rows(v_ref)
    if lr < c:
        cbuf[HALO_C + lr:HALO_C + c, :] = jnp.zeros((c - lr, QKV_C), F32)
        smbuf[...] = jnp.zeros((c, LANE), F32)
    smbuf[0:lr, :] = rows(sm_ref)

    off = HALO_C - (CONV_C - 1)
    y = cbuf[off:off + c, :] * wc_ref[0:1, :]
    for w in range(1, CONV_C):
        y = y + cbuf[off + w:off + w + c, :] * wc_ref[w:w + 1, :]
    y = y * _sigmoid(y)
    if lr == c:
        cbuf[0:HALO_C, :] = cbuf[c:c + HALO_C, :]

    small = smbuf[...]
    beta_f = _sigmoid(small)
    g_f = -jnp.exp(alog_ref[...]) * _softplus(small + dtb_ref[...])
    if lr < c:
        live = lax.broadcasted_iota(jnp.int32, (c, LANE), 0) < lr
        beta_f = jnp.where(live, beta_f, 0.0)
        g_f = jnp.where(live, g_f, 0.0)

    ri = lax.broadcasted_iota(jnp.int32, (c, c), 0)
    ci = lax.broadcasted_iota(jnp.int32, (c, c), 1)
    tri = ri >= ci
    stri = ri > ci
    eye = jnp.where(ri == ci, 1.0, 0.0).astype(F32)
    gc_all = jnp.dot(jnp.where(tri, 1.0, 0.0).astype(F32), g_f, precision=HIGHEST, preferred_element_type=F32)
    sel = jnp.where(lax.broadcasted_iota(jnp.int32, (SUBLANE, LANE), 1)
                    == lax.broadcasted_iota(jnp.int32, (SUBLANE, LANE), 0) + N_HEADS_C, 1.0, 0.0).astype(F32)
    nt = (((1,), (1,)), ((), ()))
    gc_rows = lax.dot_general(sel, gc_all, nt, precision=HIGHEST, preferred_element_type=F32)

    for h in range(N_HEADS_C):
        hs = slice(h * hd, (h + 1) * hd)
        qh = y[:, hs]
        kh = y[:, V_C + h * hd:V_C + (h + 1) * hd]
        vh = y[:, 2 * V_C + h * hd:2 * V_C + (h + 1) * hd]
        qn = qh * lax.rsqrt(jnp.sum(qh * qh, -1, keepdims=True) + RMS_EPS) * (hd ** -0.5)
        kn = kh * lax.rsqrt(jnp.sum(kh * kh, -1, keepdims=True) + RMS_EPS)
        beta = beta_f[:, h:h + 1]
        gcol = gc_all[:, N_HEADS_C + h:N_HEADS_C + h + 1]
        grow = gc_rows[h:h + 1, :]
        glast = gcol[c - 1:c, :]
        decay = jnp.where(tri, jnp.exp(jnp.where(tri, gcol - grow, 0.0)), 0.0)
        kb = kn.astype(BF16)
        kk = lax.dot_general(kb, kb, nt, preferred_element_type=F32)
        nmat = jnp.where(stri, -(beta * kk * decay), 0.0)
        inv = eye + nmat
        npow = nmat
        for _ in range(int(math.log2(c)) - 1):
            npow = jnp.dot(npow, npow, precision=HIGHEST, preferred_element_type=F32)
            inv = inv + jnp.dot(inv, npow, precision=HIGHEST, preferred_element_type=F32)
        egc = jnp.exp(gcol)
        rhs = jnp.concatenate([vh * beta, kn * (beta * egc)], -1)
        sol = jnp.dot(inv, rhs, precision=HIGHEST, preferred_element_type=F32)
        u = sol[:, :hd]
        wm = sol[:, hd:]
        qk = jnp.where(tri, lax.dot_general(qn.astype(BF16), kb, nt, preferred_element_type=F32) * decay, 0.0)
        q_dec = qn * egc
        k_dec = kn * jnp.exp(glast - gcol)
        s_h = sbuf[h]
        s_b = s_h.astype(BF16)
        v_new = u - jnp.dot(wm.astype(BF16), s_b, preferred_element_type=F32)
        vb = v_new.astype(BF16)
        o_h = (jnp.dot(q_dec.astype(BF16), s_b, preferred_element_type=F32)
               + jnp.dot(qk.astype(BF16), vb, preferred_element_type=F32))
        sbuf[h] = s_h * jnp.exp(glast) + jnp.dot(k_dec.T.astype(BF16), vb, preferred_element_type=F32)
        on = o_h * lax.rsqrt(jnp.mean(o_h * o_h, -1, keepdims=True) + RMS_EPS) * nw_ref[...]
        zh = rows(z_ref)[:, hs]
        res = (on[0:lr, :] * (zh * _sigmoid(zh))).astype(o_ref.dtype)
        if len(o_ref.shape) == 3:
            o_ref[0, :, hs] = res
        else:
            o_ref[:, hs] = res

    @pl.when(n == pl.num_programs(1) - 1)
    def _():
        sout_ref[0] = sbuf[...]


def _deltanet(h, cstate, s0, wc, alog_row, dtb_row, nw, *, nbatch, nchunk, c, lr, three_d):
    qc = COL_QKVC // V_C
    zc = COL_Z // V_C
    smc = COL_SMALL // LANE
    if three_d:
        def spec(w, col):
            return pl.BlockSpec((1, lr, w), lambda b, n: (b, 0, col))
        o_spec = pl.BlockSpec((1, lr, V_C), lambda b, n: (b, 0, 0))
        o_shape = jax.ShapeDtypeStruct((nbatch, lr, V_C), F32)
    else:
        def spec(w, col):
            return pl.BlockSpec((lr, w), lambda b, n: (b * nchunk + n, col))
        o_spec = pl.BlockSpec((lr, V_C), lambda b, n: (b * nchunk + n, 0))
        o_shape = jax.ShapeDtypeStruct((nbatch * nchunk * lr, V_C), BF16)
    return pl.pallas_call(
        functools.partial(_dn_kernel, c=c, lr=lr),
        grid=(nbatch, nchunk),
        in_specs=[spec(V_C, qc), spec(V_C, qc + 1), spec(V_C, qc + 2), spec(V_C, zc), spec(LANE, smc),
                  pl.BlockSpec((1, HALO_C, QKV_C), lambda b, n: (b, 0, 0)),
                  pl.BlockSpec((1, N_HEADS_C, HEAD_DIM_C, HEAD_DIM_C), lambda b, n: (b, 0, 0, 0)),
                  pl.BlockSpec((SUBLANE, QKV_C), lambda b, n: (0, 0)),
                  pl.BlockSpec((1, LANE), lambda b, n: (0, 0)),
                  pl.BlockSpec((1, LANE), lambda b, n: (0, 0)),
                  pl.BlockSpec((1, HEAD_DIM_C), lambda b, n: (0, 0))],
        out_specs=[o_spec,
                   pl.BlockSpec((1, N_HEADS_C, HEAD_DIM_C, HEAD_DIM_C), lambda b, n: (b, 0, 0, 0))],
        out_shape=[o_shape,
                   jax.ShapeDtypeStruct((nbatch, N_HEADS_C, HEAD_DIM_C, HEAD_DIM_C), F32)],
        scratch_shapes=[pltpu.VMEM((HALO_C + c, QKV_C), F32),
                        pltpu.VMEM((N_HEADS_C, HEAD_DIM_C, HEAD_DIM_C), F32),
                        pltpu.VMEM((c, LANE), F32)],
        compiler_params=_cparams(("arbitrary", "arbitrary")),
        name="deltanet_3d" if three_d else "deltanet_2d",
    )(h, h, h, h, h, cstate, s0, wc, alog_row, dtb_row, nw)


DN_UNIT = DN_CHUNK
DN_STACK = N_HEADS_C * DN_UNIT
DN_GL_ROWS = 16


def _stack_heads(x, first=0):
    return jnp.concatenate([x[:, first + h * HEAD_DIM_C:first + (h + 1) * HEAD_DIM_C] for h in range(N_HEADS_C)], 0)


def _stack_cols(x, first):
    return jnp.concatenate([x[:, first + h:first + h + 1] for h in range(N_HEADS_C)], 0)


def _dn_prep_kernel(q_ref, k_ref, v_ref, qp_ref, kp_ref, vp_ref, sm_ref, wc_ref, alog_ref, dtb_ref,
                    u_ref, w_ref, qd_ref, kdt_ref, qk_ref, gl_ref, cbuf, *, c, lr, ups):
    rows = ups * DN_UNIT
    halo = jnp.concatenate([qp_ref[...], kp_ref[...], vp_ref[...]], -1)
    cbuf[0:HALO_C, :] = jnp.where(pl.program_id(1) == 0, 0.0, halo)
    cbuf[HALO_C:HALO_C + rows, 0:V_C] = q_ref[...]
    cbuf[HALO_C:HALO_C + rows, V_C:2 * V_C] = k_ref[...]
    cbuf[HALO_C:HALO_C + rows, 2 * V_C:3 * V_C] = v_ref[...]
    off = HALO_C - (CONV_C - 1)
    y = cbuf[off:off + rows, :] * wc_ref[0:1, :]
    for w in range(1, CONV_C):
        y = y + cbuf[off + w:off + w + rows, :] * wc_ref[w:w + 1, :]
    y = y * _sigmoid(y)
    outs = (u_ref, w_ref, qd_ref, kdt_ref, qk_ref, gl_ref)
    for slot in range(ups):
        sl = slice(slot * DN_UNIT, (slot + 1) * DN_UNIT)
        _dn_prep_unit(y[sl], sm_ref[sl, :], alog_ref[...], dtb_ref[...], outs, slot, c=c, lr=lr)


def _dn_prep_unit(y, small, alog, dtb, outs, slot, *, c, lr):
    u_ref, w_ref, qd_ref, kdt_ref, qk_ref, gl_ref = outs
    nt = (((1,), (1,)), ((), ()))
    qs = _stack_heads(y, 0)
    ks = _stack_heads(y, V_C)
    vs = _stack_heads(y, 2 * V_C)
    qn = qs * lax.rsqrt(jnp.sum(qs * qs, -1, keepdims=True) + RMS_EPS) * (HEAD_DIM_C ** -0.5)
    kn = ks * lax.rsqrt(jnp.sum(ks * ks, -1, keepdims=True) + RMS_EPS)

    beta_f = _sigmoid(small)
    g_f = -jnp.exp(alog) * _softplus(small + dtb)
    if lr < c:
        live = lax.broadcasted_iota(jnp.int32, (DN_UNIT, LANE), 0) % c >= c - lr
        beta_f = jnp.where(live, beta_f, 0.0)
        g_f = jnp.where(live, g_f, 0.0)

    r2 = lax.broadcasted_iota(jnp.int32, (2 * DN_UNIT, DN_UNIT), 0)
    c2 = lax.broadcasted_iota(jnp.int32, (2 * DN_UNIT, DN_UNIT), 1)
    rr = jnp.where(r2 >= DN_UNIT, r2 - DN_UNIT, r2)
    same2 = (rr // c) == (c2 // c)
    summer = jnp.where(same2 & ((r2 >= DN_UNIT) | (rr >= c2)), 1.0, 0.0).astype(F32)
    cg = jnp.dot(summer, g_f, precision=HIGHEST, preferred_element_type=F32)
    beta_s = _stack_cols(beta_f, 0)
    gcs = _stack_cols(cg[0:DN_UNIT], N_HEADS_C)
    gls = _stack_cols(cg[DN_UNIT:2 * DN_UNIT], N_HEADS_C)
    grow = jnp.broadcast_to(gcs, (DN_STACK, LANE)).T[0:1, :]

    ri = lax.broadcasted_iota(jnp.int32, (DN_STACK, DN_STACK), 0)
    ci = lax.broadcasted_iota(jnp.int32, (DN_STACK, DN_STACK), 1)
    same = (ri // c) == (ci // c)
    tri = same & (ri >= ci)
    stri = same & (ri > ci)
    decay = jnp.where(tri, jnp.exp(jnp.where(tri, gcs - grow, 0.0)), 0.0)
    kb = kn.astype(BF16)
    kk = lax.dot_general(kb, kb, nt, preferred_element_type=F32)
    nmat = jnp.where(stri, -(beta_s * kk * decay), 0.0)
    inv = jnp.where(ri == ci, 1.0, 0.0).astype(F32) + nmat
    npow = nmat
    for _ in range(int(math.log2(c)) - 1):
        nb = npow.astype(BF16)
        npow = jnp.dot(nb, nb, preferred_element_type=F32)
        inv = inv + jnp.dot(inv.astype(BF16), npow.astype(BF16), preferred_element_type=F32)
    egc = jnp.exp(gcs)
    rhs = jnp.concatenate([vs * beta_s, kn * (beta_s * egc)], -1)
    sol = jnp.dot(inv.astype(BF16), rhs.astype(BF16), preferred_element_type=F32)
    u_ref[slot] = sol[:, :HEAD_DIM_C]
    w_ref[slot] = sol[:, HEAD_DIM_C:].astype(BF16)
    qk = jnp.where(tri, lax.dot_general(qn.astype(BF16), kb, nt, preferred_element_type=F32) * decay, 0.0)
    qk_ref[slot] = qk.astype(BF16)
    qd_ref[slot] = (qn * egc).astype(BF16)
    kdt_ref[slot] = (kn * jnp.exp(gls - gcs)).T.astype(BF16)
    glb = jnp.broadcast_to(jnp.exp(gls), (DN_STACK, LANE))
    groups = DN_STACK // c
    gl_rows = [glb[g * c:g * c + 1, :] for g in range(groups)]
    if groups < DN_GL_ROWS:
        gl_rows.append(jnp.ones((DN_GL_ROWS - groups, LANE), F32))
    gl_ref[slot] = jnp.concatenate(gl_rows, 0)


def _dn_prep(src, small_src, wc, alog_row, dtb_row, *, qcol, smcol, nbatch, nchunk, c, lr):
    nu = nbatch * nchunk
    total = nu
    ups = 2 if (nchunk % 2 == 0 or (nchunk == 1 and nbatch % 2 == 0)) else 1
    if nchunk == 1:
        nbatch //= ups
    else:
        nchunk //= ups
    nu = total
    rows = ups * DN_UNIT
    per8 = rows // SUBLANE

    def cur(col):
        return pl.BlockSpec((rows, V_C), lambda b, n: (b * nchunk + n, col))

    def prev(col):
        return pl.BlockSpec((HALO_C, V_C), lambda b, n: (jnp.maximum((b * nchunk + n) * per8 - 1, 0), col))

    unit = lambda r, cdim: pl.BlockSpec((ups, r, cdim), lambda b, n: (b * nchunk + n, 0, 0))
    return pl.pallas_call(
        functools.partial(_dn_prep_kernel, c=c, lr=lr, ups=ups),
        grid=(nbatch, nchunk),
        in_specs=[cur(qcol), cur(qcol + 1), cur(qcol + 2), prev(qcol), prev(qcol + 1), prev(qcol + 2),
                  pl.BlockSpec((rows, LANE), lambda b, n: (b * nchunk + n, smcol)),
                  pl.BlockSpec((SUBLANE, QKV_C), lambda b, n: (0, 0)),
                  pl.BlockSpec((1, LANE), lambda b, n: (0, 0)),
                  pl.BlockSpec((1, LANE), lambda b, n: (0, 0))],
        out_specs=[unit(DN_STACK, HEAD_DIM_C), unit(DN_STACK, HEAD_DIM_C), unit(DN_STACK, HEAD_DIM_C),
                   unit(HEAD_DIM_C, DN_STACK), unit(DN_STACK, DN_STACK), unit(DN_GL_ROWS, LANE)],
        out_shape=[jax.ShapeDtypeStruct((nu, DN_STACK, HEAD_DIM_C), F32),
                   jax.ShapeDtypeStruct((nu, DN_STACK, HEAD_DIM_C), BF16),
                   jax.ShapeDtypeStruct((nu, DN_STACK, HEAD_DIM_C), BF16),
                   jax.ShapeDtypeStruct((nu, HEAD_DIM_C, DN_STACK), BF16),
                   jax.ShapeDtypeStruct((nu, DN_STACK, DN_STACK), BF16),
                   jax.ShapeDtypeStruct((nu, DN_GL_ROWS, LANE), F32)],
        scratch_shapes=[pltpu.VMEM((HALO_C + rows, QKV_C), F32)],
        compiler_params=_cparams(("arbitrary", "arbitrary")),
        name="deltanet_prep",
    )(src, src, src, src, src, src, small_src, wc, alog_row, dtb_row)


def _dn_seq_kernel(u_ref, w_ref, qd_ref, kdt_ref, qk_ref, gl_ref, *rest, c, nsq, streams, carry):
    z_refs = rest[:streams]
    nw_ref, s0_ref, o_ref, sout_ref, sbuf = rest[streams:]
    groups = DN_STACK // c
    hd = HEAD_DIM_C
    span = groups * hd
    step = pl.program_id(0)

    def load_state():
        for j in range(streams):
            for g in range(groups):
                sbuf[j * span + g * hd:j * span + (g + 1) * hd, :] = s0_ref[j, 0, g % nsq, g // nsq]

    def store_state():
        for j in range(streams):
            for g in range(groups):
                sout_ref[j, 0, g % nsq, g // nsq] = sbuf[j * span + g * hd:j * span + (g + 1) * hd, :]

    if carry:
        pl.when(step == 0)(load_state)
    else:
        load_state()

    col_group = lax.broadcasted_iota(jnp.int32, (hd, DN_STACK), 1) // c
    for j in range(streams):
        u = u_ref[j, 0]
        w = w_ref[j, 0]
        qd = qd_ref[j, 0]
        v_parts, o_parts = [], []
        for g in range(groups):
            rows = slice(g * c, (g + 1) * c)
            s_b = sbuf[j * span + g * hd:j * span + (g + 1) * hd, :].astype(BF16)
            r = jnp.dot(jnp.concatenate([w[rows], qd[rows]], 0), s_b, preferred_element_type=F32)
            v_parts.append(u[rows] - r[:c])
            o_parts.append(r[c:])
        vb = jnp.concatenate(v_parts, 0).astype(BF16)
        o = jnp.concatenate(o_parts, 0) + jnp.dot(qk_ref[j, 0], vb, preferred_element_type=F32)

        kdt = kdt_ref[j, 0]
        k_big = jnp.concatenate([jnp.where(col_group == g, kdt, jnp.zeros_like(kdt)) for g in range(groups)], 0)
        gl = gl_ref[j, 0]
        scale = jnp.concatenate([jnp.broadcast_to(gl[g:g + 1, :], (hd, LANE)) for g in range(groups)], 0)
        sbuf[j * span:(j + 1) * span, :] = (sbuf[j * span:(j + 1) * span, :] * scale
                                             + jnp.dot(k_big, vb, preferred_element_type=F32))

        outs = []
        for h in range(N_HEADS_C):
            oh = o[h * DN_UNIT:(h + 1) * DN_UNIT]
            on = oh * lax.rsqrt(jnp.mean(oh * oh, -1, keepdims=True) + RMS_EPS) * nw_ref[...]
            zh = z_refs[j][:, h * hd:(h + 1) * hd]
            outs.append(on * (zh * _sigmoid(zh)))
        o_ref[j] = jnp.concatenate(outs, -1).astype(o_ref.dtype)

    if carry:
        pl.when(step == pl.num_programs(0) - 1)(store_state)
    else:
        store_state()


def _dn_seq(prep, z_src, nw, s0, *, zcol, streams, steps, c, nsq, carry):
    groups = DN_STACK // c
    prep = [t.reshape((streams, steps) + t.shape[1:]) for t in prep]
    unit = lambda r, cdim: pl.BlockSpec((streams, 1, r, cdim), lambda i: (0, i, 0, 0))
    sdim = (streams, 1, nsq, N_HEADS_C, HEAD_DIM_C, HEAD_DIM_C)
    state = pl.BlockSpec(sdim, (lambda i: (0, 0, 0, 0, 0, 0)) if carry else (lambda i: (0, i, 0, 0, 0, 0)))
    z_specs = [pl.BlockSpec((DN_UNIT, V_C), functools.partial(lambda i, j: (j * steps + i, zcol), j=j))
               for j in range(streams)]
    o, s_out = pl.pallas_call(
        functools.partial(_dn_seq_kernel, c=c, nsq=nsq, streams=streams, carry=carry),
        grid=(steps,),
        in_specs=[unit(DN_STACK, HEAD_DIM_C), unit(DN_STACK, HEAD_DIM_C), unit(DN_STACK, HEAD_DIM_C),
                  unit(HEAD_DIM_C, DN_STACK), unit(DN_STACK, DN_STACK), unit(DN_GL_ROWS, LANE)]
                 + z_specs + [pl.BlockSpec((1, HEAD_DIM_C), lambda i: (0, 0)), state],
        out_specs=[pl.BlockSpec((streams, DN_UNIT, V_C), lambda i: (0, i, 0)), state],
        out_shape=[jax.ShapeDtypeStruct((streams, steps * DN_UNIT, V_C), BF16),
                   jax.ShapeDtypeStruct(s0.shape, F32)],
        scratch_shapes=[pltpu.VMEM((streams * groups * HEAD_DIM_C, HEAD_DIM_C), F32)],
        compiler_params=_cparams(("arbitrary",)),
        name="deltanet_seq",
    )(*prep, *([z_src] * streams), nw, s0)
    return o.reshape(streams * steps * DN_UNIT, V_C), s_out


def _merge_kernel(oa_ref, cb_ref, oc_ref, ga_ref, gb_ref, gc_ref, x_ref, woa_ref, wob_ref, woc_ref, wout_ref,
                  g1_ref, b1_ref, wr_ref, br_ref, x1_ref, route_ref, gate_ref, cnt_ref, carry, *, alpha):
    @pl.when(pl.program_id(0) == 0)
    def _():
        carry[...] = jnp.zeros(carry.shape, F32)

    br_a = jnp.dot(oa_ref[...], woa_ref[...], preferred_element_type=F32)
    br_b = jnp.dot(cb_ref[...], wob_ref[...], preferred_element_type=F32)
    br_c = jnp.dot(oc_ref[...], woc_ref[...], preferred_element_type=F32)
    mixin = _sigmoid(ga_ref[...]) * br_a + _sigmoid(gb_ref[...]) * br_b + _sigmoid(gc_ref[...]) * br_c
    mix = jnp.dot(mixin.astype(BF16), wout_ref[...], preferred_element_type=F32)
    x1 = _layer_norm(alpha * x_ref[...] + mix, g1_ref[...], b1_ref[...])
    x1_ref[...] = x1
    logits = jnp.dot(x1.astype(BF16), wr_ref[...], preferred_element_type=F32) + br_ref[...]
    lane = lax.broadcasted_iota(jnp.int32, logits.shape, 1)
    logits = jnp.where(lane < N_EXPERTS, logits, -jnp.inf)
    route = jnp.zeros(logits.shape, jnp.int32)
    val_out = jnp.zeros(logits.shape, F32)
    member = jnp.zeros(logits.shape, F32)
    sels = []
    top0 = None
    den = None
    for k in range(TOP_K):
        m = jnp.max(logits, -1, keepdims=True)
        sel = jnp.min(jnp.where(logits == m, lane, LANE), -1, keepdims=True)
        if k == 0:
            top0 = m
            e = jnp.ones_like(m)
            den = e
        else:
            e = jnp.exp(m - top0)
            den = den + e
        hit = lane == sel
        route = jnp.where(lane == k, sel, route)
        val_out = jnp.where(lane == k, e, val_out)
        member = jnp.where(hit, 1.0, member)
        logits = jnp.where(hit, -jnp.inf, logits)
        sels.append(sel)
    gate_ref[...] = val_out / den
    tm = logits.shape[0]
    earlier = lax.broadcasted_iota(jnp.int32, (tm, tm), 0) > lax.broadcasted_iota(jnp.int32, (tm, tm), 1)
    cum = jnp.dot(jnp.where(earlier, 1.0, 0.0).astype(BF16), member.astype(BF16),
                  preferred_element_type=F32) + carry[...]
    for k in range(TOP_K):
        rank = jnp.sum(jnp.where(lane == sels[k], cum, 0.0), -1, keepdims=True)
        route = jnp.where(lane == TOP_K + k, rank.astype(jnp.int32), route)
    route_ref[...] = route
    carry[...] = carry[...] + jnp.sum(member, axis=0, keepdims=True)
    cnt_ref[...] = carry[...]


def _merge(oa, cb, oc, h, x, lw, alpha):
    m = x.shape[0]
    tm = _tile(m, 512)
    row = lambda i: (i, 0)
    full = lambda i: (0, 0)
    gcol = COL_GATE // D_MODEL
    return pl.pallas_call(
        functools.partial(_merge_kernel, alpha=alpha),
        grid=(m // tm,),
        in_specs=[pl.BlockSpec((tm, Q_A), row), pl.BlockSpec((tm, C_B), row), pl.BlockSpec((tm, V_C), row),
                  pl.BlockSpec((tm, D_MODEL), lambda i: (i, gcol)),
                  pl.BlockSpec((tm, D_MODEL), lambda i: (i, gcol + 1)),
                  pl.BlockSpec((tm, D_MODEL), lambda i: (i, gcol + 2)),
                  pl.BlockSpec((tm, D_MODEL), row),
                  pl.BlockSpec((Q_A, D_MODEL), full), pl.BlockSpec((C_B, D_MODEL), full),
                  pl.BlockSpec((V_C, D_MODEL), full), pl.BlockSpec((D_MODEL, D_MODEL), full),
                  pl.BlockSpec((1, D_MODEL), full), pl.BlockSpec((1, D_MODEL), full),
                  pl.BlockSpec((D_MODEL, LANE), full), pl.BlockSpec((1, LANE), full)],
        out_specs=[pl.BlockSpec((tm, D_MODEL), row), pl.BlockSpec((tm, LANE), row),
                   pl.BlockSpec((tm, LANE), row), pl.BlockSpec((1, LANE), full)],
        out_shape=[jax.ShapeDtypeStruct((m, D_MODEL), F32), jax.ShapeDtypeStruct((m, LANE), jnp.int32),
                   jax.ShapeDtypeStruct((m, LANE), F32), jax.ShapeDtypeStruct((1, LANE), F32)],
        scratch_shapes=[pltpu.VMEM((1, LANE), F32)],
        compiler_params=_cparams(("arbitrary",)),
        name="merge_router",
    )(oa, cb, oc, h, h, h, x, lw["w_o_a"], lw["w_o_b"], lw["w_o_c"], lw["w_out"], lw["ln1_g"], lw["ln1_b"],
      lw["w_router"], lw["b_router"])


def _row_copy(src, src_row, dst, dst_row, sem):
    return pltpu.make_async_copy(src.at[pl.ds(src_row, 1)], dst.at[pl.ds(dst_row, 1)], sem)


def _dispatch_kernel(dest_ref, x_ref, xs_init_ref, xs_ref, sem, *, tt):
    del xs_init_ref

    def body(t, carry):
        for k in range(TOP_K):
            _row_copy(x_ref, t, xs_ref, dest_ref[t * TOP_K + k], sem).start(priority=k % 2)
        return carry

    lax.fori_loop(0, tt, body, 0, unroll=4)
    for k in range(TOP_K):
        pltpu.make_async_copy(x_ref, xs_ref.at[pl.ds(0, tt)], sem).wait()


def _dispatch(dest, x1, n_rows):
    m = x1.shape[0]
    tt = _tile(m, 512)
    xs0 = jnp.zeros((n_rows, D_MODEL), F32)
    return pl.pallas_call(
        functools.partial(_dispatch_kernel, tt=tt),
        grid=(m // tt,),
        in_specs=[pl.BlockSpec((tt * TOP_K,), lambda i: (i,), memory_space=pltpu.SMEM),
                  pl.BlockSpec((tt, D_MODEL), lambda i: (i, 0)),
                  pl.BlockSpec(memory_space=pl.ANY)],
        out_specs=pl.BlockSpec(memory_space=pl.ANY),
        out_shape=jax.ShapeDtypeStruct((n_rows, D_MODEL), F32),
        scratch_shapes=[pltpu.SemaphoreType.DMA(())],
        input_output_aliases={2: 0},
        compiler_params=pltpu.CompilerParams(dimension_semantics=("arbitrary",), vmem_limit_bytes=VMEM_LIMIT,
                                             disable_bounds_checks=True),
        name="moe_dispatch",
    )(dest, x1, xs0)


def _moe_kernel(be_ref, na_ref, x_ref, wu_ref, bu_ref, wd_ref, bd_ref, o_ref, wu_b, wd_b):
    i = pl.program_id(0)
    active = i < na_ref[0]
    new_expert = jnp.logical_or(i == 0, be_ref[i] != be_ref[jnp.maximum(i - 1, 0)])

    @pl.when(jnp.logical_and(active, new_expert))
    def _():
        wu_b[...] = wu_ref[0, 0].astype(BF16)
        wd_b[...] = wd_ref[0, 0].astype(BF16)

    @pl.when(active)
    def _():
        hu = jnp.dot(x_ref[...].astype(BF16), wu_b[...], preferred_element_type=F32) + bu_ref[0, 0]
        gt = jnp.minimum(hu[:, :D_FF], SWIGLU_LIMIT)
        up = jnp.clip(hu[:, D_FF:], -SWIGLU_LIMIT, SWIGLU_LIMIT)
        act = (up + 1.0) * gt * _sigmoid(SWIGLU_ALPHA * gt)
        o_ref[...] = jnp.dot(act.astype(BF16), wd_b[...], preferred_element_type=F32) + bd_ref[0, 0]

    @pl.when(jnp.logical_not(active))
    def _():
        o_ref[...] = jnp.zeros(o_ref.shape, F32)


def _moe(xs, blk_e, n_active, w_up, b_up, w_down, b_down, layer):
    nblk = xs.shape[0] // MOE_ROWS

    def blk(i, be, na):
        return jnp.minimum(i, jnp.maximum(na[0] - 1, 0))

    grid_spec = pltpu.PrefetchScalarGridSpec(
        num_scalar_prefetch=2,
        grid=(nblk,),
        in_specs=[pl.BlockSpec((MOE_ROWS, D_MODEL), lambda i, be, na: (blk(i, be, na), 0)),
                  pl.BlockSpec((1, 1, D_MODEL, 2 * D_FF), lambda i, be, na: (layer, be[blk(i, be, na)], 0, 0)),
                  pl.BlockSpec((1, 1, 1, 2 * D_FF), lambda i, be, na: (layer, be[blk(i, be, na)], 0, 0)),
                  pl.BlockSpec((1, 1, D_FF, D_MODEL), lambda i, be, na: (layer, be[blk(i, be, na)], 0, 0)),
                  pl.BlockSpec((1, 1, 1, D_MODEL), lambda i, be, na: (layer, be[blk(i, be, na)], 0, 0))],
        out_specs=pl.BlockSpec((MOE_ROWS, D_MODEL), lambda i, be, na: (i, 0)),
        scratch_shapes=[pltpu.VMEM((D_MODEL, 2 * D_FF), BF16), pltpu.VMEM((D_FF, D_MODEL), BF16)],
    )
    return pl.pallas_call(
        _moe_kernel,
        grid_spec=grid_spec,
        out_shape=jax.ShapeDtypeStruct((nblk * MOE_ROWS, D_MODEL), F32),
        compiler_params=_cparams(("arbitrary",)),
        name="moe_experts",
    )(blk_e, n_active, xs, w_up, b_up, w_down, b_down)

def _post_kernel(dest_ref, yb_ref, gate_ref, x1_ref, p_ref, wg_ref, wp_ref, g2_ref, b2_ref, x_ref, xb_ref,
                 ybuf, sem, *, alpha, tm):
    def body(t, carry):
        for k in range(TOP_K):
            _row_copy(yb_ref, dest_ref[t * TOP_K + k], ybuf.at[k], t, sem).start(priority=k % 2)
        return carry

    lax.fori_loop(0, tm, body, 0, unroll=4)
    for k in range(TOP_K):
        pltpu.make_async_copy(yb_ref.at[pl.ds(0, tm)], ybuf.at[k], sem).wait()

    gate = gate_ref[...]
    ffn = gate[:, 0:1] * ybuf[0]
    for k in range(1, TOP_K):
        ffn = ffn + gate[:, k:k + 1] * ybuf[k]
    x2 = _layer_norm(alpha * x1_ref[...] + ffn, g2_ref[...], b2_ref[...])
    gl = jnp.dot(x2.astype(BF16), wg_ref[...], preferred_element_type=F32)
    pe = jnp.dot(p_ref[...].astype(BF16), wp_ref[...], preferred_element_type=F32)
    xo = x2 + _sigmoid(gl) * pe
    x_ref[...] = xo
    xb_ref[...] = xo.astype(BF16)


def _post(dest, yb, gate, x1, p, lw, alpha):
    m = x1.shape[0]
    tm = _tile(m, 512)
    row = lambda i: (i, 0)
    full = lambda i: (0, 0)
    return pl.pallas_call(
        functools.partial(_post_kernel, alpha=alpha, tm=tm),
        grid=(m // tm,),
        in_specs=[pl.BlockSpec((tm * TOP_K,), lambda i: (i,), memory_space=pltpu.SMEM),
                  pl.BlockSpec(memory_space=pl.ANY),
                  pl.BlockSpec((tm, LANE), row),
                  pl.BlockSpec((tm, D_MODEL), row), pl.BlockSpec((tm, PLE_DIM), row),
                  pl.BlockSpec((D_MODEL, D_MODEL), full), pl.BlockSpec((PLE_DIM, D_MODEL), full),
                  pl.BlockSpec((1, D_MODEL), full), pl.BlockSpec((1, D_MODEL), full)],
        out_specs=[pl.BlockSpec((tm, D_MODEL), row), pl.BlockSpec((tm, D_MODEL), row)],
        out_shape=[jax.ShapeDtypeStruct((m, D_MODEL), F32), jax.ShapeDtypeStruct((m, D_MODEL), BF16)],
        scratch_shapes=[pltpu.VMEM((TOP_K, tm, D_MODEL), F32), pltpu.SemaphoreType.DMA(())],
        compiler_params=pltpu.CompilerParams(dimension_semantics=("arbitrary",), vmem_limit_bytes=VMEM_LIMIT,
                                             disable_bounds_checks=True),
        name="combine_ple",
    )(dest, yb, gate, x1, p, lw["w_ple_gate"], lw["w_ple_proj"], lw["ln2_g"], lw["ln2_b"])


def _rel_bucket(n):
    max_exact = NUM_BUCKETS // 2
    nf = jnp.maximum(n, 1).astype(F32)
    large = max_exact + (jnp.log(nf / max_exact) / math.log(MAX_DISTANCE / max_exact)
                         * (NUM_BUCKETS - max_exact)).astype(jnp.int32)
    return jnp.where(n < max_exact, jnp.maximum(n, 0), jnp.minimum(large, NUM_BUCKETS - 1))


def _bias_table(rel_bias, dist, valid):
    b = rel_bias.astype(F32)[_rel_bucket(dist)]
    b = jnp.where(valid[..., None], b, NEG_INF)
    return jnp.moveaxis(b, -1, 0)


def _permute_in(w):
    o = np.concatenate([[0], np.cumsum(IN_SIZES)])
    seg = lambda i: w[..., o[i]:o[i + 1]]
    small = jnp.concatenate([seg(6), seg(7)], -1)
    small = jnp.pad(small, [(0, 0)] * (w.ndim - 1) + [(0, LANE - 2 * N_HEADS_C)])
    return jnp.concatenate([seg(8), seg(3), seg(4), seg(5), seg(0), seg(1), seg(2), small], -1)


def _route(route, counts, n_tok):
    n_blk = -(-n_tok * TOP_K // MOE_ROWS) + N_EXPERTS
    cnt = counts[0, :N_EXPERTS].astype(jnp.int32)
    padded = (cnt + MOE_ROWS - 1) // MOE_ROWS * MOE_ROWS
    pad_end = jnp.cumsum(padded)
    pad_start = pad_end - padded
    idx = route[:, 0:TOP_K]
    rank = route[:, TOP_K:2 * TOP_K]
    start_of = jnp.sum(jnp.where(idx[:, :, None] == jnp.arange(N_EXPERTS)[None, None, :],
                                 pad_start[None, None, :], 0), -1)
    dest = (start_of + rank).reshape(n_tok * TOP_K).astype(jnp.int32)
    blk_first = (jnp.arange(n_blk) * MOE_ROWS)[:, None]
    blk_e = jnp.minimum(jnp.sum((pad_end[None, :] <= blk_first).astype(jnp.int32), -1), N_EXPERTS - 1)
    n_active = (pad_end[-1:] // MOE_ROWS).astype(jnp.int32)
    return dest, blk_e.astype(jnp.int32), n_active, n_blk


def kernel(x_prompt, x_sample, cache_k_a, cache_v_a, state_conv_b, state_conv_c, state_s_c, p_prompt, p_sample,
           rel_bias, w_in, b_in, sink_a, w_o_a, w_dw_b, b_dw_b, ln_b_g, ln_b_b, w_o_b, w_conv_c, a_log_c,
           dt_bias_c, norm_c, w_o_c, w_out, ln1_g, ln1_b, w_router, b_router, w_up, b_up, w_down, b_down,
           ln2_g, ln2_b, w_ple_gate, w_ple_proj):
    depth = w_in.shape[0]
    nb_p, seq, _ = x_prompt.shape
    ns, ls, _ = x_sample.shape
    w_buf = cache_k_a.shape[2]
    alpha = (2 * depth) ** 0.25
    tp = nb_p * seq
    tsm = ns * ls
    n_tok = tp + tsm
    assert seq % DN_CHUNK == 0 and seq % Q_BLOCK == 0 and w_buf == WINDOW and ls <= 16

    w_in_p = _permute_in(w_in).astype(BF16)
    b_in_p = _permute_in(b_in)[:, None, :]
    w_o_a_b, w_o_b_b, w_o_c_b, w_out_b = (t.astype(BF16) for t in (w_o_a, w_o_b, w_o_c, w_out))
    w_ple_gate_b, w_ple_proj_b = w_ple_gate.astype(BF16), w_ple_proj.astype(BF16)
    b_up4, b_down4 = b_up[:, :, None, :], b_down[:, :, None, :]
    w_router_p = jnp.pad(w_router, ((0, 0), (0, 0), (0, LANE - N_EXPERTS))).astype(BF16)
    b_router_p = jnp.pad(b_router, ((0, 0), (0, LANE - N_EXPERTS)))[:, None, :]
    w_dw_p = jnp.pad(w_dw_b, ((0, 0), (0, HALO_B - CONV_B), (0, 0)))
    w_cc_p = jnp.pad(w_conv_c, ((0, 0), (0, SUBLANE - CONV_C), (0, 0)))
    lane_pad = lambda t: jnp.pad(t, ((0, 0), (N_HEADS_C, LANE - 2 * N_HEADS_C)))[:, None, :]
    alog_rows, dtb_rows = lane_pad(a_log_c), lane_pad(dt_bias_c)

    qi = jnp.arange(Q_BLOCK)[:, None]
    kj = jnp.arange(2 * Q_BLOCK)[None, :]
    dist = qi - kj + Q_BLOCK
    valid = (dist >= 0) & (dist < WINDOW)
    tab_rest = _bias_table(rel_bias, dist, valid)
    tab_first = _bias_table(rel_bias, dist, valid & (kj >= Q_BLOCK))
    bias_prompt = jnp.stack([tab_first, tab_rest])

    kwin = 2 * WINDOW
    ti = jnp.arange(ls)[:, None]
    kjs = jnp.arange(kwin)[None, :]
    dist_s = w_buf + ti - kjs
    valid_s = (dist_s >= 0) & (dist_s < WINDOW) & (kjs < w_buf + ls)
    tab_s = _bias_table(rel_bias, dist_s, valid_s)
    tab_s = tab_s.reshape(N_KV_A, GROUP_A * ls, kwin)
    gb = 2 * _tile(ns, 16, 1)
    bias_s = jnp.tile(tab_s, (gb // N_KV_A, 1, 1))

    wrows = -(-(CONV_B - 1 + ls) // SUBLANE) * SUBLANE
    jj = jnp.arange(wrows)[None, :] - jnp.arange(ls)[:, None]
    tap_ok = (jj >= 0) & (jj < CONV_B)

    x_all = jnp.concatenate([x_prompt.reshape(tp, D_MODEL), x_sample.reshape(tsm, D_MODEL)], 0)
    xb_all = x_all.astype(BF16)
    p_all = jnp.concatenate([p_prompt.reshape(depth, tp, PLE_DIM), p_sample.reshape(depth, tsm, PLE_DIM)], 1)
    zero_s = jnp.zeros((nb_p, 1, 1, N_HEADS_C, HEAD_DIM_C, HEAD_DIM_C), F32)
    c_s = 16
    nsq_s = DN_UNIT // c_s
    assert ls + CONV_C - 1 <= c_s and ns % nsq_s == 0

    st_p, st_s = [], []
    for i in range(depth):
        lw = {"w_o_a": w_o_a_b[i], "w_o_b": w_o_b_b[i], "w_o_c": w_o_c_b[i], "w_out": w_out_b[i],
              "ln1_g": ln1_g[i][None], "ln1_b": ln1_b[i][None], "w_router": w_router_p[i],
              "b_router": b_router_p[i], "w_ple_gate": w_ple_gate_b[i], "w_ple_proj": w_ple_proj_b[i],
              "ln2_g": ln2_g[i][None], "ln2_b": ln2_b[i][None]}
        h = _in_proj(xb_all, w_in_p[i], b_in_p[i])
        hs = h[tp:].reshape(ns, ls, N_IN_PAD)

        sinks = sink_a[i].astype(F32)
        oa_p = _attn_prompt(h, sinks, bias_prompt, nb_p, seq)
        q_s = hs[:, :, COL_QA:COL_QA + Q_A].reshape(ns, ls, N_KV_A, GROUP_A, HEAD_DIM_A)
        q_s = q_s.transpose(0, 2, 3, 1, 4).reshape(ns * N_KV_A, GROUP_A * ls, HEAD_DIM_A)
        k_new = hs[:, :, COL_KA:COL_KA + KV_A].reshape(ns, ls, N_KV_A, HEAD_DIM_A)
        v_new = hs[:, :, COL_VA:COL_VA + KV_A].reshape(ns, ls, N_KV_A, HEAD_DIM_A)
        k_all = jnp.concatenate([cache_k_a[i], k_new], 1)
        v_all = jnp.concatenate([cache_v_a[i], v_new], 1)

        def keywin(t):
            t = jnp.pad(t, ((0, 0), (0, kwin - w_buf - ls), (0, 0), (0, 0)))
            return t.transpose(0, 2, 1, 3).reshape(ns * N_KV_A, kwin, HEAD_DIM_A)

        sink_s = jnp.tile(jnp.repeat(sinks.reshape(N_KV_A, GROUP_A), ls, axis=1)[:, :, None], (gb // N_KV_A, 1, 1))
        oa_s = _attn_sample(q_s, keywin(k_all), keywin(v_all), bias_s, sink_s)
        oa_s = oa_s.reshape(ns, N_KV_A, GROUP_A, ls, HEAD_DIM_A).transpose(0, 3, 1, 2, 4).reshape(tsm, Q_A)
        oa = jnp.concatenate([oa_p, oa_s.astype(BF16)], 0)

        cb_p, tail_p = _convb_prompt(h, w_dw_p[i], b_dw_b[i][None], ln_b_g[i][None], ln_b_b[i][None], nb_p, seq)
        wt = jnp.where(tap_ok[:, :, None], w_dw_b[i][jnp.clip(jj, 0, CONV_B - 1)], 0.0)
        cb_s, nconv_b_s = _convb_sample(hs, state_conv_b[i], wt, b_dw_b[i][None], ln_b_g[i][None], ln_b_b[i][None])
        cb = jnp.concatenate([cb_p, cb_s.transpose(1, 0, 2).reshape(tsm, C_B)], 0)

        nw = norm_c[i][None]
        nchunk = seq // DN_CHUNK
        prep_p = _dn_prep(h, h, w_cc_p[i], alog_rows[i], dtb_rows[i], qcol=COL_QKVC // V_C,
                          smcol=COL_SMALL // LANE, nbatch=nb_p, nchunk=nchunk, c=DN_CHUNK, lr=DN_CHUNK)
        oc_p, s_p = _dn_seq(prep_p, h, nw, zero_s, zcol=COL_Z // V_C, streams=nb_p, steps=nchunk,
                            c=DN_CHUNK, nsq=1, carry=True)
        s_p = s_p.reshape(nb_p, N_HEADS_C, HEAD_DIM_C, HEAD_DIM_C)
        lead = c_s - ls
        hist = jnp.concatenate([jnp.zeros((ns, lead - (CONV_C - 1), QKV_C), F32), state_conv_c[i],
                                hs[:, :, COL_QKVC:COL_QKVC + QKV_C]], 1).reshape(ns * c_s, QKV_C)
        pad_s = lambda col, wid: jnp.pad(hs[:, :, col:col + wid], ((0, 0), (lead, 0), (0, 0))).reshape(ns * c_s, wid)
        prep_s = _dn_prep(hist, pad_s(COL_SMALL, LANE), w_cc_p[i], alog_rows[i], dtb_rows[i], qcol=0, smcol=0,
                          nbatch=ns // nsq_s, nchunk=1, c=c_s, lr=ls)
        units_s = ns // nsq_s
        streams_s = 2 if units_s % 2 == 0 else 1
        s0_s = state_s_c[i].astype(F32).reshape(streams_s, units_s // streams_s, nsq_s, N_HEADS_C, HEAD_DIM_C,
                                                HEAD_DIM_C)
        oc_s, s_s = _dn_seq(prep_s, pad_s(COL_Z, V_C), nw, s0_s, zcol=0, streams=streams_s,
                            steps=units_s // streams_s, c=c_s, nsq=nsq_s, carry=False)
        s_s = s_s.reshape(ns, N_HEADS_C, HEAD_DIM_C, HEAD_DIM_C)
        oc = jnp.concatenate([oc_p, oc_s.reshape(ns, c_s, V_C)[:, lead:].reshape(tsm, V_C)], 0)

        x1, route, gate, counts = _merge(oa, cb, oc, h, x_all, lw, alpha)

        dest, blk_e, n_active, n_blk = _route(route, counts, n_tok)
        xs = _dispatch(dest, x1, n_blk * MOE_ROWS)
        yb = _moe(xs, blk_e, n_active, w_up, b_up4, w_down, b_down4, i)

        x_all, xb_all = _post(dest, yb, gate, x1, p_all[i], lw, alpha)

        def tail_rows(nrows, col, width):
            return jnp.stack([h[(b + 1) * seq - nrows:(b + 1) * seq, col:col + width] for b in range(nb_p)])

        kp_new = tail_rows(w_buf, COL_KA, KV_A)
        vp_new = tail_rows(w_buf, COL_VA, KV_A)
        ccp_new = tail_rows(CONV_C - 1, COL_QKVC, QKV_C)
        st_p.append((kp_new.reshape(nb_p, w_buf, N_KV_A, HEAD_DIM_A), vp_new.reshape(nb_p, w_buf, N_KV_A, HEAD_DIM_A),
                     tail_p[:, HALO_B - (CONV_B - 1):], ccp_new, s_p))
        ccs_ext = jnp.concatenate([state_conv_c[i], hs[:, :, COL_QKVC:COL_QKVC + QKV_C]], 1)
        st_s.append((k_all[:, ls:], v_all[:, ls:], nconv_b_s, ccs_ext[:, -(CONV_C - 1):], s_s))

    yp = x_all[:tp].reshape(nb_p, seq, D_MODEL)
    ys = x_all[tp:].reshape(ns, ls, D_MODEL)
    kp, vp, cbp, ccp, scp = [jnp.stack(z) for z in zip(*st_p)]
    ks_, vs_, cbs, ccs, scs = [jnp.stack(z) for z in zip(*st_s)]
    return (yp, ys, kp, vp, cbp, ccp, scp, ks_, vs_, cbs, ccs, scs)
```

```python
import functools
import math

import numpy as np
import jax
import jax.numpy as jnp
from jax import lax
from jax.experimental import pallas as pl
from jax.experimental.pallas import tpu as pltpu

F32 = jnp.float32
BF16 = jnp.bfloat16
HIGHEST = lax.Precision.HIGHEST

D_MODEL = 1024
N_HEADS_A, N_KV_A, HEAD_DIM_A = 8, 2, 64
GROUP_A = N_HEADS_A // N_KV_A
Q_A = N_HEADS_A * HEAD_DIM_A
KV_A = N_KV_A * HEAD_DIM_A
WINDOW = 128
Q_BLOCK = 128
NUM_BUCKETS = 32
MAX_DISTANCE = 128
NEG_INF = -1e30
C_B = 512
CONV_B = 31
N_HEADS_C, HEAD_DIM_C = 4, 128
V_C = N_HEADS_C * HEAD_DIM_C
QKV_C = 3 * V_C
CONV_C = 4
DN_CHUNK = 64
N_BRANCH = 3
IN_SIZES = (Q_A, KV_A, KV_A, 2 * C_B, QKV_C, V_C, N_HEADS_C, N_HEADS_C, N_BRANCH * D_MODEL)
N_EXPERTS = 32
TOP_K = 4
D_FF = 1024
SWIGLU_LIMIT = 7.0
SWIGLU_ALPHA = 1.702
PLE_DIM = 256
LN_EPS = 1e-5
RMS_EPS = 1e-6

LANE = 128
SUBLANE = 8
VMEM_LIMIT = 56 * 1024 * 1024

COL_GATE = 0
COL_GLU = 3 * D_MODEL
COL_QKVC = COL_GLU + 2 * C_B
COL_Z = COL_QKVC + QKV_C
COL_QA = COL_Z + V_C
COL_KA = COL_QA + Q_A
COL_VA = COL_KA + KV_A
COL_SMALL = COL_VA + KV_A
N_IN_PAD = COL_SMALL + LANE
MOE_ROWS = 512
HALO_B = 32
HALO_C = 8


def _tile(n, target, mult=SUBLANE):
    best = None
    for t in range(mult, min(n, target) + 1, mult):
        if n % t == 0:
            best = t
    return best if best is not None else n


def _cparams(sem):
    return pltpu.CompilerParams(dimension_semantics=sem, vmem_limit_bytes=VMEM_LIMIT)


def _layer_norm(x, g, b):
    mu = jnp.mean(x, -1, keepdims=True)
    xc = x - mu
    var = jnp.mean(xc * xc, -1, keepdims=True)
    return xc * lax.rsqrt(var + LN_EPS) * g + b


def _sigmoid(x):
    return 1.0 / (1.0 + jnp.exp(-x))


def _mm_bias_kernel(x_ref, w_ref, b_ref, o_ref):
    o_ref[...] = jnp.dot(x_ref[...], w_ref[...], preferred_element_type=F32) + b_ref[...]


def _in_proj(x_bf, w_bf, b):
    m, k = x_bf.shape
    n = w_bf.shape[1]
    tm = _tile(m, 1536)
    tn = _tile(n, 1408, LANE)
    return pl.pallas_call(
        _mm_bias_kernel,
        grid=(n // tn, m // tm),
        in_specs=[pl.BlockSpec((tm, k), lambda j, i: (i, 0)),
                  pl.BlockSpec((k, tn), lambda j, i: (0, j)),
                  pl.BlockSpec((1, tn), lambda j, i: (0, j))],
        out_specs=pl.BlockSpec((tm, tn), lambda j, i: (i, j)),
        out_shape=jax.ShapeDtypeStruct((m, n), F32),
        compiler_params=_cparams(("arbitrary", "arbitrary")),
        name="in_proj",
    )(x_bf, w_bf, b)


def _attn_prompt_kernel(sink_ref, q_ref, kp_ref, kc_ref, vp_ref, vc_ref, bias_ref, o_ref):
    nt = (((1,), (1,)), ((), ()))
    q = (q_ref[...] * (HEAD_DIM_A ** -0.5)).astype(BF16)
    lo = lax.broadcasted_iota(jnp.int32, (2 * Q_BLOCK, KV_A), 1) < HEAD_DIM_A

    def placed(prev_ref, cur_ref):
        band = jnp.concatenate([prev_ref[...], cur_ref[...]], 0)
        own = (jnp.where(lo, band, 0.0), jnp.where(lo, 0.0, band))
        moved = (pltpu.roll(own[0], HEAD_DIM_A, 1), pltpu.roll(own[1], HEAD_DIM_A, 1))
        return ((own[0].astype(BF16), moved[0].astype(BF16)), (moved[1].astype(BF16), own[1].astype(BF16)))

    kz = placed(kp_ref, kc_ref)
    vz = placed(vp_ref, vc_ref)
    pairs = []
    for kv in range(N_KV_A):
        heads = range(kv * GROUP_A, (kv + 1) * GROUP_A)
        scores = [lax.dot_general(q[:, (h // 2) * LANE:(h // 2 + 1) * LANE], kz[kv][h % 2], nt,
                                  preferred_element_type=F32) + bias_ref[0, h] for h in heads]
        probs = []
        for h, s in zip(heads, scores):
            sink = sink_ref[h]
            m = jnp.maximum(jnp.max(s, -1, keepdims=True), sink)
            e = jnp.exp(s - m)
            inv = 1.0 / (jnp.sum(e, -1, keepdims=True) + jnp.exp(sink - m))
            probs.append((e * inv).astype(BF16))
        pv = [jnp.dot(p, vz[kv][h % 2], preferred_element_type=F32) for h, p in zip(heads, probs)]
        pairs += [pv[0] + pv[1], pv[2] + pv[3]]
    o_ref[...] = jnp.concatenate(pairs, -1).astype(o_ref.dtype)


def _attn_prompt(h, sinks, bias_tab, nbatch, seq):
    nb = seq // Q_BLOCK
    qcol = COL_QA // Q_A
    kcol = COL_KA // KV_A
    vcol = COL_VA // KV_A

    def cur(c):
        return lambda b, i: (b * nb + i, c)

    def prev(c):
        return lambda b, i: (jnp.maximum(b * nb + i - 1, 0), c)

    return pl.pallas_call(
        _attn_prompt_kernel,
        grid=(nbatch, nb),
        in_specs=[pl.BlockSpec(memory_space=pltpu.SMEM),
                  pl.BlockSpec((Q_BLOCK, Q_A), cur(qcol)),
                  pl.BlockSpec((Q_BLOCK, KV_A), prev(kcol)),
                  pl.BlockSpec((Q_BLOCK, KV_A), cur(kcol)),
                  pl.BlockSpec((Q_BLOCK, KV_A), prev(vcol)),
                  pl.BlockSpec((Q_BLOCK, KV_A), cur(vcol)),
                  pl.BlockSpec((1, N_HEADS_A, Q_BLOCK, 2 * Q_BLOCK), lambda b, i: (jnp.minimum(i, 1), 0, 0, 0))],
        out_specs=pl.BlockSpec((Q_BLOCK, Q_A), lambda b, i: (b * nb + i, 0)),
        out_shape=jax.ShapeDtypeStruct((nbatch * seq, Q_A), BF16),
        compiler_params=_cparams(("arbitrary", "arbitrary")),
        name="attn_prompt",
    )(sinks, h, h, h, h, h, bias_tab)


def _attn_sample_kernel(q_ref, k_ref, v_ref, bias_ref, sink_ref, o_ref):
    q = (q_ref[...] * (HEAD_DIM_A ** -0.5)).astype(BF16)
    s = jnp.einsum("bqd,bkd->bqk", q, k_ref[...].astype(BF16), preferred_element_type=F32) + bias_ref[...]
    sink = sink_ref[...]
    m = jnp.maximum(jnp.max(s, -1, keepdims=True), sink)
    e = jnp.exp(s - m)
    pr = e / (jnp.sum(e, -1, keepdims=True) + jnp.exp(sink - m))
    o_ref[...] = jnp.einsum("bqk,bkd->bqd", pr.astype(BF16), v_ref[...].astype(BF16), preferred_element_type=F32)


def _attn_sample(q, k, v, bias, sink):
    g, r, hd = q.shape
    kk = k.shape[1]
    gb = bias.shape[0]
    return pl.pallas_call(
        _attn_sample_kernel,
        grid=(g // gb,),
        in_specs=[pl.BlockSpec((gb, r, hd), lambda i: (i, 0, 0)),
                  pl.BlockSpec((gb, kk, hd), lambda i: (i, 0, 0)),
                  pl.BlockSpec((gb, kk, hd), lambda i: (i, 0, 0)),
                  pl.BlockSpec((gb, r, kk), lambda i: (0, 0, 0)),
                  pl.BlockSpec((gb, r, 1), lambda i: (0, 0, 0))],
        out_specs=pl.BlockSpec((gb, r, hd), lambda i: (i, 0, 0)),
        out_shape=jax.ShapeDtypeStruct((g, r, hd), F32),
        compiler_params=_cparams(("arbitrary",)),
        name="attn_sample",
    )(q, k, v, bias, sink)


CONV_ROWS = 64


def _convb_prompt_kernel(a_ref, b_ref, w_ref, bias_ref, g_ref, beta_ref, cb_ref, tail_ref, ubuf, shbuf, *, tl):
    @pl.when(pl.program_id(1) == 0)
    def _():
        ubuf[0:HALO_B, :] = jnp.zeros((HALO_B, C_B), F32)

    ubuf[HALO_B:HALO_B + tl, :] = a_ref[...] * _sigmoid(b_ref[...])
    off = HALO_B - (CONV_B - 1)
    span = tl + HALO_B - SUBLANE
    for s in range(1, SUBLANE):
        shbuf[s - 1, 0:span, :] = ubuf[s:s + span, :]

    def body(r, carry):
        base = pl.multiple_of(r * CONV_ROWS, CONV_ROWS)
        accs = []
        for c in range(C_B // LANE):
            cs = slice(c * LANE, (c + 1) * LANE)
            acc = jnp.zeros((CONV_ROWS, LANE), F32)
            for w in range(CONV_B):
                d = off + w
                start = pl.multiple_of(base + (d // SUBLANE) * SUBLANE, SUBLANE)
                if d % SUBLANE == 0:
                    win = ubuf[pl.ds(start, CONV_ROWS), cs]
                else:
                    win = shbuf[d % SUBLANE - 1, pl.ds(start, CONV_ROWS), cs]
                acc = acc + win * w_ref[w:w + 1, cs]
            accs.append(acc + bias_ref[:, cs])
        y = _layer_norm(jnp.concatenate(accs, -1), g_ref[...], beta_ref[...])
        cb_ref[pl.ds(base, CONV_ROWS), :] = (y * _sigmoid(y)).astype(cb_ref.dtype)
        return carry

    lax.fori_loop(0, tl // CONV_ROWS, body, 0)
    tail = ubuf[tl:tl + HALO_B, :]
    tail_ref[0] = tail
    ubuf[0:HALO_B, :] = tail


def _convb_prompt(h, w_dw, b_dw, ln_g, ln_b, nbatch, seq):
    tl = _tile(seq, 512, CONV_ROWS)
    nt = seq // tl
    acol = COL_GLU // C_B
    return pl.pallas_call(
        functools.partial(_convb_prompt_kernel, tl=tl),
        grid=(nbatch, nt),
        in_specs=[pl.BlockSpec((tl, C_B), lambda b, t: (b * nt + t, acol)),
                  pl.BlockSpec((tl, C_B), lambda b, t: (b * nt + t, acol + 1)),
                  pl.BlockSpec((HALO_B, C_B), lambda b, t: (0, 0)),
                  pl.BlockSpec((1, C_B), lambda b, t: (0, 0)),
                  pl.BlockSpec((1, C_B), lambda b, t: (0, 0)),
                  pl.BlockSpec((1, C_B), lambda b, t: (0, 0))],
        out_specs=[pl.BlockSpec((tl, C_B), lambda b, t: (b * nt + t, 0)),
                   pl.BlockSpec((1, HALO_B, C_B), lambda b, t: (b, 0, 0))],
        out_shape=[jax.ShapeDtypeStruct((nbatch * seq, C_B), BF16),
                   jax.ShapeDtypeStruct((nbatch, HALO_B, C_B), F32)],
        scratch_shapes=[pltpu.VMEM((HALO_B + tl, C_B), F32),
                        pltpu.VMEM((SUBLANE - 1, HALO_B + tl, C_B), F32)],
        compiler_params=_cparams(("arbitrary", "arbitrary")),
        name="convb_prompt",
    )(h, h, w_dw, b_dw, ln_g, ln_b)


def _convb_sample_kernel(a_ref, b_ref, st_ref, wt_ref, bias_ref, g_ref, beta_ref, cb_ref, nst_ref, ext, *, ls, wrows):
    ns = CONV_B - 1
    ext[:, ns - ns % SUBLANE:wrows, :] = jnp.zeros((ext.shape[0], wrows - (ns - ns % SUBLANE), C_B), F32)
    ext[:, 0:ns, :] = st_ref[...]
    ext[:, ns:ns + ls, :] = a_ref[...] * _sigmoid(b_ref[...])
    nst_ref[...] = ext[:, ls:ls + ns, :]
    win = ext[...]
    for t in range(ls):
        y = jnp.sum(win * wt_ref[t][None], axis=1) + bias_ref[...]
        y = _layer_norm(y, g_ref[...], beta_ref[...])
        cb_ref[t] = (y * _sigmoid(y)).astype(cb_ref.dtype)


def _convb_sample(hs, state, wt, b_dw, ln_g, ln_b):
    s, ls, _ = hs.shape
    wrows = wt.shape[1]
    sb = _tile(s, 16, 1)
    acol = COL_GLU // C_B
    ns = CONV_B - 1
    return pl.pallas_call(
        functools.partial(_convb_sample_kernel, ls=ls, wrows=wrows),
        grid=(s // sb,),
        in_specs=[pl.BlockSpec((sb, ls, C_B), lambda i: (i, 0, acol)),
                  pl.BlockSpec((sb, ls, C_B), lambda i: (i, 0, acol + 1)),
                  pl.BlockSpec((sb, ns, C_B), lambda i: (i, 0, 0)),
                  pl.BlockSpec((ls, wrows, C_B), lambda i: (0, 0, 0)),
                  pl.BlockSpec((1, C_B), lambda i: (0, 0)),
                  pl.BlockSpec((1, C_B), lambda i: (0, 0)),
                  pl.BlockSpec((1, C_B), lambda i: (0, 0))],
        out_specs=[pl.BlockSpec((ls, sb, C_B), lambda i: (0, i, 0)),
                   pl.BlockSpec((sb, ns, C_B), lambda i: (i, 0, 0))],
        out_shape=[jax.ShapeDtypeStruct((ls, s, C_B), BF16),
                   jax.ShapeDtypeStruct((s, ns, C_B), F32)],
        scratch_shapes=[pltpu.VMEM((sb, wrows, C_B), F32)],
        compiler_params=_cparams(("arbitrary",)),
        name="convb_sample",
    )(hs, hs, state, wt, b_dw, ln_g, ln_b)


def _softplus(x):
    return jnp.maximum(x, 0.0) + jnp.log(1.0 + jnp.exp(-jnp.abs(x)))


def _dn_kernel(q_ref, k_ref, v_ref, z_ref, sm_ref, cst_ref, s0_ref, wc_ref, alog_ref, dtb_ref, nw_ref,
               o_ref, sout_ref, cbuf, sbuf, smbuf, *, c, lr):
    n = pl.program_id(1)
    hd = HEAD_DIM_C

    @pl.when(n == 0)
    def _():
        cbuf[0:HALO_C, :] = cst_ref[0]
        sbuf[...] = s0_ref[0]

    def rows(ref):
        x = ref[...]
        return x.reshape(x.shape[-2:])

    cbuf[HALO_C:HALO_C + lr, 0:V_C] = rows(q_ref)
    cbuf[HALO_C:HALO_C + lr, V_C:2 * V_C] = rows(k_ref)
    cbuf[HALO_C:HALO_C + lr, 2 * V_C:3 * V_C] = rows(v_ref)
    if lr < c:
        cbuf[HALO_C + lr:HALO_C + c, :] = jnp.zeros((c - lr, QKV_C), F32)
        smbuf[...] = jnp.zeros((c, LANE), F32)
    smbuf[0:lr, :] = rows(sm_ref)

    off = HALO_C - (CONV_C - 1)
    y = cbuf[off:off + c, :] * wc_ref[0:1, :]
    for w in range(1, CONV_C):
        y = y + cbuf[off + w:off + w + c, :] * wc_ref[w:w + 1, :]
    y = y * _sigmoid(y)
    if lr == c:
        cbuf[0:HALO_C, :] = cbuf[c:c + HALO_C, :]

    small = smbuf[...]
    beta_f = _sigmoid(small)
    g_f = -jnp.exp(alog_ref[...]) * _softplus(small + dtb_ref[...])
    if lr < c:
        live = lax.broadcasted_iota(jnp.int32, (c, LANE), 0) < lr
        beta_f = jnp.where(live, beta_f, 0.0)
        g_f = jnp.where(live, g_f, 0.0)

    ri = lax.broadcasted_iota(jnp.int32, (c, c), 0)
    ci = lax.broadcasted_iota(jnp.int32, (c, c), 1)
    tri = ri >= ci
    stri = ri > ci
    eye = jnp.where(ri == ci, 1.0, 0.0).astype(F32)
    gc_all = jnp.dot(jnp.where(tri, 1.0, 0.0).astype(F32), g_f, precision=HIGHEST, preferred_element_type=F32)
    sel = jnp.where(lax.broadcasted_iota(jnp.int32, (SUBLANE, LANE), 1)
                    == lax.broadcasted_iota(jnp.int32, (SUBLANE, LANE), 0) + N_HEADS_C, 1.0, 0.0).astype(F32)
    nt = (((1,), (1,)), ((), ()))
    gc_rows = lax.dot_general(sel, gc_all, nt, precision=HIGHEST, preferred_element_type=F32)

    for h in range(N_HEADS_C):
        hs = slice(h * hd, (h + 1) * hd)
        qh = y[:, hs]
        kh = y[:, V_C + h * hd:V_C + (h + 1) * hd]
        vh = y[:, 2 * V_C + h * hd:2 * V_C + (h + 1) * hd]
        qn = qh * lax.rsqrt(jnp.sum(qh * qh, -1, keepdims=True) + RMS_EPS) * (hd ** -0.5)
        kn = kh * lax.rsqrt(jnp.sum(kh * kh, -1, keepdims=True) + RMS_EPS)
        beta = beta_f[:, h:h + 1]
        gcol = gc_all[:, N_HEADS_C + h:N_HEADS_C + h + 1]
        grow = gc_rows[h:h + 1, :]
        glast = gcol[c - 1:c, :]
        decay = jnp.where(tri, jnp.exp(jnp.where(tri, gcol - grow, 0.0)), 0.0)
        kb = kn.astype(BF16)
        kk = lax.dot_general(kb, kb, nt, preferred_element_type=F32)
        nmat = jnp.where(stri, -(beta * kk * decay), 0.0)
        inv = eye + nmat
        npow = nmat
        for _ in range(int(math.log2(c)) - 1):
            npow = jnp.dot(npow, npow, precision=HIGHEST, preferred_element_type=F32)
            inv = inv + jnp.dot(inv, npow, precision=HIGHEST, preferred_element_type=F32)
        egc = jnp.exp(gcol)
        rhs = jnp.concatenate([vh * beta, kn * (beta * egc)], -1)
        sol = jnp.dot(inv, rhs, precision=HIGHEST, preferred_element_type=F32)
        u = sol[:, :hd]
        wm = sol[:, hd:]
        qk = jnp.where(tri, lax.dot_general(qn.astype(BF16), kb, nt, preferred_element_type=F32) * decay, 0.0)
        q_dec = qn * egc
        k_dec = kn * jnp.exp(glast - gcol)
        s_h = sbuf[h]
        s_b = s_h.astype(BF16)
        v_new = u - jnp.dot(wm.astype(BF16), s_b, preferred_element_type=F32)
        vb = v_new.astype(BF16)
        o_h = (jnp.dot(q_dec.astype(BF16), s_b, preferred_element_type=F32)
               + jnp.dot(qk.astype(BF16), vb, preferred_element_type=F32))
        sbuf[h] = s_h * jnp.exp(glast) + jnp.dot(k_dec.T.astype(BF16), vb, preferred_element_type=F32)
        on = o_h * lax.rsqrt(jnp.mean(o_h * o_h, -1, keepdims=True) + RMS_EPS) * nw_ref[...]
        zh = rows(z_ref)[:, hs]
        res = (on[0:lr, :] * (zh * _sigmoid(zh))).astype(o_ref.dtype)
        if len(o_ref.shape) == 3:
            o_ref[0, :, hs] = res
        else:
            o_ref[:, hs] = res

    @pl.when(n == pl.num_programs(1) - 1)
    def _():
        sout_ref[0] = sbuf[...]


def _deltanet(h, cstate, s0, wc, alog_row, dtb_row, nw, *, nbatch, nchunk, c, lr, three_d):
    qc = COL_QKVC // V_C
    zc = COL_Z // V_C
    smc = COL_SMALL // LANE
    if three_d:
        def spec(w, col):
            return pl.BlockSpec((1, lr, w), lambda b, n: (b, 0, col))
        o_spec = pl.BlockSpec((1, lr, V_C), lambda b, n: (b, 0, 0))
        o_shape = jax.ShapeDtypeStruct((nbatch, lr, V_C), F32)
    else:
        def spec(w, col):
            return pl.BlockSpec((lr, w), lambda b, n: (b * nchunk + n, col))
        o_spec = pl.BlockSpec((lr, V_C), lambda b, n: (b * nchunk + n, 0))
        o_shape = jax.ShapeDtypeStruct((nbatch * nchunk * lr, V_C), BF16)
    return pl.pallas_call(
        functools.partial(_dn_kernel, c=c, lr=lr),
        grid=(nbatch, nchunk),
        in_specs=[spec(V_C, qc), spec(V_C, qc + 1), spec(V_C, qc + 2), spec(V_C, zc), spec(LANE, smc),
                  pl.BlockSpec((1, HALO_C, QKV_C), lambda b, n: (b, 0, 0)),
                  pl.BlockSpec((1, N_HEADS_C, HEAD_DIM_C, HEAD_DIM_C), lambda b, n: (b, 0, 0, 0)),
                  pl.BlockSpec((SUBLANE, QKV_C), lambda b, n: (0, 0)),
                  pl.BlockSpec((1, LANE), lambda b, n: (0, 0)),
                  pl.BlockSpec((1, LANE), lambda b, n: (0, 0)),
                  pl.BlockSpec((1, HEAD_DIM_C), lambda b, n: (0, 0))],
        out_specs=[o_spec,
                   pl.BlockSpec((1, N_HEADS_C, HEAD_DIM_C, HEAD_DIM_C), lambda b, n: (b, 0, 0, 0))],
        out_shape=[o_shape,
                   jax.ShapeDtypeStruct((nbatch, N_HEADS_C, HEAD_DIM_C, HEAD_DIM_C), F32)],
        scratch_shapes=[pltpu.VMEM((HALO_C + c, QKV_C), F32),
                        pltpu.VMEM((N_HEADS_C, HEAD_DIM_C, HEAD_DIM_C), F32),
                        pltpu.VMEM((c, LANE), F32)],
        compiler_params=_cparams(("arbitrary", "arbitrary")),
        name="deltanet_3d" if three_d else "deltanet_2d",
    )(h, h, h, h, h, cstate, s0, wc, alog_row, dtb_row, nw)


DN_UNIT = DN_CHUNK
DN_STACK = N_HEADS_C * DN_UNIT
DN_PAIR = 2 * DN_UNIT
DN_GL_ROWS = 16


def _dn_prep_kernel(q_ref, k_ref, v_ref, qp_ref, kp_ref, vp_ref, sm_ref, wc_ref, alog_ref, dtb_ref,
                    u_ref, w_ref, qd_ref, kdt_ref, qk_ref, gl_ref, cbuf, *, c, lr, ups):
    rows = ups * DN_UNIT
    halo = jnp.concatenate([qp_ref[...], kp_ref[...], vp_ref[...]], -1)
    cbuf[0:HALO_C, :] = jnp.where(pl.program_id(1) == 0, 0.0, halo)
    cbuf[HALO_C:HALO_C + rows, 0:V_C] = q_ref[...]
    cbuf[HALO_C:HALO_C + rows, V_C:2 * V_C] = k_ref[...]
    cbuf[HALO_C:HALO_C + rows, 2 * V_C:3 * V_C] = v_ref[...]
    off = HALO_C - (CONV_C - 1)
    y = cbuf[off:off + rows, :] * wc_ref[0:1, :]
    for w in range(1, CONV_C):
        y = y + cbuf[off + w:off + w + rows, :] * wc_ref[w:w + 1, :]
    y = y * _sigmoid(y)
    outs = (u_ref, w_ref, qd_ref, kdt_ref, qk_ref, gl_ref)
    _dn_prep_tables(y, sm_ref[...], alog_ref[...], dtb_ref[...], outs, ups=ups, c=c, lr=lr)


def _dn_prep_tables(y, small, alog, dtb, outs, *, ups, c, lr):
    u_ref, w_ref, qd_ref, kdt_ref, qk_ref, gl_ref = outs
    nt = (((1,), (1,)), ((), ()))
    hd = HEAD_DIM_C
    beta_f = _sigmoid(small)
    g_f = -jnp.exp(alog) * _softplus(small + dtb)
    if lr < c:
        live = lax.broadcasted_iota(jnp.int32, beta_f.shape, 0) % c >= c - lr
        beta_f = jnp.where(live, beta_f, 0.0)
        g_f = jnp.where(live, g_f, 0.0)

    r2 = lax.broadcasted_iota(jnp.int32, (2 * DN_UNIT, DN_UNIT), 0)
    c2 = lax.broadcasted_iota(jnp.int32, (2 * DN_UNIT, DN_UNIT), 1)
    rr = jnp.where(r2 >= DN_UNIT, r2 - DN_UNIT, r2)
    same2 = (rr // c) == (c2 // c)
    summer = jnp.where(same2 & ((r2 >= DN_UNIT) | (rr >= c2)), 1.0, 0.0).astype(F32)

    ri = lax.broadcasted_iota(jnp.int32, (DN_PAIR, DN_PAIR), 0)
    ci = lax.broadcasted_iota(jnp.int32, (DN_PAIR, DN_PAIR), 1)
    same = (ri // c) == (ci // c)
    tri = same & (ri >= ci)
    stri = same & (ri > ci)
    eye = jnp.where(ri == ci, 1.0, 0.0).astype(F32)
    items = [(slot, pair) for slot in range(ups) for pair in range(N_HEADS_C // 2)]
    st = []
    for slot, pair in items:
        unit_rows = slice(slot * DN_UNIT, (slot + 1) * DN_UNIT)
        yu = y[unit_rows]
        if pair == 0:
            cg = jnp.dot(summer, g_f[unit_rows], precision=HIGHEST, preferred_element_type=F32)
        heads = (2 * pair, 2 * pair + 1)
        stack = lambda first: jnp.concatenate([yu[:, first + h * hd:first + (h + 1) * hd] for h in heads], 0)
        cols = lambda x, first: jnp.concatenate([x[:, first + h:first + h + 1] for h in heads], 0)
        qs, ks, vs = stack(0), stack(V_C), stack(2 * V_C)
        qn = qs * lax.rsqrt(jnp.sum(qs * qs, -1, keepdims=True) + RMS_EPS) * (hd ** -0.5)
        kn = ks * lax.rsqrt(jnp.sum(ks * ks, -1, keepdims=True) + RMS_EPS)
        beta_s = cols(beta_f[unit_rows], 0)
        gcs = cols(cg[0:DN_UNIT], N_HEADS_C)
        gls = cols(cg[DN_UNIT:2 * DN_UNIT], N_HEADS_C)
        grow = jnp.broadcast_to(gcs, (DN_PAIR, LANE)).T[0:1, :]
        decay = jnp.where(tri, jnp.exp(jnp.where(tri, gcs - grow, 0.0)), 0.0)
        kb = kn.astype(BF16)
        kk = lax.dot_general(kb, kb, nt, preferred_element_type=F32)
        nmat = jnp.where(stri, -(beta_s * kk * decay), 0.0)
        egc = jnp.exp(gcs)
        st.append(dict(qn=qn, kn=kn, kb=kb, decay=decay, nmat=nmat, egc=egc, gcs=gcs, gls=gls,
                       rhs=jnp.concatenate([vs * beta_s, kn * (beta_s * egc)], -1).astype(BF16)))

    invs = [eye + s["nmat"] for s in st]
    nbs = [s["nmat"].astype(BF16) for s in st]
    for _ in range(int(math.log2(c)) - 1):
        nbs = [jnp.dot(nb, nb, preferred_element_type=F32).astype(BF16) for nb in nbs]
        invs = [inv + jnp.dot(inv.astype(BF16), nb, preferred_element_type=F32) for inv, nb in zip(invs, nbs)]
    sols = [jnp.dot(inv.astype(BF16), s["rhs"], preferred_element_type=F32) for inv, s in zip(invs, st)]

    gl_rows = {slot: [] for slot in range(ups)}
    for (slot, pair), s, sol in zip(items, st, sols):
        rows = slice(pair * DN_PAIR, (pair + 1) * DN_PAIR)
        u_ref[slot, rows, :] = sol[:, :hd]
        w_ref[slot, rows, :] = sol[:, hd:].astype(BF16)
        qk = jnp.where(tri, lax.dot_general(s["qn"].astype(BF16), s["kb"], nt, preferred_element_type=F32)
                       * s["decay"], 0.0)
        qk_ref[slot, rows, :] = qk.astype(BF16)
        qd_ref[slot, rows, :] = (s["qn"] * s["egc"]).astype(BF16)
        kdt_ref[slot, :, rows] = (s["kn"] * jnp.exp(s["gls"] - s["gcs"])).T.astype(BF16)
        glb = jnp.broadcast_to(jnp.exp(s["gls"]), (DN_PAIR, LANE))
        gl_rows[slot] += [glb[g * c:g * c + 1, :] for g in range(DN_PAIR // c)]
    groups = DN_STACK // c
    for slot in range(ups):
        if groups < DN_GL_ROWS:
            gl_rows[slot].append(jnp.ones((DN_GL_ROWS - groups, LANE), F32))
        gl_ref[slot] = jnp.concatenate(gl_rows[slot], 0)


def _dn_prep(src, small_src, wc, alog_row, dtb_row, *, qcol, smcol, nbatch, nchunk, c, lr):
    nu = nbatch * nchunk
    total = nu
    ups = 2 if (nchunk % 2 == 0 or (nchunk == 1 and nbatch % 2 == 0)) else 1
    if nchunk == 1:
        nbatch //= ups
    else:
        nchunk //= ups
    nu = total
    rows = ups * DN_UNIT
    per8 = rows // SUBLANE

    def cur(col):
        return pl.BlockSpec((rows, V_C), lambda b, n: (b * nchunk + n, col))

    def prev(col):
        return pl.BlockSpec((HALO_C, V_C), lambda b, n: (jnp.maximum((b * nchunk + n) * per8 - 1, 0), col))

    unit = lambda r, cdim: pl.BlockSpec((ups, r, cdim), lambda b, n: (b * nchunk + n, 0, 0))
    return pl.pallas_call(
        functools.partial(_dn_prep_kernel, c=c, lr=lr, ups=ups),
        grid=(nbatch, nchunk),
        in_specs=[cur(qcol), cur(qcol + 1), cur(qcol + 2), prev(qcol), prev(qcol + 1), prev(qcol + 2),
                  pl.BlockSpec((rows, LANE), lambda b, n: (b * nchunk + n, smcol)),
                  pl.BlockSpec((SUBLANE, QKV_C), lambda b, n: (0, 0)),
                  pl.BlockSpec((1, LANE), lambda b, n: (0, 0)),
                  pl.BlockSpec((1, LANE), lambda b, n: (0, 0))],
        out_specs=[unit(DN_STACK, HEAD_DIM_C), unit(DN_STACK, HEAD_DIM_C), unit(DN_STACK, HEAD_DIM_C),
                   unit(HEAD_DIM_C, DN_STACK), unit(DN_STACK, DN_PAIR), unit(DN_GL_ROWS, LANE)],
        out_shape=[jax.ShapeDtypeStruct((nu, DN_STACK, HEAD_DIM_C), F32),
                   jax.ShapeDtypeStruct((nu, DN_STACK, HEAD_DIM_C), BF16),
                   jax.ShapeDtypeStruct((nu, DN_STACK, HEAD_DIM_C), BF16),
                   jax.ShapeDtypeStruct((nu, HEAD_DIM_C, DN_STACK), BF16),
                   jax.ShapeDtypeStruct((nu, DN_STACK, DN_PAIR), BF16),
                   jax.ShapeDtypeStruct((nu, DN_GL_ROWS, LANE), F32)],
        scratch_shapes=[pltpu.VMEM((HALO_C + rows, QKV_C), F32)],
        compiler_params=_cparams(("arbitrary", "arbitrary")),
        name="deltanet_prep",
    )(src, src, src, src, src, src, small_src, wc, alog_row, dtb_row)


def _dn_seq_kernel(u_ref, w_ref, qd_ref, kdt_ref, qk_ref, gl_ref, *rest, c, nsq, streams, carry):
    z_refs = rest[:streams]
    nw_ref, s0_ref, o_ref, sout_ref, sbuf = rest[streams:]
    groups = DN_STACK // c
    hd = HEAD_DIM_C
    span = groups * hd
    step = pl.program_id(0)

    def load_state():
        for j in range(streams):
            for g in range(groups):
                sbuf[j * span + g * hd:j * span + (g + 1) * hd, :] = s0_ref[j, 0, g % nsq, g // nsq]

    def store_state():
        for j in range(streams):
            for g in range(groups):
                sout_ref[j, 0, g % nsq, g // nsq] = sbuf[j * span + g * hd:j * span + (g + 1) * hd, :]

    if carry:
        pl.when(step == 0)(load_state)
    else:
        load_state()

    col_group = lax.broadcasted_iota(jnp.int32, (hd, DN_STACK), 1) // c
    vbs, o_inter = [], []
    for j in range(streams):
        u = u_ref[j, 0]
        w = w_ref[j, 0]
        qd = qd_ref[j, 0]
        v_parts, o_parts = [], []
        for g in range(groups):
            rows = slice(g * c, (g + 1) * c)
            s_b = sbuf[j * span + g * hd:j * span + (g + 1) * hd, :].astype(BF16)
            r = jnp.dot(jnp.concatenate([w[rows], qd[rows]], 0), s_b, preferred_element_type=F32)
            v_parts.append(u[rows] - r[:c])
            o_parts.append(r[c:])
        vbs.append(jnp.concatenate(v_parts, 0).astype(BF16))
        o_inter.append(jnp.concatenate(o_parts, 0))

    o_all = []
    for j in range(streams):
        qk = qk_ref[j, 0]
        o_all.append(o_inter[j] + jnp.concatenate(
            [jnp.dot(qk[p * DN_PAIR:(p + 1) * DN_PAIR], vbs[j][p * DN_PAIR:(p + 1) * DN_PAIR],
                     preferred_element_type=F32) for p in range(DN_STACK // DN_PAIR)], 0))

    for j in range(streams):
        kdt = kdt_ref[j, 0]
        k_big = jnp.concatenate([jnp.where(col_group == g, kdt, jnp.zeros_like(kdt)) for g in range(groups)], 0)
        gl = gl_ref[j, 0]
        scale = jnp.concatenate([jnp.broadcast_to(gl[g:g + 1, :], (hd, LANE)) for g in range(groups)], 0)
        sbuf[j * span:(j + 1) * span, :] = (sbuf[j * span:(j + 1) * span, :] * scale
                                             + jnp.dot(k_big, vbs[j], preferred_element_type=F32))

    for j in range(streams):
        outs = []
        for h in range(N_HEADS_C):
            oh = o_all[j][h * DN_UNIT:(h + 1) * DN_UNIT]
            on = oh * lax.rsqrt(jnp.mean(oh * oh, -1, keepdims=True) + RMS_EPS) * nw_ref[...]
            zh = z_refs[j][:, h * hd:(h + 1) * hd]
            outs.append(on * (zh * _sigmoid(zh)))
        o_ref[j] = jnp.concatenate(outs, -1).astype(o_ref.dtype)

    if carry:
        pl.when(step == pl.num_programs(0) - 1)(store_state)
    else:
        store_state()


def _dn_seq(prep, z_src, nw, s0, *, zcol, streams, steps, c, nsq, carry):
    groups = DN_STACK // c
    prep = [t.reshape((streams, steps) + t.shape[1:]) for t in prep]
    unit = lambda r, cdim: pl.BlockSpec((streams, 1, r, cdim), lambda i: (0, i, 0, 0))
    sdim = (streams, 1, nsq, N_HEADS_C, HEAD_DIM_C, HEAD_DIM_C)
    state = pl.BlockSpec(sdim, (lambda i: (0, 0, 0, 0, 0, 0)) if carry else (lambda i: (0, i, 0, 0, 0, 0)))
    z_specs = [pl.BlockSpec((DN_UNIT, V_C), functools.partial(lambda i, j: (j * steps + i, zcol), j=j))
               for j in range(streams)]
    o, s_out = pl.pallas_call(
        functools.partial(_dn_seq_kernel, c=c, nsq=nsq, streams=streams, carry=carry),
        grid=(steps,),
        in_specs=[unit(DN_STACK, HEAD_DIM_C), unit(DN_STACK, HEAD_DIM_C), unit(DN_STACK, HEAD_DIM_C),
                  unit(HEAD_DIM_C, DN_STACK), unit(DN_STACK, DN_PAIR), unit(DN_GL_ROWS, LANE)]
                 + z_specs + [pl.BlockSpec((1, HEAD_DIM_C), lambda i: (0, 0)), state],
        out_specs=[pl.BlockSpec((streams, DN_UNIT, V_C), lambda i: (0, i, 0)), state],
        out_shape=[jax.ShapeDtypeStruct((streams, steps * DN_UNIT, V_C), BF16),
                   jax.ShapeDtypeStruct(s0.shape, F32)],
        scratch_shapes=[pltpu.VMEM((streams * groups * HEAD_DIM_C, HEAD_DIM_C), F32)],
        compiler_params=_cparams(("arbitrary",)),
        name="deltanet_seq",
    )(*prep, *([z_src] * streams), nw, s0)
    return o.reshape(streams * steps * DN_UNIT, V_C), s_out


def _merge_kernel(oa_ref, cb_ref, oc_ref, ga_ref, gb_ref, gc_ref, x_ref, woa_ref, wob_ref, woc_ref, wout_ref,
                  g1_ref, b1_ref, wr_ref, br_ref, x1_ref, route_ref, gate_ref, cnt_ref, carry, *, alpha):
    @pl.when(pl.program_id(0) == 0)
    def _():
        carry[...] = jnp.zeros(carry.shape, F32)

    br_a = jnp.dot(oa_ref[...], woa_ref[...], preferred_element_type=F32)
    br_b = jnp.dot(cb_ref[...], wob_ref[...], preferred_element_type=F32)
    br_c = jnp.dot(oc_ref[...], woc_ref[...], preferred_element_type=F32)
    mixin = _sigmoid(ga_ref[...]) * br_a + _sigmoid(gb_ref[...]) * br_b + _sigmoid(gc_ref[...]) * br_c
    mix = jnp.dot(mixin.astype(BF16), wout_ref[...], preferred_element_type=F32)
    x1 = _layer_norm(alpha * x_ref[...] + mix, g1_ref[...], b1_ref[...])
    x1_ref[...] = x1
    logits = jnp.dot(x1.astype(BF16), wr_ref[...], preferred_element_type=F32) + br_ref[...]
    lane = lax.broadcasted_iota(jnp.int32, logits.shape, 1)
    logits = jnp.where(lane < N_EXPERTS, logits, -jnp.inf)
    route = jnp.zeros(logits.shape, jnp.int32)
    val_out = jnp.zeros(logits.shape, F32)
    member = jnp.zeros(logits.shape, F32)
    sels = []
    top0 = None
    den = None
    for k in range(TOP_K):
        m = jnp.max(logits, -1, keepdims=True)
        sel = jnp.min(jnp.where(logits == m, lane, LANE), -1, keepdims=True)
        if k == 0:
            top0 = m
            e = jnp.ones_like(m)
            den = e
        else:
            e = jnp.exp(m - top0)
            den = den + e
        hit = lane == sel
        route = jnp.where(lane == k, sel, route)
        val_out = jnp.where(lane == k, e, val_out)
        member = jnp.where(hit, 1.0, member)
        logits = jnp.where(hit, -jnp.inf, logits)
        sels.append(sel)
    gate_ref[...] = val_out / den
    tm = logits.shape[0]
    earlier = lax.broadcasted_iota(jnp.int32, (tm, tm), 0) > lax.broadcasted_iota(jnp.int32, (tm, tm), 1)
    cum = jnp.dot(jnp.where(earlier, 1.0, 0.0).astype(BF16), member.astype(BF16),
                  preferred_element_type=F32) + carry[...]
    for k in range(TOP_K):
        rank = jnp.sum(jnp.where(lane == sels[k], cum, 0.0), -1, keepdims=True)
        route = jnp.where(lane == TOP_K + k, rank.astype(jnp.int32), route)
    route_ref[...] = route
    carry[...] = carry[...] + jnp.sum(member, axis=0, keepdims=True)
    cnt_ref[...] = carry[...]


def _merge(oa, cb, oc, h, x, lw, alpha):
    m = x.shape[0]
    tm = _tile(m, 512)
    row = lambda i: (i, 0)
    full = lambda i: (0, 0)
    gcol = COL_GATE // D_MODEL
    return pl.pallas_call(
        functools.partial(_merge_kernel, alpha=alpha),
        grid=(m // tm,),
        in_specs=[pl.BlockSpec((tm, Q_A), row), pl.BlockSpec((tm, C_B), row), pl.BlockSpec((tm, V_C), row),
                  pl.BlockSpec((tm, D_MODEL), lambda i: (i, gcol)),
                  pl.BlockSpec((tm, D_MODEL), lambda i: (i, gcol + 1)),
                  pl.BlockSpec((tm, D_MODEL), lambda i: (i, gcol + 2)),
                  pl.BlockSpec((tm, D_MODEL), row),
                  pl.BlockSpec((Q_A, D_MODEL), full), pl.BlockSpec((C_B, D_MODEL), full),
                  pl.BlockSpec((V_C, D_MODEL), full), pl.BlockSpec((D_MODEL, D_MODEL), full),
                  pl.BlockSpec((1, D_MODEL), full), pl.BlockSpec((1, D_MODEL), full),
                  pl.BlockSpec((D_MODEL, LANE), full), pl.BlockSpec((1, LANE), full)],
        out_specs=[pl.BlockSpec((tm, D_MODEL), row), pl.BlockSpec((tm, LANE), row),
                   pl.BlockSpec((tm, LANE), row), pl.BlockSpec((1, LANE), full)],
        out_shape=[jax.ShapeDtypeStruct((m, D_MODEL), F32), jax.ShapeDtypeStruct((m, LANE), jnp.int32),
                   jax.ShapeDtypeStruct((m, LANE), F32), jax.ShapeDtypeStruct((1, LANE), F32)],
        scratch_shapes=[pltpu.VMEM((1, LANE), F32)],
        compiler_params=_cparams(("arbitrary",)),
        name="merge_router",
    )(oa, cb, oc, h, h, h, x, lw["w_o_a"], lw["w_o_b"], lw["w_o_c"], lw["w_out"], lw["ln1_g"], lw["ln1_b"],
      lw["w_router"], lw["b_router"])


def _row_copy(src, src_row, dst, dst_row, sem):
    return pltpu.make_async_copy(src.at[pl.ds(src_row, 1)], dst.at[pl.ds(dst_row, 1)], sem)


def _dispatch_kernel(dest_ref, last_ref, x_ref, xs_ref, zbuf, sem, *, tt):
    @pl.when(pl.program_id(0) == 0)
    def _():
        zbuf[...] = jnp.zeros(zbuf.shape, F32)
        for e in range(N_EXPERTS):
            first_row = pl.multiple_of(last_ref[e], MOE_ROWS)
            pltpu.make_async_copy(zbuf, xs_ref.at[pl.ds(first_row, MOE_ROWS)], sem).start()
        for e in range(N_EXPERTS):
            pltpu.make_async_copy(zbuf, xs_ref.at[pl.ds(0, MOE_ROWS)], sem).wait()

    def body(t, carry):
        for k in range(TOP_K):
            _row_copy(x_ref, t, xs_ref, dest_ref[t * TOP_K + k], sem).start(priority=k % 2)
        return carry

    lax.fori_loop(0, tt, body, 0, unroll=4)
    for k in range(TOP_K):
        pltpu.make_async_copy(x_ref, xs_ref.at[pl.ds(0, tt)], sem).wait()


def _dispatch(dest, last_blk, x1, n_rows):
    m = x1.shape[0]
    tt = _tile(m, 512)
    return pl.pallas_call(
        functools.partial(_dispatch_kernel, tt=tt),
        grid=(m // tt,),
        in_specs=[pl.BlockSpec((tt * TOP_K,), lambda i: (i,), memory_space=pltpu.SMEM),
                  pl.BlockSpec(memory_space=pltpu.SMEM),
                  pl.BlockSpec((tt, D_MODEL), lambda i: (i, 0))],
        out_specs=pl.BlockSpec(memory_space=pl.ANY),
        out_shape=jax.ShapeDtypeStruct((n_rows, D_MODEL), F32),
        scratch_shapes=[pltpu.VMEM((MOE_ROWS, D_MODEL), F32), pltpu.SemaphoreType.DMA(())],
        compiler_params=pltpu.CompilerParams(dimension_semantics=("arbitrary",), vmem_limit_bytes=VMEM_LIMIT,
                                             disable_bounds_checks=True),
        name="moe_dispatch",
    )(dest, last_blk, x1)


def _moe_kernel(be_ref, na_ref, x_ref, wu_ref, bu_ref, wd_ref, bd_ref, o_ref, wu_b, wd_b):
    i = pl.program_id(0)
    active = i < na_ref[0]
    new_expert = jnp.logical_or(i == 0, be_ref[i] != be_ref[jnp.maximum(i - 1, 0)])

    @pl.when(jnp.logical_and(active, new_expert))
    def _():
        wu_b[...] = wu_ref[0, 0].astype(BF16)
        wd_b[...] = wd_ref[0, 0].astype(BF16)

    @pl.when(active)
    def _():
        hu = jnp.dot(x_ref[...].astype(BF16), wu_b[...], preferred_element_type=F32) + bu_ref[0, 0]
        gt = jnp.minimum(hu[:, :D_FF], SWIGLU_LIMIT)
        up = jnp.clip(hu[:, D_FF:], -SWIGLU_LIMIT, SWIGLU_LIMIT)
        act = (up + 1.0) * gt * _sigmoid(SWIGLU_ALPHA * gt)
        o_ref[...] = jnp.dot(act.astype(BF16), wd_b[...], preferred_element_type=F32) + bd_ref[0, 0]

    @pl.when(jnp.logical_not(active))
    def _():
        o_ref[...] = jnp.zeros(o_ref.shape, F32)


def _moe(xs, blk_e, n_active, w_up, b_up, w_down, b_down, layer):
    nblk = xs.shape[0] // MOE_ROWS

    def blk(i, be, na):
        return jnp.minimum(i, jnp.maximum(na[0] - 1, 0))

    grid_spec = pltpu.PrefetchScalarGridSpec(
        num_scalar_prefetch=2,
        grid=(nblk,),
        in_specs=[pl.BlockSpec((MOE_ROWS, D_MODEL), lambda i, be, na: (blk(i, be, na), 0)),
                  pl.BlockSpec((1, 1, D_MODEL, 2 * D_FF), lambda i, be, na: (layer, be[blk(i, be, na)], 0, 0)),
                  pl.BlockSpec((1, 1, 1, 2 * D_FF), lambda i, be, na: (layer, be[blk(i, be, na)], 0, 0)),
                  pl.BlockSpec((1, 1, D_FF, D_MODEL), lambda i, be, na: (layer, be[blk(i, be, na)], 0, 0)),
                  pl.BlockSpec((1, 1, 1, D_MODEL), lambda i, be, na: (layer, be[blk(i, be, na)], 0, 0))],
        out_specs=pl.BlockSpec((MOE_ROWS, D_MODEL), lambda i, be, na: (i, 0)),
        scratch_shapes=[pltpu.VMEM((D_MODEL, 2 * D_FF), BF16), pltpu.VMEM((D_FF, D_MODEL), BF16)],
    )
    return pl.pallas_call(
        _moe_kernel,
        grid_spec=grid_spec,
        out_shape=jax.ShapeDtypeStruct((nblk * MOE_ROWS, D_MODEL), F32),
        compiler_params=_cparams(("arbitrary",)),
        name="moe_experts",
    )(blk_e, n_active, xs, w_up, b_up, w_down, b_down)

def _post_kernel(dest_ref, yb_ref, gate_ref, x1_ref, p_ref, wg_ref, wp_ref, g2_ref, b2_ref, x_ref, xb_ref,
                 ybuf, sem, *, alpha, tm):
    def body(t, carry):
        for k in range(TOP_K):
            _row_copy(yb_ref, dest_ref[t * TOP_K + k], ybuf.at[k], t, sem).start(priority=k % 2)
        return carry

    lax.fori_loop(0, tm, body, 0, unroll=4)
    for k in range(TOP_K):
        pltpu.make_async_copy(yb_ref.at[pl.ds(0, tm)], ybuf.at[k], sem).wait()

    gate = gate_ref[...]
    ffn = gate[:, 0:1] * ybuf[0]
    for k in range(1, TOP_K):
        ffn = ffn + gate[:, k:k + 1] * ybuf[k]
    x2 = _layer_norm(alpha * x1_ref[...] + ffn, g2_ref[...], b2_ref[...])
    gl = jnp.dot(x2.astype(BF16), wg_ref[...], preferred_element_type=F32)
    pe = jnp.dot(p_ref[...].astype(BF16), wp_ref[...], preferred_element_type=F32)
    xo = x2 + _sigmoid(gl) * pe
    x_ref[...] = xo
    xb_ref[...] = xo.astype(BF16)


def _post(dest, yb, gate, x1, p, lw, alpha):
    m = x1.shape[0]
    tm = _tile(m, 512)
    row = lambda i: (i, 0)
    full = lambda i: (0, 0)
    return pl.pallas_call(
        functools.partial(_post_kernel, alpha=alpha, tm=tm),
        grid=(m // tm,),
        in_specs=[pl.BlockSpec((tm * TOP_K,), lambda i: (i,), memory_space=pltpu.SMEM),
                  pl.BlockSpec(memory_space=pl.ANY),
                  pl.BlockSpec((tm, LANE), row),
                  pl.BlockSpec((tm, D_MODEL), row), pl.BlockSpec((tm, PLE_DIM), row),
                  pl.BlockSpec((D_MODEL, D_MODEL), full), pl.BlockSpec((PLE_DIM, D_MODEL), full),
                  pl.BlockSpec((1, D_MODEL), full), pl.BlockSpec((1, D_MODEL), full)],
        out_specs=[pl.BlockSpec((tm, D_MODEL), row), pl.BlockSpec((tm, D_MODEL), row)],
        out_shape=[jax.ShapeDtypeStruct((m, D_MODEL), F32), jax.ShapeDtypeStruct((m, D_MODEL), BF16)],
        scratch_shapes=[pltpu.VMEM((TOP_K, tm, D_MODEL), F32), pltpu.SemaphoreType.DMA(())],
        compiler_params=pltpu.CompilerParams(dimension_semantics=("arbitrary",), vmem_limit_bytes=VMEM_LIMIT,
                                             disable_bounds_checks=True),
        name="combine_ple",
    )(dest, yb, gate, x1, p, lw["w_ple_gate"], lw["w_ple_proj"], lw["ln2_g"], lw["ln2_b"])


def _rel_bucket(n):
    max_exact = NUM_BUCKETS // 2
    nf = jnp.maximum(n, 1).astype(F32)
    large = max_exact + (jnp.log(nf / max_exact) / math.log(MAX_DISTANCE / max_exact)
                         * (NUM_BUCKETS - max_exact)).astype(jnp.int32)
    return jnp.where(n < max_exact, jnp.maximum(n, 0), jnp.minimum(large, NUM_BUCKETS - 1))


def _bias_table(rel_bias, dist, valid):
    b = rel_bias.astype(F32)[_rel_bucket(dist)]
    b = jnp.where(valid[..., None], b, NEG_INF)
    return jnp.moveaxis(b, -1, 0)


def _permute_in(w):
    o = np.concatenate([[0], np.cumsum(IN_SIZES)])
    seg = lambda i: w[..., o[i]:o[i + 1]]
    small = jnp.concatenate([seg(6), seg(7)], -1)
    small = jnp.pad(small, [(0, 0)] * (w.ndim - 1) + [(0, LANE - 2 * N_HEADS_C)])
    return jnp.concatenate([seg(8), seg(3), seg(4), seg(5), seg(0), seg(1), seg(2), small], -1)


def _route(route, counts, n_tok):
    n_blk = -(-n_tok * TOP_K // MOE_ROWS) + N_EXPERTS
    cnt = counts[0, :N_EXPERTS].astype(jnp.int32)
    padded = (cnt + MOE_ROWS - 1) // MOE_ROWS * MOE_ROWS
    pad_end = jnp.cumsum(padded)
    pad_start = pad_end - padded
    idx = route[:, 0:TOP_K]
    rank = route[:, TOP_K:2 * TOP_K]
    start_of = jnp.sum(jnp.where(idx[:, :, None] == jnp.arange(N_EXPERTS)[None, None, :],
                                 pad_start[None, None, :], 0), -1)
    dest = (start_of + rank).reshape(n_tok * TOP_K).astype(jnp.int32)
    blk_first = (jnp.arange(n_blk) * MOE_ROWS)[:, None]
    blk_e = jnp.minimum(jnp.sum((pad_end[None, :] <= blk_first).astype(jnp.int32), -1), N_EXPERTS - 1)
    n_active = (pad_end[-1:] // MOE_ROWS).astype(jnp.int32)
    last_blk = jnp.maximum(pad_end - MOE_ROWS, 0).astype(jnp.int32)
    return dest, last_blk, blk_e.astype(jnp.int32), n_active, n_blk


def kernel(x_prompt, x_sample, cache_k_a, cache_v_a, state_conv_b, state_conv_c, state_s_c, p_prompt, p_sample,
           rel_bias, w_in, b_in, sink_a, w_o_a, w_dw_b, b_dw_b, ln_b_g, ln_b_b, w_o_b, w_conv_c, a_log_c,
           dt_bias_c, norm_c, w_o_c, w_out, ln1_g, ln1_b, w_router, b_router, w_up, b_up, w_down, b_down,
           ln2_g, ln2_b, w_ple_gate, w_ple_proj):
    depth = w_in.shape[0]
    nb_p, seq, _ = x_prompt.shape
    ns, ls, _ = x_sample.shape
    w_buf = cache_k_a.shape[2]
    alpha = (2 * depth) ** 0.25
    tp = nb_p * seq
    tsm = ns * ls
    n_tok = tp + tsm
    assert seq % DN_CHUNK == 0 and seq % Q_BLOCK == 0 and w_buf == WINDOW and ls <= 16

    w_in_p = _permute_in(w_in).astype(BF16)
    b_in_p = _permute_in(b_in)[:, None, :]
    w_o_a_b, w_o_b_b, w_o_c_b, w_out_b = (t.astype(BF16) for t in (w_o_a, w_o_b, w_o_c, w_out))
    w_ple_gate_b, w_ple_proj_b = w_ple_gate.astype(BF16), w_ple_proj.astype(BF16)
    b_up4, b_down4 = b_up[:, :, None, :], b_down[:, :, None, :]
    w_router_p = jnp.pad(w_router, ((0, 0), (0, 0), (0, LANE - N_EXPERTS))).astype(BF16)
    b_router_p = jnp.pad(b_router, ((0, 0), (0, LANE - N_EXPERTS)))[:, None, :]
    w_dw_p = jnp.pad(w_dw_b, ((0, 0), (0, HALO_B - CONV_B), (0, 0)))
    w_cc_p = jnp.pad(w_conv_c, ((0, 0), (0, SUBLANE - CONV_C), (0, 0)))
    lane_pad = lambda t: jnp.pad(t, ((0, 0), (N_HEADS_C, LANE - 2 * N_HEADS_C)))[:, None, :]
    alog_rows, dtb_rows = lane_pad(a_log_c), lane_pad(dt_bias_c)

    qi = jnp.arange(Q_BLOCK)[:, None]
    kj = jnp.arange(2 * Q_BLOCK)[None, :]
    dist = qi - kj + Q_BLOCK
    valid = (dist >= 0) & (dist < WINDOW)
    tab_rest = _bias_table(rel_bias, dist, valid)
    tab_first = _bias_table(rel_bias, dist, valid & (kj >= Q_BLOCK))
    bias_prompt = jnp.stack([tab_first, tab_rest])

    kwin = 2 * WINDOW
    ti = jnp.arange(ls)[:, None]
    kjs = jnp.arange(kwin)[None, :]
    dist_s = w_buf + ti - kjs
    valid_s = (dist_s >= 0) & (dist_s < WINDOW) & (kjs < w_buf + ls)
    tab_s = _bias_table(rel_bias, dist_s, valid_s)
    tab_s = tab_s.reshape(N_KV_A, GROUP_A * ls, kwin)
    gb = 2 * _tile(ns, 16, 1)
    bias_s = jnp.tile(tab_s, (gb // N_KV_A, 1, 1))

    wrows = -(-(CONV_B - 1 + ls) // SUBLANE) * SUBLANE
    jj = jnp.arange(wrows)[None, :] - jnp.arange(ls)[:, None]
    tap_ok = (jj >= 0) & (jj < CONV_B)

    x_all = jnp.concatenate([x_prompt.reshape(tp, D_MODEL), x_sample.reshape(tsm, D_MODEL)], 0)
    xb_all = x_all.astype(BF16)
    p_all = jnp.concatenate([p_prompt.reshape(depth, tp, PLE_DIM), p_sample.reshape(depth, tsm, PLE_DIM)], 1)
    zero_s = jnp.zeros((nb_p, 1, 1, N_HEADS_C, HEAD_DIM_C, HEAD_DIM_C), F32)
    c_s = 16
    nsq_s = DN_UNIT // c_s
    assert ls + CONV_C - 1 <= c_s and ns % nsq_s == 0

    st_p, st_s = [], []
    for i in range(depth):
        lw = {"w_o_a": w_o_a_b[i], "w_o_b": w_o_b_b[i], "w_o_c": w_o_c_b[i], "w_out": w_out_b[i],
              "ln1_g": ln1_g[i][None], "ln1_b": ln1_b[i][None], "w_router": w_router_p[i],
              "b_router": b_router_p[i], "w_ple_gate": w_ple_gate_b[i], "w_ple_proj": w_ple_proj_b[i],
              "ln2_g": ln2_g[i][None], "ln2_b": ln2_b[i][None]}
        h = _in_proj(xb_all, w_in_p[i], b_in_p[i])
        hs = h[tp:].reshape(ns, ls, N_IN_PAD)

        sinks = sink_a[i].astype(F32)
        oa_p = _attn_prompt(h, sinks, bias_prompt, nb_p, seq)
        q_s = hs[:, :, COL_QA:COL_QA + Q_A].reshape(ns, ls, N_KV_A, GROUP_A, HEAD_DIM_A)
        q_s = q_s.transpose(0, 2, 3, 1, 4).reshape(ns * N_KV_A, GROUP_A * ls, HEAD_DIM_A)
        k_new = hs[:, :, COL_KA:COL_KA + KV_A].reshape(ns, ls, N_KV_A, HEAD_DIM_A)
        v_new = hs[:, :, COL_VA:COL_VA + KV_A].reshape(ns, ls, N_KV_A, HEAD_DIM_A)
        k_all = jnp.concatenate([cache_k_a[i], k_new], 1)
        v_all = jnp.concatenate([cache_v_a[i], v_new], 1)

        def keywin(t):
            t = jnp.pad(t, ((0, 0), (0, kwin - w_buf - ls), (0, 0), (0, 0)))
            return t.transpose(0, 2, 1, 3).reshape(ns * N_KV_A, kwin, HEAD_DIM_A)

        sink_s = jnp.tile(jnp.repeat(sinks.reshape(N_KV_A, GROUP_A), ls, axis=1)[:, :, None], (gb // N_KV_A, 1, 1))
        oa_s = _attn_sample(q_s, keywin(k_all), keywin(v_all), bias_s, sink_s)
        oa_s = oa_s.reshape(ns, N_KV_A, GROUP_A, ls, HEAD_DIM_A).transpose(0, 3, 1, 2, 4).reshape(tsm, Q_A)
        oa = jnp.concatenate([oa_p, oa_s.astype(BF16)], 0)

        cb_p, tail_p = _convb_prompt(h, w_dw_p[i], b_dw_b[i][None], ln_b_g[i][None], ln_b_b[i][None], nb_p, seq)
        wt = jnp.where(tap_ok[:, :, None], w_dw_b[i][jnp.clip(jj, 0, CONV_B - 1)], 0.0)
        cb_s, nconv_b_s = _convb_sample(hs, state_conv_b[i], wt, b_dw_b[i][None], ln_b_g[i][None], ln_b_b[i][None])
        cb = jnp.concatenate([cb_p, cb_s.transpose(1, 0, 2).reshape(tsm, C_B)], 0)

        nw = norm_c[i][None]
        nchunk = seq // DN_CHUNK
        prep_p = _dn_prep(h, h, w_cc_p[i], alog_rows[i], dtb_rows[i], qcol=COL_QKVC // V_C,
                          smcol=COL_SMALL // LANE, nbatch=nb_p, nchunk=nchunk, c=DN_CHUNK, lr=DN_CHUNK)
        oc_p, s_p = _dn_seq(prep_p, h, nw, zero_s, zcol=COL_Z // V_C, streams=nb_p, steps=nchunk,
                            c=DN_CHUNK, nsq=1, carry=True)
        s_p = s_p.reshape(nb_p, N_HEADS_C, HEAD_DIM_C, HEAD_DIM_C)
        lead = c_s - ls
        hist = jnp.concatenate([jnp.zeros((ns, lead - (CONV_C - 1), QKV_C), F32), state_conv_c[i],
                                hs[:, :, COL_QKVC:COL_QKVC + QKV_C]], 1).reshape(ns * c_s, QKV_C)
        pad_s = lambda col, wid: jnp.pad(hs[:, :, col:col + wid], ((0, 0), (lead, 0), (0, 0))).reshape(ns * c_s, wid)
        prep_s = _dn_prep(hist, pad_s(COL_SMALL, LANE), w_cc_p[i], alog_rows[i], dtb_rows[i], qcol=0, smcol=0,
                          nbatch=ns // nsq_s, nchunk=1, c=c_s, lr=ls)
        units_s = ns // nsq_s
        streams_s = 2 if units_s % 2 == 0 else 1
        s0_s = state_s_c[i].astype(F32).reshape(streams_s, units_s // streams_s, nsq_s, N_HEADS_C, HEAD_DIM_C,
                                                HEAD_DIM_C)
        oc_s, s_s = _dn_seq(prep_s, pad_s(COL_Z, V_C), nw, s0_s, zcol=0, streams=streams_s,
                            steps=units_s // streams_s, c=c_s, nsq=nsq_s, carry=False)
        s_s = s_s.reshape(ns, N_HEADS_C, HEAD_DIM_C, HEAD_DIM_C)
        oc = jnp.concatenate([oc_p, oc_s.reshape(ns, c_s, V_C)[:, lead:].reshape(tsm, V_C)], 0)

        x1, route, gate, counts = _merge(oa, cb, oc, h, x_all, lw, alpha)

        dest, last_blk, blk_e, n_active, n_blk = _route(route, counts, n_tok)
        xs = _dispatch(dest, last_blk, x1, n_blk * MOE_ROWS)
        yb = _moe(xs, blk_e, n_active, w_up, b_up4, w_down, b_down4, i)

        x_all, xb_all = _post(dest, yb, gate, x1, p_all[i], lw, alpha)

        def tail_rows(nrows, col, width):
            return jnp.stack([h[(b + 1) * seq - nrows:(b + 1) * seq, col:col + width] for b in range(nb_p)])

        kp_new = tail_rows(w_buf, COL_KA, KV_A)
        vp_new = tail_rows(w_buf, COL_VA, KV_A)
        ccp_new = tail_rows(CONV_C - 1, COL_QKVC, QKV_C)
        st_p.append((kp_new.reshape(nb_p, w_buf, N_KV_A, HEAD_DIM_A), vp_new.reshape(nb_p, w_buf, N_KV_A, HEAD_DIM_A),
                     tail_p[:, HALO_B - (CONV_B - 1):], ccp_new, s_p))
        ccs_ext = jnp.concatenate([state_conv_c[i], hs[:, :, COL_QKVC:COL_QKVC + QKV_C]], 1)
        st_s.append((k_all[:, ls:], v_all[:, ls:], nconv_b_s, ccs_ext[:, -(CONV_C - 1):], s_s))

    yp = x_all[:tp].reshape(nb_p, seq, D_MODEL)
    ys = x_all[tp:].reshape(ns, ls, D_MODEL)
    kp, vp, cbp, ccp, scp = [jnp.stack(z) for z in zip(*st_p)]
    ks_, vs_, cbs, ccs, scs = [jnp.stack(z) for z in zip(*st_s)]
    return (yp, ys, kp, vp, cbp, ccp, scp, ks_, vs_, cbs, ccs, scs)
```

```python
import functools
import math

import numpy as np
import jax
import jax.numpy as jnp
from jax import lax
from jax.experimental import pallas as pl
from jax.experimental.pallas import tpu as pltpu

F32 = jnp.float32
BF16 = jnp.bfloat16
HIGHEST = lax.Precision.HIGHEST

D_MODEL = 1024
N_HEADS_A, N_KV_A, HEAD_DIM_A = 8, 2, 64
GROUP_A = N_HEADS_A // N_KV_A
Q_A = N_HEADS_A * HEAD_DIM_A
KV_A = N_KV_A * HEAD_DIM_A
WINDOW = 128
Q_BLOCK = 128
NUM_BUCKETS = 32
MAX_DISTANCE = 128
NEG_INF = -1e30
C_B = 512
CONV_B = 31
N_HEADS_C, HEAD_DIM_C = 4, 128
V_C = N_HEADS_C * HEAD_DIM_C
QKV_C = 3 * V_C
CONV_C = 4
DN_CHUNK = 64
N_BRANCH = 3
IN_SIZES = (Q_A, KV_A, KV_A, 2 * C_B, QKV_C, V_C, N_HEADS_C, N_HEADS_C, N_BRANCH * D_MODEL)
N_EXPERTS = 32
TOP_K = 4
D_FF = 1024
SWIGLU_LIMIT = 7.0
SWIGLU_ALPHA = 1.702
PLE_DIM = 256
LN_EPS = 1e-5
RMS_EPS = 1e-6

LANE = 128
SUBLANE = 8
VMEM_LIMIT = 56 * 1024 * 1024

COL_GATE = 0
COL_GLU = 3 * D_MODEL
COL_QKVC = COL_GLU + 2 * C_B
COL_Z = COL_QKVC + QKV_C
COL_QA = COL_Z + V_C
COL_KA = COL_QA + Q_A
COL_VA = COL_KA + KV_A
COL_SMALL = COL_VA + KV_A
N_IN_PAD = COL_SMALL + LANE
MOE_ROWS = 512
HALO_B = 32
HALO_C = 8


def _tile(n, target, mult=SUBLANE):
    best = None
    for t in range(mult, min(n, target) + 1, mult):
        if n % t == 0:
            best = t
    return best if best is not None else n


def _cparams(sem):
    return pltpu.CompilerParams(dimension_semantics=sem, vmem_limit_bytes=VMEM_LIMIT)


def _layer_norm(x, g, b):
    mu = jnp.mean(x, -1, keepdims=True)
    xc = x - mu
    var = jnp.mean(xc * xc, -1, keepdims=True)
    return xc * lax.rsqrt(var + LN_EPS) * g + b


def _sigmoid(x):
    return 1.0 / (1.0 + jnp.exp(-x))


def _mm_bias_kernel(x_ref, w_ref, b_ref, o_ref):
    o_ref[...] = jnp.dot(x_ref[...], w_ref[...], preferred_element_type=F32) + b_ref[...]


def _in_proj(x_bf, w_bf, b):
    m, k = x_bf.shape
    n = w_bf.shape[1]
    tm = _tile(m, 1536)
    tn = _tile(n, 1408, LANE)
    return pl.pallas_call(
        _mm_bias_kernel,
        grid=(n // tn, m // tm),
        in_specs=[pl.BlockSpec((tm, k), lambda j, i: (i, 0)),
                  pl.BlockSpec((k, tn), lambda j, i: (0, j)),
                  pl.BlockSpec((1, tn), lambda j, i: (0, j))],
        out_specs=pl.BlockSpec((tm, tn), lambda j, i: (i, j)),
        out_shape=jax.ShapeDtypeStruct((m, n), F32),
        compiler_params=_cparams(("arbitrary", "arbitrary")),
        name="in_proj",
    )(x_bf, w_bf, b)


def _attn_prompt_kernel(sink_ref, q_ref, kp_ref, kc_ref, vp_ref, vc_ref, bias_ref, o_ref):
    nt = (((1,), (1,)), ((), ()))
    q = (q_ref[...] * (HEAD_DIM_A ** -0.5)).astype(BF16)
    lo = lax.broadcasted_iota(jnp.int32, (2 * Q_BLOCK, KV_A), 1) < HEAD_DIM_A

    def placed(prev_ref, cur_ref):
        band = jnp.concatenate([prev_ref[...], cur_ref[...]], 0)
        own = (jnp.where(lo, band, 0.0), jnp.where(lo, 0.0, band))
        moved = (pltpu.roll(own[0], HEAD_DIM_A, 1), pltpu.roll(own[1], HEAD_DIM_A, 1))
        return ((own[0].astype(BF16), moved[0].astype(BF16)), (moved[1].astype(BF16), own[1].astype(BF16)))

    kz = placed(kp_ref, kc_ref)
    vz = placed(vp_ref, vc_ref)
    pairs = []
    for kv in range(N_KV_A):
        heads = range(kv * GROUP_A, (kv + 1) * GROUP_A)
        scores = [lax.dot_general(q[:, (h // 2) * LANE:(h // 2 + 1) * LANE], kz[kv][h % 2], nt,
                                  preferred_element_type=F32) + bias_ref[0, h] for h in heads]
        probs = []
        for h, s in zip(heads, scores):
            sink = sink_ref[h]
            m = jnp.maximum(jnp.max(s, -1, keepdims=True), sink)
            e = jnp.exp(s - m)
            inv = 1.0 / (jnp.sum(e, -1, keepdims=True) + jnp.exp(sink - m))
            probs.append((e * inv).astype(BF16))
        pv = [jnp.dot(p, vz[kv][h % 2], preferred_element_type=F32) for h, p in zip(heads, probs)]
        pairs += [pv[0] + pv[1], pv[2] + pv[3]]
    o_ref[...] = jnp.concatenate(pairs, -1).astype(o_ref.dtype)


def _attn_prompt(h, sinks, bias_tab, nbatch, seq):
    nb = seq // Q_BLOCK
    qcol = COL_QA // Q_A
    kcol = COL_KA // KV_A
    vcol = COL_VA // KV_A

    def cur(c):
        return lambda b, i: (b * nb + i, c)

    def prev(c):
        return lambda b, i: (jnp.maximum(b * nb + i - 1, 0), c)

    return pl.pallas_call(
        _attn_prompt_kernel,
        grid=(nbatch, nb),
        in_specs=[pl.BlockSpec(memory_space=pltpu.SMEM),
                  pl.BlockSpec((Q_BLOCK, Q_A), cur(qcol)),
                  pl.BlockSpec((Q_BLOCK, KV_A), prev(kcol)),
                  pl.BlockSpec((Q_BLOCK, KV_A), cur(kcol)),
                  pl.BlockSpec((Q_BLOCK, KV_A), prev(vcol)),
                  pl.BlockSpec((Q_BLOCK, KV_A), cur(vcol)),
                  pl.BlockSpec((1, N_HEADS_A, Q_BLOCK, 2 * Q_BLOCK), lambda b, i: (jnp.minimum(i, 1), 0, 0, 0))],
        out_specs=pl.BlockSpec((Q_BLOCK, Q_A), lambda b, i: (b * nb + i, 0)),
        out_shape=jax.ShapeDtypeStruct((nbatch * seq, Q_A), BF16),
        compiler_params=_cparams(("arbitrary", "arbitrary")),
        name="attn_prompt",
    )(sinks, h, h, h, h, h, bias_tab)


def _attn_sample_kernel(q_ref, k_ref, v_ref, bias_ref, sink_ref, o_ref):
    q = (q_ref[...] * (HEAD_DIM_A ** -0.5)).astype(BF16)
    s = jnp.einsum("bqd,bkd->bqk", q, k_ref[...].astype(BF16), preferred_element_type=F32) + bias_ref[...]
    sink = sink_ref[...]
    m = jnp.maximum(jnp.max(s, -1, keepdims=True), sink)
    e = jnp.exp(s - m)
    pr = e / (jnp.sum(e, -1, keepdims=True) + jnp.exp(sink - m))
    o_ref[...] = jnp.einsum("bqk,bkd->bqd", pr.astype(BF16), v_ref[...].astype(BF16), preferred_element_type=F32)


def _attn_sample(q, k, v, bias, sink):
    g, r, hd = q.shape
    kk = k.shape[1]
    gb = bias.shape[0]
    return pl.pallas_call(
        _attn_sample_kernel,
        grid=(g // gb,),
        in_specs=[pl.BlockSpec((gb, r, hd), lambda i: (i, 0, 0)),
                  pl.BlockSpec((gb, kk, hd), lambda i: (i, 0, 0)),
                  pl.BlockSpec((gb, kk, hd), lambda i: (i, 0, 0)),
                  pl.BlockSpec((gb, r, kk), lambda i: (0, 0, 0)),
                  pl.BlockSpec((gb, r, 1), lambda i: (0, 0, 0))],
        out_specs=pl.BlockSpec((gb, r, hd), lambda i: (i, 0, 0)),
        out_shape=jax.ShapeDtypeStruct((g, r, hd), F32),
        compiler_params=_cparams(("arbitrary",)),
        name="attn_sample",
    )(q, k, v, bias, sink)


CONV_ROWS = 64


def _convb_prompt_kernel(a_ref, b_ref, w_ref, bias_ref, g_ref, beta_ref, cb_ref, tail_ref, ubuf, shbuf, *, tl):
    @pl.when(pl.program_id(1) == 0)
    def _():
        ubuf[0:HALO_B, :] = jnp.zeros((HALO_B, C_B), F32)

    ubuf[HALO_B:HALO_B + tl, :] = a_ref[...] * _sigmoid(b_ref[...])
    off = HALO_B - (CONV_B - 1)
    span = tl + HALO_B - SUBLANE
    for s in range(1, SUBLANE):
        shbuf[s - 1, 0:span, :] = ubuf[s:s + span, :]

    def body(r, carry):
        base = pl.multiple_of(r * CONV_ROWS, CONV_ROWS)
        accs = []
        for c in range(C_B // LANE):
            cs = slice(c * LANE, (c + 1) * LANE)
            acc = jnp.zeros((CONV_ROWS, LANE), F32)
            for w in range(CONV_B):
                d = off + w
                start = pl.multiple_of(base + (d // SUBLANE) * SUBLANE, SUBLANE)
                if d % SUBLANE == 0:
                    win = ubuf[pl.ds(start, CONV_ROWS), cs]
                else:
                    win = shbuf[d % SUBLANE - 1, pl.ds(start, CONV_ROWS), cs]
                acc = acc + win * w_ref[w:w + 1, cs]
            accs.append(acc + bias_ref[:, cs])
        y = _layer_norm(jnp.concatenate(accs, -1), g_ref[...], beta_ref[...])
        cb_ref[pl.ds(base, CONV_ROWS), :] = (y * _sigmoid(y)).astype(cb_ref.dtype)
        return carry

    lax.fori_loop(0, tl // CONV_ROWS, body, 0)
    tail = ubuf[tl:tl + HALO_B, :]
    tail_ref[0] = tail
    ubuf[0:HALO_B, :] = tail


def _convb_prompt(h, w_dw, b_dw, ln_g, ln_b, nbatch, seq):
    tl = _tile(seq, 512, CONV_ROWS)
    nt = seq // tl
    acol = COL_GLU // C_B
    return pl.pallas_call(
        functools.partial(_convb_prompt_kernel, tl=tl),
        grid=(nbatch, nt),
        in_specs=[pl.BlockSpec((tl, C_B), lambda b, t: (b * nt + t, acol)),
                  pl.BlockSpec((tl, C_B), lambda b, t: (b * nt + t, acol + 1)),
                  pl.BlockSpec((HALO_B, C_B), lambda b, t: (0, 0)),
                  pl.BlockSpec((1, C_B), lambda b, t: (0, 0)),
                  pl.BlockSpec((1, C_B), lambda b, t: (0, 0)),
                  pl.BlockSpec((1, C_B), lambda b, t: (0, 0))],
        out_specs=[pl.BlockSpec((tl, C_B), lambda b, t: (b * nt + t, 0)),
                   pl.BlockSpec((1, HALO_B, C_B), lambda b, t: (b, 0, 0))],
        out_shape=[jax.ShapeDtypeStruct((nbatch * seq, C_B), BF16),
                   jax.ShapeDtypeStruct((nbatch, HALO_B, C_B), F32)],
        scratch_shapes=[pltpu.VMEM((HALO_B + tl, C_B), F32),
                        pltpu.VMEM((SUBLANE - 1, HALO_B + tl, C_B), F32)],
        compiler_params=_cparams(("arbitrary", "arbitrary")),
        name="convb_prompt",
    )(h, h, w_dw, b_dw, ln_g, ln_b)


def _convb_sample_kernel(a_ref, b_ref, st_ref, wt_ref, bias_ref, g_ref, beta_ref, cb_ref, nst_ref, ext, *, ls, wrows):
    ns = CONV_B - 1
    ext[:, ns - ns % SUBLANE:wrows, :] = jnp.zeros((ext.shape[0], wrows - (ns - ns % SUBLANE), C_B), F32)
    ext[:, 0:ns, :] = st_ref[...]
    ext[:, ns:ns + ls, :] = a_ref[...] * _sigmoid(b_ref[...])
    nst_ref[...] = ext[:, ls:ls + ns, :]
    win = ext[...]
    for t in range(ls):
        y = jnp.sum(win * wt_ref[t][None], axis=1) + bias_ref[...]
        y = _layer_norm(y, g_ref[...], beta_ref[...])
        cb_ref[t] = (y * _sigmoid(y)).astype(cb_ref.dtype)


def _convb_sample(hs, state, wt, b_dw, ln_g, ln_b):
    s, ls, _ = hs.shape
    wrows = wt.shape[1]
    sb = _tile(s, 16, 1)
    acol = COL_GLU // C_B
    ns = CONV_B - 1
    return pl.pallas_call(
        functools.partial(_convb_sample_kernel, ls=ls, wrows=wrows),
        grid=(s // sb,),
        in_specs=[pl.BlockSpec((sb, ls, C_B), lambda i: (i, 0, acol)),
                  pl.BlockSpec((sb, ls, C_B), lambda i: (i, 0, acol + 1)),
                  pl.BlockSpec((sb, ns, C_B), lambda i: (i, 0, 0)),
                  pl.BlockSpec((ls, wrows, C_B), lambda i: (0, 0, 0)),
                  pl.BlockSpec((1, C_B), lambda i: (0, 0)),
                  pl.BlockSpec((1, C_B), lambda i: (0, 0)),
                  pl.BlockSpec((1, C_B), lambda i: (0, 0))],
        out_specs=[pl.BlockSpec((ls, sb, C_B), lambda i: (0, i, 0)),
                   pl.BlockSpec((sb, ns, C_B), lambda i: (i, 0, 0))],
        out_shape=[jax.ShapeDtypeStruct((ls, s, C_B), BF16),
                   jax.ShapeDtypeStruct((s, ns, C_B), F32)],
        scratch_shapes=[pltpu.VMEM((sb, wrows, C_B), F32)],
        compiler_params=_cparams(("arbitrary",)),
        name="convb_sample",
    )(hs, hs, state, wt, b_dw, ln_g, ln_b)


def _softplus(x):
    return jnp.maximum(x, 0.0) + jnp.log(1.0 + jnp.exp(-jnp.abs(x)))


def _dn_kernel(q_ref, k_ref, v_ref, z_ref, sm_ref, cst_ref, s0_ref, wc_ref, alog_ref, dtb_ref, nw_ref,
               o_ref, sout_ref, cbuf, sbuf, smbuf, *, c, lr):
    n = pl.program_id(1)
    hd = HEAD_DIM_C

    @pl.when(n == 0)
    def _():
        cbuf[0:HALO_C, :] = cst_ref[0]
        sbuf[...] = s0_ref[0]

    def rows(ref):
        x = ref[...]
        return x.reshape(x.shape[-2:])

    cbuf[HALO_C:HALO_C + lr, 0:V_C] = rows(q_ref)
    cbuf[HALO_C:HALO_C + lr, V_C:2 * V_C] = rows(k_ref)
    cbuf[HALO_C:HALO_C + lr, 2 * V_C:3 * V_C] = rows(v_ref)
    if lr < c:
        cbuf[HALO_C + lr:HALO_C + c, :] = jnp.zeros((c - lr, QKV_C), F32)
        smbuf[...] = jnp.zeros((c, LANE), F32)
    smbuf[0:lr, :] = rows(sm_ref)

    off = HALO_C - (CONV_C - 1)
    y = cbuf[off:off + c, :] * wc_ref[0:1, :]
    for w in range(1, CONV_C):
        y = y + cbuf[off + w:off + w + c, :] * wc_ref[w:w + 1, :]
    y = y * _sigmoid(y)
    if lr == c:
        cbuf[0:HALO_C, :] = cbuf[c:c + HALO_C, :]

    small = smbuf[...]
    beta_f = _sigmoid(small)
    g_f = -jnp.exp(alog_ref[...]) * _softplus(small + dtb_ref[...])
    if lr < c:
        live = lax.broadcasted_iota(jnp.int32, (c, LANE), 0) < lr
        beta_f = jnp.where(live, beta_f, 0.0)
        g_f = jnp.where(live, g_f, 0.0)

    ri = lax.broadcasted_iota(jnp.int32, (c, c), 0)
    ci = lax.broadcasted_iota(jnp.int32, (c, c), 1)
    tri = ri >= ci
    stri = ri > ci
    eye = jnp.where(ri == ci, 1.0, 0.0).astype(F32)
    gc_all = jnp.dot(jnp.where(tri, 1.0, 0.0).astype(F32), g_f, precision=HIGHEST, preferred_element_type=F32)
    sel = jnp.where(lax.broadcasted_iota(jnp.int32, (SUBLANE, LANE), 1)
                    == lax.broadcasted_iota(jnp.int32, (SUBLANE, LANE), 0) + N_HEADS_C, 1.0, 0.0).astype(F32)
    nt = (((1,), (1,)), ((), ()))
    gc_rows = lax.dot_general(sel, gc_all, nt, precision=HIGHEST, preferred_element_type=F32)

    for h in range(N_HEADS_C):
        hs = slice(h * hd, (h + 1) * hd)
        qh = y[:, hs]
        kh = y[:, V_C + h * hd:V_C + (h + 1) * hd]
        vh = y[:, 2 * V_C + h * hd:2 * V_C + (h + 1) * hd]
        qn = qh * lax.rsqrt(jnp.sum(qh * qh, -1, keepdims=True) + RMS_EPS) * (hd ** -0.5)
        kn = kh * lax.rsqrt(jnp.sum(kh * kh, -1, keepdims=True) + RMS_EPS)
        beta = beta_f[:, h:h + 1]
        gcol = gc_all[:, N_HEADS_C + h:N_HEADS_C + h + 1]
        grow = gc_rows[h:h + 1, :]
        glast = gcol[c - 1:c, :]
        decay = jnp.where(tri, jnp.exp(jnp.where(tri, gcol - grow, 0.0)), 0.0)
        kb = kn.astype(BF16)
        kk = lax.dot_general(kb, kb, nt, preferred_element_type=F32)
        nmat = jnp.where(stri, -(beta * kk * decay), 0.0)
        inv = eye + nmat
        npow = nmat
        for _ in range(int(math.log2(c)) - 1):
            npow = jnp.dot(npow, npow, precision=HIGHEST, preferred_element_type=F32)
            inv = inv + jnp.dot(inv, npow, precision=HIGHEST, preferred_element_type=F32)
        egc = jnp.exp(gcol)
        rhs = jnp.concatenate([vh * beta, kn * (beta * egc)], -1)
        sol = jnp.dot(inv, rhs, precision=HIGHEST, preferred_element_type=F32)
        u = sol[:, :hd]
        wm = sol[:, hd:]
        qk = jnp.where(tri, lax.dot_general(qn.astype(BF16), kb, nt, preferred_element_type=F32) * decay, 0.0)
        q_dec = qn * egc
        k_dec = kn * jnp.exp(glast - gcol)
        s_h = sbuf[h]
        s_b = s_h.astype(BF16)
        v_new = u - jnp.dot(wm.astype(BF16), s_b, preferred_element_type=F32)
        vb = v_new.astype(BF16)
        o_h = (jnp.dot(q_dec.astype(BF16), s_b, preferred_element_type=F32)
               + jnp.dot(qk.astype(BF16), vb, preferred_element_type=F32))
        sbuf[h] = s_h * jnp.exp(glast) + jnp.dot(k_dec.T.astype(BF16), vb, preferred_element_type=F32)
        on = o_h * lax.rsqrt(jnp.mean(o_h * o_h, -1, keepdims=True) + RMS_EPS) * nw_ref[...]
        zh = rows(z_ref)[:, hs]
        res = (on[0:lr, :] * (zh * _sigmoid(zh))).astype(o_ref.dtype)
        if len(o_ref.shape) == 3:
            o_ref[0, :, hs] = res
        else:
            o_ref[:, hs] = res

    @pl.when(n == pl.num_programs(1) - 1)
    def _():
        sout_ref[0] = sbuf[...]


def _deltanet(h, cstate, s0, wc, alog_row, dtb_row, nw, *, nbatch, nchunk, c, lr, three_d):
    qc = COL_QKVC // V_C
    zc = COL_Z // V_C
    smc = COL_SMALL // LANE
    if three_d:
        def spec(w, col):
            return pl.BlockSpec((1, lr, w), lambda b, n: (b, 0, col))
        o_spec = pl.BlockSpec((1, lr, V_C), lambda b, n: (b, 0, 0))
        o_shape = jax.ShapeDtypeStruct((nbatch, lr, V_C), F32)
    else:
        def spec(w, col):
            return pl.BlockSpec((lr, w), lambda b, n: (b * nchunk + n, col))
        o_spec = pl.BlockSpec((lr, V_C), lambda b, n: (b * nchunk + n, 0))
        o_shape = jax.ShapeDtypeStruct((nbatch * nchunk * lr, V_C), BF16)
    return pl.pallas_call(
        functools.partial(_dn_kernel, c=c, lr=lr),
        grid=(nbatch, nchunk),
        in_specs=[spec(V_C, qc), spec(V_C, qc + 1), spec(V_C, qc + 2), spec(V_C, zc), spec(LANE, smc),
                  pl.BlockSpec((1, HALO_C, QKV_C), lambda b, n: (b, 0, 0)),
                  pl.BlockSpec((1, N_HEADS_C, HEAD_DIM_C, HEAD_DIM_C), lambda b, n: (b, 0, 0, 0)),
                  pl.BlockSpec((SUBLANE, QKV_C), lambda b, n: (0, 0)),
                  pl.BlockSpec((1, LANE), lambda b, n: (0, 0)),
                  pl.BlockSpec((1, LANE), lambda b, n: (0, 0)),
                  pl.BlockSpec((1, HEAD_DIM_C), lambda b, n: (0, 0))],
        out_specs=[o_spec,
                   pl.BlockSpec((1, N_HEADS_C, HEAD_DIM_C, HEAD_DIM_C), lambda b, n: (b, 0, 0, 0))],
        out_shape=[o_shape,
                   jax.ShapeDtypeStruct((nbatch, N_HEADS_C, HEAD_DIM_C, HEAD_DIM_C), F32)],
        scratch_shapes=[pltpu.VMEM((HALO_C + c, QKV_C), F32),
                        pltpu.VMEM((N_HEADS_C, HEAD_DIM_C, HEAD_DIM_C), F32),
                        pltpu.VMEM((c, LANE), F32)],
        compiler_params=_cparams(("arbitrary", "arbitrary")),
        name="deltanet_3d" if three_d else "deltanet_2d",
    )(h, h, h, h, h, cstate, s0, wc, alog_row, dtb_row, nw)


DN_UNIT = DN_CHUNK
DN_STACK = N_HEADS_C * DN_UNIT
DN_PAIR = 2 * DN_UNIT
DN_GL_ROWS = 16


def _dn_prep_kernel(q_ref, k_ref, v_ref, qp_ref, kp_ref, vp_ref, sm_ref, wc_ref, alog_ref, dtb_ref,
                    u_ref, w_ref, qd_ref, kdt_ref, qk_ref, gl_ref, cbuf, *, c, lr, ups):
    rows = ups * DN_UNIT
    halo = jnp.concatenate([qp_ref[...], kp_ref[...], vp_ref[...]], -1)
    cbuf[0:HALO_C, :] = jnp.where(pl.program_id(1) == 0, 0.0, halo)
    cbuf[HALO_C:HALO_C + rows, 0:V_C] = q_ref[...]
    cbuf[HALO_C:HALO_C + rows, V_C:2 * V_C] = k_ref[...]
    cbuf[HALO_C:HALO_C + rows, 2 * V_C:3 * V_C] = v_ref[...]
    off = HALO_C - (CONV_C - 1)
    y = cbuf[off:off + rows, :] * wc_ref[0:1, :]
    for w in range(1, CONV_C):
        y = y + cbuf[off + w:off + w + rows, :] * wc_ref[w:w + 1, :]
    y = y * _sigmoid(y)
    outs = (u_ref, w_ref, qd_ref, kdt_ref, qk_ref, gl_ref)
    _dn_prep_tables(y, sm_ref[...], alog_ref[...], dtb_ref[...], outs, ups=ups, c=c, lr=lr)


def _dn_prep_tables(y, small, alog, dtb, outs, *, ups, c, lr):
    u_ref, w_ref, qd_ref, kdt_ref, qk_ref, gl_ref = outs
    nt = (((1,), (1,)), ((), ()))
    hd = HEAD_DIM_C
    beta_f = _sigmoid(small)
    g_f = -jnp.exp(alog) * _softplus(small + dtb)
    if lr < c:
        live = lax.broadcasted_iota(jnp.int32, beta_f.shape, 0) % c >= c - lr
        beta_f = jnp.where(live, beta_f, 0.0)
        g_f = jnp.where(live, g_f, 0.0)

    r2 = lax.broadcasted_iota(jnp.int32, (2 * DN_UNIT, DN_UNIT), 0)
    c2 = lax.broadcasted_iota(jnp.int32, (2 * DN_UNIT, DN_UNIT), 1)
    rr = jnp.where(r2 >= DN_UNIT, r2 - DN_UNIT, r2)
    same2 = (rr // c) == (c2 // c)
    summer = jnp.where(same2 & ((r2 >= DN_UNIT) | (rr >= c2)), 1.0, 0.0).astype(F32)

    ri = lax.broadcasted_iota(jnp.int32, (DN_PAIR, DN_PAIR), 0)
    ci = lax.broadcasted_iota(jnp.int32, (DN_PAIR, DN_PAIR), 1)
    same = (ri // c) == (ci // c)
    tri = same & (ri >= ci)
    stri = same & (ri > ci)
    eye = jnp.where(ri == ci, 1.0, 0.0).astype(F32)
    items = [(slot, pair) for slot in range(ups) for pair in range(N_HEADS_C // 2)]
    st = []
    for slot, pair in items:
        unit_rows = slice(slot * DN_UNIT, (slot + 1) * DN_UNIT)
        yu = y[unit_rows]
        if pair == 0:
            cg = jnp.dot(summer, g_f[unit_rows], precision=HIGHEST, preferred_element_type=F32)
        heads = (2 * pair, 2 * pair + 1)
        stack = lambda first: jnp.concatenate([yu[:, first + h * hd:first + (h + 1) * hd] for h in heads], 0)
        cols = lambda x, first: jnp.concatenate([x[:, first + h:first + h + 1] for h in heads], 0)
        qs, ks, vs = stack(0), stack(V_C), stack(2 * V_C)
        qn = qs * lax.rsqrt(jnp.sum(qs * qs, -1, keepdims=True) + RMS_EPS) * (hd ** -0.5)
        kn = ks * lax.rsqrt(jnp.sum(ks * ks, -1, keepdims=True) + RMS_EPS)
        beta_s = cols(beta_f[unit_rows], 0)
        gcs = cols(cg[0:DN_UNIT], N_HEADS_C)
        gls = cols(cg[DN_UNIT:2 * DN_UNIT], N_HEADS_C)
        grow = jnp.broadcast_to(gcs, (DN_PAIR, LANE)).T[0:1, :]
        decay = jnp.where(tri, jnp.exp(jnp.where(tri, gcs - grow, 0.0)), 0.0)
        kb = kn.astype(BF16)
        kk = lax.dot_general(kb, kb, nt, preferred_element_type=F32)
        nmat = jnp.where(stri, -(beta_s * kk * decay), 0.0)
        egc = jnp.exp(gcs)
        st.append(dict(qn=qn, kn=kn, kb=kb, decay=decay, nmat=nmat, egc=egc, gcs=gcs, gls=gls,
                       rhs=jnp.concatenate([vs * beta_s, kn * (beta_s * egc)], -1).astype(BF16)))

    invs = [eye + s["nmat"] for s in st]
    nbs = [s["nmat"].astype(BF16) for s in st]
    for _ in range(int(math.log2(c)) - 1):
        nbs = [jnp.dot(nb, nb, preferred_element_type=F32).astype(BF16) for nb in nbs]
        invs = [inv + jnp.dot(inv.astype(BF16), nb, preferred_element_type=F32) for inv, nb in zip(invs, nbs)]
    sols = [jnp.dot(inv.astype(BF16), s["rhs"], preferred_element_type=F32) for inv, s in zip(invs, st)]

    gl_rows = {slot: [] for slot in range(ups)}
    for (slot, pair), s, sol in zip(items, st, sols):
        rows = slice(pair * DN_PAIR, (pair + 1) * DN_PAIR)
        u_ref[slot, rows, :] = sol[:, :hd]
        w_ref[slot, rows, :] = sol[:, hd:].astype(BF16)
        qk = jnp.where(tri, lax.dot_general(s["qn"].astype(BF16), s["kb"], nt, preferred_element_type=F32)
                       * s["decay"], 0.0)
        qk_ref[slot, rows, :] = qk.astype(BF16)
        qd_ref[slot, rows, :] = (s["qn"] * s["egc"]).astype(BF16)
        kdt_ref[slot, :, rows] = (s["kn"] * jnp.exp(s["gls"] - s["gcs"])).T.astype(BF16)
        glb = jnp.broadcast_to(jnp.exp(s["gls"]), (DN_PAIR, LANE))
        gl_rows[slot] += [glb[g * c:g * c + 1, :] for g in range(DN_PAIR // c)]
    groups = DN_STACK // c
    for slot in range(ups):
        if groups < DN_GL_ROWS:
            gl_rows[slot].append(jnp.ones((DN_GL_ROWS - groups, LANE), F32))
        gl_ref[slot] = jnp.concatenate(gl_rows[slot], 0)


def _dn_prep(src, small_src, wc, alog_row, dtb_row, *, qcol, smcol, nbatch, nchunk, c, lr):
    nu = nbatch * nchunk
    total = nu
    ups = 2 if (nchunk % 2 == 0 or (nchunk == 1 and nbatch % 2 == 0)) else 1
    if nchunk == 1:
        nbatch //= ups
    else:
        nchunk //= ups
    nu = total
    rows = ups * DN_UNIT
    per8 = rows // SUBLANE

    def cur(col):
        return pl.BlockSpec((rows, V_C), lambda b, n: (b * nchunk + n, col))

    def prev(col):
        return pl.BlockSpec((HALO_C, V_C), lambda b, n: (jnp.maximum((b * nchunk + n) * per8 - 1, 0), col))

    unit = lambda r, cdim: pl.BlockSpec((ups, r, cdim), lambda b, n: (b * nchunk + n, 0, 0))
    return pl.pallas_call(
        functools.partial(_dn_prep_kernel, c=c, lr=lr, ups=ups),
        grid=(nbatch, nchunk),
        in_specs=[cur(qcol), cur(qcol + 1), cur(qcol + 2), prev(qcol), prev(qcol + 1), prev(qcol + 2),
                  pl.BlockSpec((rows, LANE), lambda b, n: (b * nchunk + n, smcol)),
                  pl.BlockSpec((SUBLANE, QKV_C), lambda b, n: (0, 0)),
                  pl.BlockSpec((1, LANE), lambda b, n: (0, 0)),
                  pl.BlockSpec((1, LANE), lambda b, n: (0, 0))],
        out_specs=[unit(DN_STACK, HEAD_DIM_C), unit(DN_STACK, HEAD_DIM_C), unit(DN_STACK, HEAD_DIM_C),
                   unit(HEAD_DIM_C, DN_STACK), unit(DN_STACK, DN_PAIR), unit(DN_GL_ROWS, LANE)],
        out_shape=[jax.ShapeDtypeStruct((nu, DN_STACK, HEAD_DIM_C), F32),
                   jax.ShapeDtypeStruct((nu, DN_STACK, HEAD_DIM_C), BF16),
                   jax.ShapeDtypeStruct((nu, DN_STACK, HEAD_DIM_C), BF16),
                   jax.ShapeDtypeStruct((nu, HEAD_DIM_C, DN_STACK), BF16),
                   jax.ShapeDtypeStruct((nu, DN_STACK, DN_PAIR), BF16),
                   jax.ShapeDtypeStruct((nu, DN_GL_ROWS, LANE), F32)],
        scratch_shapes=[pltpu.VMEM((HALO_C + rows, QKV_C), F32)],
        compiler_params=_cparams(("arbitrary", "arbitrary")),
        name="deltanet_prep",
    )(src, src, src, src, src, src, small_src, wc, alog_row, dtb_row)


def _dn_seq_kernel(u_ref, w_ref, qd_ref, kdt_ref, qk_ref, gl_ref, *rest, c, nsq, streams, carry):
    z_refs = rest[:streams]
    nw_ref, s0_ref, o_ref, sout_ref, sbuf = rest[streams:]
    groups = DN_STACK // c
    hd = HEAD_DIM_C
    span = groups * hd
    step = pl.program_id(0)

    def load_state():
        for j in range(streams):
            for g in range(groups):
                sbuf[j * span + g * hd:j * span + (g + 1) * hd, :] = s0_ref[j, 0, g % nsq, g // nsq]

    def store_state():
        for j in range(streams):
            for g in range(groups):
                sout_ref[j, 0, g % nsq, g // nsq] = sbuf[j * span + g * hd:j * span + (g + 1) * hd, :]

    if carry:
        pl.when(step == 0)(load_state)
    else:
        load_state()

    col_group = lax.broadcasted_iota(jnp.int32, (hd, DN_STACK), 1) // c
    vbs, o_inter = [], []
    for j in range(streams):
        u = u_ref[j, 0]
        w = w_ref[j, 0]
        qd = qd_ref[j, 0]
        v_parts, o_parts = [], []
        for g in range(groups):
            rows = slice(g * c, (g + 1) * c)
            s_b = sbuf[j * span + g * hd:j * span + (g + 1) * hd, :].astype(BF16)
            r = jnp.dot(jnp.concatenate([w[rows], qd[rows]], 0), s_b, preferred_element_type=F32)
            v_parts.append(u[rows] - r[:c])
            o_parts.append(r[c:])
        vbs.append(jnp.concatenate(v_parts, 0).astype(BF16))
        o_inter.append(jnp.concatenate(o_parts, 0))

    o_all = []
    for j in range(streams):
        qk = qk_ref[j, 0]
        o_all.append(o_inter[j] + jnp.concatenate(
            [jnp.dot(qk[p * DN_PAIR:(p + 1) * DN_PAIR], vbs[j][p * DN_PAIR:(p + 1) * DN_PAIR],
                     preferred_element_type=F32) for p in range(DN_STACK // DN_PAIR)], 0))

    for j in range(streams):
        kdt = kdt_ref[j, 0]
        k_big = jnp.concatenate([jnp.where(col_group == g, kdt, jnp.zeros_like(kdt)) for g in range(groups)], 0)
        gl = gl_ref[j, 0]
        scale = jnp.concatenate([jnp.broadcast_to(gl[g:g + 1, :], (hd, LANE)) for g in range(groups)], 0)
        sbuf[j * span:(j + 1) * span, :] = (sbuf[j * span:(j + 1) * span, :] * scale
                                             + jnp.dot(k_big, vbs[j], preferred_element_type=F32))

    for j in range(streams):
        outs = []
        for h in range(N_HEADS_C):
            oh = o_all[j][h * DN_UNIT:(h + 1) * DN_UNIT]
            on = oh * lax.rsqrt(jnp.mean(oh * oh, -1, keepdims=True) + RMS_EPS) * nw_ref[...]
            zh = z_refs[j][:, h * hd:(h + 1) * hd]
            outs.append(on * (zh * _sigmoid(zh)))
        o_ref[j] = jnp.concatenate(outs, -1).astype(o_ref.dtype)

    if carry:
        pl.when(step == pl.num_programs(0) - 1)(store_state)
    else:
        store_state()


def _dn_seq(prep, z_src, nw, s0, *, zcol, streams, steps, c, nsq, carry, layer=None):
    groups = DN_STACK // c
    prep = [t.reshape((streams, steps) + t.shape[1:]) for t in prep]
    unit = lambda r, cdim: pl.BlockSpec((streams, 1, r, cdim), lambda i: (0, i, 0, 0))
    sdim = (streams, 1, nsq, N_HEADS_C, HEAD_DIM_C, HEAD_DIM_C)
    state = pl.BlockSpec(sdim, (lambda i: (0, 0, 0, 0, 0, 0)) if carry else (lambda i: (0, i, 0, 0, 0, 0)))
    state_in, state_shape = state, s0.shape
    if layer is not None:
        state_in = pl.BlockSpec((None,) + sdim, lambda i: (layer, 0, 0 if carry else i, 0, 0, 0, 0))
        state_shape = s0.shape[1:]
    z_specs = [pl.BlockSpec((DN_UNIT, V_C), functools.partial(lambda i, j: (j * steps + i, zcol), j=j))
               for j in range(streams)]
    o, s_out = pl.pallas_call(
        functools.partial(_dn_seq_kernel, c=c, nsq=nsq, streams=streams, carry=carry),
        grid=(steps,),
        in_specs=[unit(DN_STACK, HEAD_DIM_C), unit(DN_STACK, HEAD_DIM_C), unit(DN_STACK, HEAD_DIM_C),
                  unit(HEAD_DIM_C, DN_STACK), unit(DN_STACK, DN_PAIR), unit(DN_GL_ROWS, LANE)]
                 + z_specs + [pl.BlockSpec((1, HEAD_DIM_C), lambda i: (0, 0)), state_in],
        out_specs=[pl.BlockSpec((streams, DN_UNIT, V_C), lambda i: (0, i, 0)), state],
        out_shape=[jax.ShapeDtypeStruct((streams, steps * DN_UNIT, V_C), BF16),
                   jax.ShapeDtypeStruct(state_shape, F32)],
        scratch_shapes=[pltpu.VMEM((streams * groups * HEAD_DIM_C, HEAD_DIM_C), F32)],
        compiler_params=_cparams(("arbitrary",)),
        name="deltanet_seq",
    )(*prep, *([z_src] * streams), nw, s0)
    return o.reshape(streams * steps * DN_UNIT, V_C), s_out


def _merge_kernel(oap_ref, oas_ref, cbp_ref, cbs_ref, ocp_ref, ocs_ref, ga_ref, gb_ref, gc_ref, x_ref,
                  woa_ref, wob_ref, woc_ref, wout_ref, g1_ref, b1_ref, wr_ref, br_ref,
                  x1_ref, route_ref, gate_ref, cnt_ref, carry, *, alpha, prompt_tiles):
    @pl.when(pl.program_id(0) == 0)
    def _():
        carry[...] = jnp.zeros(carry.shape, F32)

    is_prompt = pl.program_id(0) < prompt_tiles
    pick = lambda p_ref, s_ref: jnp.where(is_prompt, p_ref[...], s_ref[...])
    br_a = jnp.dot(pick(oap_ref, oas_ref), woa_ref[...], preferred_element_type=F32)
    br_b = jnp.dot(pick(cbp_ref, cbs_ref), wob_ref[...], preferred_element_type=F32)
    br_c = jnp.dot(pick(ocp_ref, ocs_ref), woc_ref[...], preferred_element_type=F32)
    mixin = _sigmoid(ga_ref[...]) * br_a + _sigmoid(gb_ref[...]) * br_b + _sigmoid(gc_ref[...]) * br_c
    mix = jnp.dot(mixin.astype(BF16), wout_ref[...], preferred_element_type=F32)
    x1 = _layer_norm(alpha * x_ref[...] + mix, g1_ref[...], b1_ref[...])
    x1_ref[...] = x1
    logits = jnp.dot(x1.astype(BF16), wr_ref[...], preferred_element_type=F32) + br_ref[...]
    lane = lax.broadcasted_iota(jnp.int32, logits.shape, 1)
    logits = jnp.where(lane < N_EXPERTS, logits, -jnp.inf)
    route = jnp.zeros(logits.shape, jnp.int32)
    val_out = jnp.zeros(logits.shape, F32)
    member = jnp.zeros(logits.shape, F32)
    sels = []
    top0 = None
    den = None
    for k in range(TOP_K):
        m = jnp.max(logits, -1, keepdims=True)
        sel = jnp.min(jnp.where(logits == m, lane, LANE), -1, keepdims=True)
        if k == 0:
            top0 = m
            e = jnp.ones_like(m)
            den = e
        else:
            e = jnp.exp(m - top0)
            den = den + e
        hit = lane == sel
        route = jnp.where(lane == k, sel, route)
        val_out = jnp.where(lane == k, e, val_out)
        member = jnp.where(hit, 1.0, member)
        logits = jnp.where(hit, -jnp.inf, logits)
        sels.append(sel)
    gate_ref[...] = val_out / den
    tm = logits.shape[0]
    earlier = lax.broadcasted_iota(jnp.int32, (tm, tm), 0) > lax.broadcasted_iota(jnp.int32, (tm, tm), 1)
    cum = jnp.dot(jnp.where(earlier, 1.0, 0.0).astype(BF16), member.astype(BF16),
                  preferred_element_type=F32) + carry[...]
    for k in range(TOP_K):
        rank = jnp.sum(jnp.where(lane == sels[k], cum, 0.0), -1, keepdims=True)
        route = jnp.where(lane == TOP_K + k, rank.astype(jnp.int32), route)
    route_ref[...] = route
    carry[...] = carry[...] + jnp.sum(member, axis=0, keepdims=True)
    cnt_ref[...] = carry[...]


def _merge(branches, h, x, lw, alpha):
    m = x.shape[0]
    tp = branches[0][0].shape[0]
    tm = _tile(math.gcd(tp, m - tp), 512)
    np_tiles = tp // tm
    row = lambda i: (i, 0)
    full = lambda i: (0, 0)
    gcol = COL_GATE // D_MODEL
    p_spec = pl.BlockSpec((tm, Q_A), lambda i: (jnp.minimum(i, np_tiles - 1), 0))
    s_spec = pl.BlockSpec((tm, Q_A), lambda i: (jnp.maximum(i - np_tiles, 0), 0))
    return pl.pallas_call(
        functools.partial(_merge_kernel, alpha=alpha, prompt_tiles=np_tiles),
        grid=(m // tm,),
        in_specs=[p_spec, s_spec, p_spec, s_spec, p_spec, s_spec,
                  pl.BlockSpec((tm, D_MODEL), lambda i: (i, gcol)),
                  pl.BlockSpec((tm, D_MODEL), lambda i: (i, gcol + 1)),
                  pl.BlockSpec((tm, D_MODEL), lambda i: (i, gcol + 2)),
                  pl.BlockSpec((tm, D_MODEL), row),
                  pl.BlockSpec((Q_A, D_MODEL), full), pl.BlockSpec((C_B, D_MODEL), full),
                  pl.BlockSpec((V_C, D_MODEL), full), pl.BlockSpec((D_MODEL, D_MODEL), full),
                  pl.BlockSpec((1, D_MODEL), full), pl.BlockSpec((1, D_MODEL), full),
                  pl.BlockSpec((D_MODEL, LANE), full), pl.BlockSpec((1, LANE), full)],
        out_specs=[pl.BlockSpec((tm, D_MODEL), row), pl.BlockSpec((tm, LANE), row),
                   pl.BlockSpec((tm, LANE), row), pl.BlockSpec((1, LANE), full)],
        out_shape=[jax.ShapeDtypeStruct((m, D_MODEL), F32), jax.ShapeDtypeStruct((m, LANE), jnp.int32),
                   jax.ShapeDtypeStruct((m, LANE), F32), jax.ShapeDtypeStruct((1, LANE), F32)],
        scratch_shapes=[pltpu.VMEM((1, LANE), F32)],
        compiler_params=_cparams(("arbitrary",)),
        name="merge_router",
    )(*branches[0], *branches[1], *branches[2], h, h, h, x, lw["w_o_a"], lw["w_o_b"], lw["w_o_c"], lw["w_out"],
      lw["ln1_g"], lw["ln1_b"], lw["w_router"], lw["b_router"])


def _row_copy(src, src_row, dst, dst_row, sem):
    return pltpu.make_async_copy(src.at[pl.ds(src_row, 1)], dst.at[pl.ds(dst_row, 1)], sem)


def _dispatch_kernel(dest_ref, last_ref, x_ref, xs_ref, zbuf, sem, *, tt, n_blk):
    @pl.when(pl.program_id(0) == 0)
    def _():
        zbuf[...] = jnp.zeros(zbuf.shape, F32)

        def zero_block(first_row):
            return pltpu.make_async_copy(zbuf, xs_ref.at[pl.ds(pl.multiple_of(first_row, MOE_ROWS), MOE_ROWS)], sem)

        def start_tail(b, carry):
            zero_block(b * MOE_ROWS).start()
            return carry

        def wait_tail(b, carry):
            zero_block(0).wait()
            return carry

        n_active = last_ref[N_EXPERTS]
        for e in range(N_EXPERTS):
            zero_block(last_ref[e]).start()
        lax.fori_loop(n_active, n_blk, start_tail, 0)
        for e in range(N_EXPERTS):
            zero_block(0).wait()
        lax.fori_loop(n_active, n_blk, wait_tail, 0)

    def body(t, carry):
        for k in range(TOP_K):
            _row_copy(x_ref, t, xs_ref, dest_ref[t * TOP_K + k], sem).start(priority=k % 2)
        return carry

    lax.fori_loop(0, tt, body, 0, unroll=4)
    for k in range(TOP_K):
        pltpu.make_async_copy(x_ref, xs_ref.at[pl.ds(0, tt)], sem).wait()


def _dispatch(dest, last_blk, x1, n_rows):
    m = x1.shape[0]
    tt = _tile(m, 512)
    return pl.pallas_call(
        functools.partial(_dispatch_kernel, tt=tt, n_blk=n_rows // MOE_ROWS),
        grid=(m // tt,),
        in_specs=[pl.BlockSpec((tt * TOP_K,), lambda i: (i,), memory_space=pltpu.SMEM),
                  pl.BlockSpec(memory_space=pltpu.SMEM),
                  pl.BlockSpec((tt, D_MODEL), lambda i: (i, 0))],
        out_specs=pl.BlockSpec(memory_space=pl.ANY),
        out_shape=jax.ShapeDtypeStruct((n_rows, D_MODEL), F32),
        scratch_shapes=[pltpu.VMEM((MOE_ROWS, D_MODEL), F32), pltpu.SemaphoreType.DMA(())],
        compiler_params=pltpu.CompilerParams(dimension_semantics=("arbitrary",), vmem_limit_bytes=VMEM_LIMIT,
                                             disable_bounds_checks=True),
        name="moe_dispatch",
    )(dest, last_blk, x1)


def _moe_kernel(be_ref, na_ref, x_ref, wu_ref, bu_ref, wd_ref, bd_ref, o_ref, wu_b, wd_b):
    i = pl.program_id(0)
    active = i < na_ref[0]
    new_expert = jnp.logical_or(i == 0, be_ref[i] != be_ref[jnp.maximum(i - 1, 0)])

    @pl.when(jnp.logical_and(active, new_expert))
    def _():
        wu_b[...] = wu_ref[0, 0].astype(BF16)
        wd_b[...] = wd_ref[0, 0].astype(BF16)

    @pl.when(active)
    def _():
        hu = jnp.dot(x_ref[...].astype(BF16), wu_b[...], preferred_element_type=F32) + bu_ref[0, 0]
        gt = jnp.minimum(hu[:, :D_FF], SWIGLU_LIMIT)
        up = jnp.clip(hu[:, D_FF:], -SWIGLU_LIMIT, SWIGLU_LIMIT)
        act = (up + 1.0) * gt * _sigmoid(SWIGLU_ALPHA * gt)
        o_ref[...] = jnp.dot(act.astype(BF16), wd_b[...], preferred_element_type=F32) + bd_ref[0, 0]

    @pl.when(jnp.logical_not(active))
    def _():
        o_ref[...] = jnp.zeros(o_ref.shape, F32)


def _moe(xs, blk_e, n_active, w_up, b_up, w_down, b_down, layer):
    nblk = xs.shape[0] // MOE_ROWS

    def blk(i, be, na):
        return jnp.minimum(i, jnp.maximum(na[0] - 1, 0))

    grid_spec = pltpu.PrefetchScalarGridSpec(
        num_scalar_prefetch=2,
        grid=(nblk,),
        in_specs=[pl.BlockSpec((MOE_ROWS, D_MODEL), lambda i, be, na: (blk(i, be, na), 0)),
                  pl.BlockSpec((1, 1, D_MODEL, 2 * D_FF), lambda i, be, na: (layer, be[blk(i, be, na)], 0, 0)),
                  pl.BlockSpec((1, 1, 1, 2 * D_FF), lambda i, be, na: (layer, be[blk(i, be, na)], 0, 0)),
                  pl.BlockSpec((1, 1, D_FF, D_MODEL), lambda i, be, na: (layer, be[blk(i, be, na)], 0, 0)),
                  pl.BlockSpec((1, 1, 1, D_MODEL), lambda i, be, na: (layer, be[blk(i, be, na)], 0, 0))],
        out_specs=pl.BlockSpec((MOE_ROWS, D_MODEL), lambda i, be, na: (i, 0)),
        scratch_shapes=[pltpu.VMEM((D_MODEL, 2 * D_FF), BF16), pltpu.VMEM((D_FF, D_MODEL), BF16)],
    )
    return pl.pallas_call(
        _moe_kernel,
        grid_spec=grid_spec,
        out_shape=jax.ShapeDtypeStruct((nblk * MOE_ROWS, D_MODEL), F32),
        compiler_params=_cparams(("arbitrary",)),
        name="moe_experts",
    )(blk_e, n_active, xs, w_up, b_up, w_down, b_down)

def _post_kernel(dest_ref, yb_ref, gate_ref, x1_ref, p_ref, wg_ref, wp_ref, g2_ref, b2_ref, x_ref, xb_ref,
                 ybuf, sem, *, alpha, tm):
    def body(t, carry):
        for k in range(TOP_K):
            _row_copy(yb_ref, dest_ref[t * TOP_K + k], ybuf.at[k], t, sem).start(priority=k % 2)
        return carry

    lax.fori_loop(0, tm, body, 0, unroll=4)
    for k in range(TOP_K):
        pltpu.make_async_copy(yb_ref.at[pl.ds(0, tm)], ybuf.at[k], sem).wait()

    gate = gate_ref[...]
    ffn = gate[:, 0:1] * ybuf[0]
    for k in range(1, TOP_K):
        ffn = ffn + gate[:, k:k + 1] * ybuf[k]
    x2 = _layer_norm(alpha * x1_ref[...] + ffn, g2_ref[...], b2_ref[...])
    gl = jnp.dot(x2.astype(BF16), wg_ref[...], preferred_element_type=F32)
    pe = jnp.dot(p_ref[...].astype(BF16), wp_ref[...], preferred_element_type=F32)
    xo = x2 + _sigmoid(gl) * pe
    x_ref[...] = xo
    xb_ref[...] = xo.astype(BF16)


def _post(dest, yb, gate, x1, p, lw, alpha):
    m = x1.shape[0]
    tm = _tile(m, 512)
    row = lambda i: (i, 0)
    full = lambda i: (0, 0)
    return pl.pallas_call(
        functools.partial(_post_kernel, alpha=alpha, tm=tm),
        grid=(m // tm,),
        in_specs=[pl.BlockSpec((tm * TOP_K,), lambda i: (i,), memory_space=pltpu.SMEM),
                  pl.BlockSpec(memory_space=pl.ANY),
                  pl.BlockSpec((tm, LANE), row),
                  pl.BlockSpec((tm, D_MODEL), row), pl.BlockSpec((tm, PLE_DIM), row),
                  pl.BlockSpec((D_MODEL, D_MODEL), full), pl.BlockSpec((PLE_DIM, D_MODEL), full),
                  pl.BlockSpec((1, D_MODEL), full), pl.BlockSpec((1, D_MODEL), full)],
        out_specs=[pl.BlockSpec((tm, D_MODEL), row), pl.BlockSpec((tm, D_MODEL), row)],
        out_shape=[jax.ShapeDtypeStruct((m, D_MODEL), F32), jax.ShapeDtypeStruct((m, D_MODEL), BF16)],
        scratch_shapes=[pltpu.VMEM((TOP_K, tm, D_MODEL), F32), pltpu.SemaphoreType.DMA(())],
        compiler_params=pltpu.CompilerParams(dimension_semantics=("arbitrary",), vmem_limit_bytes=VMEM_LIMIT,
                                             disable_bounds_checks=True),
        name="combine_ple",
    )(dest, yb, gate, x1, p, lw["w_ple_gate"], lw["w_ple_proj"], lw["ln2_g"], lw["ln2_b"])


def _rel_bucket(n):
    max_exact = NUM_BUCKETS // 2
    nf = jnp.maximum(n, 1).astype(F32)
    large = max_exact + (jnp.log(nf / max_exact) / math.log(MAX_DISTANCE / max_exact)
                         * (NUM_BUCKETS - max_exact)).astype(jnp.int32)
    return jnp.where(n < max_exact, jnp.maximum(n, 0), jnp.minimum(large, NUM_BUCKETS - 1))


def _bias_table(rel_bias, dist, valid):
    b = rel_bias.astype(F32)[_rel_bucket(dist)]
    b = jnp.where(valid[..., None], b, NEG_INF)
    return jnp.moveaxis(b, -1, 0)


def _permute_in(w):
    o = np.concatenate([[0], np.cumsum(IN_SIZES)])
    seg = lambda i: w[..., o[i]:o[i + 1]]
    small = jnp.concatenate([seg(6), seg(7)], -1)
    small = jnp.pad(small, [(0, 0)] * (w.ndim - 1) + [(0, LANE - 2 * N_HEADS_C)])
    return jnp.concatenate([seg(8), seg(3), seg(4), seg(5), seg(0), seg(1), seg(2), small], -1)


def _route(route, counts, n_tok):
    n_blk = -(-n_tok * TOP_K // MOE_ROWS) + N_EXPERTS
    cnt = counts[0, :N_EXPERTS].astype(jnp.int32)
    padded = (cnt + MOE_ROWS - 1) // MOE_ROWS * MOE_ROWS
    pad_end = jnp.cumsum(padded)
    pad_start = pad_end - padded
    idx = route[:, 0:TOP_K]
    rank = route[:, TOP_K:2 * TOP_K]
    start_of = jnp.sum(jnp.where(idx[:, :, None] == jnp.arange(N_EXPERTS)[None, None, :],
                                 pad_start[None, None, :], 0), -1)
    dest = (start_of + rank).reshape(n_tok * TOP_K).astype(jnp.int32)
    blk_first = (jnp.arange(n_blk) * MOE_ROWS)[:, None]
    blk_e = jnp.minimum(jnp.sum((pad_end[None, :] <= blk_first).astype(jnp.int32), -1), N_EXPERTS - 1)
    n_active = (pad_end[-1:] // MOE_ROWS).astype(jnp.int32)
    last_blk = jnp.concatenate([jnp.maximum(pad_end - MOE_ROWS, 0).astype(jnp.int32), n_active])
    return dest, last_blk, blk_e.astype(jnp.int32), n_active, n_blk


def kernel(x_prompt, x_sample, cache_k_a, cache_v_a, state_conv_b, state_conv_c, state_s_c, p_prompt, p_sample,
           rel_bias, w_in, b_in, sink_a, w_o_a, w_dw_b, b_dw_b, ln_b_g, ln_b_b, w_o_b, w_conv_c, a_log_c,
           dt_bias_c, norm_c, w_o_c, w_out, ln1_g, ln1_b, w_router, b_router, w_up, b_up, w_down, b_down,
           ln2_g, ln2_b, w_ple_gate, w_ple_proj):
    depth = w_in.shape[0]
    nb_p, seq, _ = x_prompt.shape
    ns, ls, _ = x_sample.shape
    w_buf = cache_k_a.shape[2]
    alpha = (2 * depth) ** 0.25
    tp = nb_p * seq
    tsm = ns * ls
    n_tok = tp + tsm
    assert seq % DN_CHUNK == 0 and seq % Q_BLOCK == 0 and w_buf == WINDOW and ls <= 16

    w_in_p = _permute_in(w_in).astype(BF16)
    b_in_p = _permute_in(b_in)[:, None, :]
    w_o_a_b, w_o_b_b, w_o_c_b, w_out_b = (t.astype(BF16) for t in (w_o_a, w_o_b, w_o_c, w_out))
    w_ple_gate_b, w_ple_proj_b = w_ple_gate.astype(BF16), w_ple_proj.astype(BF16)
    b_up4, b_down4 = b_up[:, :, None, :], b_down[:, :, None, :]
    w_router_p = jnp.pad(w_router, ((0, 0), (0, 0), (0, LANE - N_EXPERTS))).astype(BF16)
    b_router_p = jnp.pad(b_router, ((0, 0), (0, LANE - N_EXPERTS)))[:, None, :]
    w_dw_p = jnp.pad(w_dw_b, ((0, 0), (0, HALO_B - CONV_B), (0, 0)))
    w_cc_p = jnp.pad(w_conv_c, ((0, 0), (0, SUBLANE - CONV_C), (0, 0)))
    lane_pad = lambda t: jnp.pad(t, ((0, 0), (N_HEADS_C, LANE - 2 * N_HEADS_C)))[:, None, :]
    alog_rows, dtb_rows = lane_pad(a_log_c), lane_pad(dt_bias_c)

    qi = jnp.arange(Q_BLOCK)[:, None]
    kj = jnp.arange(2 * Q_BLOCK)[None, :]
    dist = qi - kj + Q_BLOCK
    valid = (dist >= 0) & (dist < WINDOW)
    tab_rest = _bias_table(rel_bias, dist, valid)
    tab_first = _bias_table(rel_bias, dist, valid & (kj >= Q_BLOCK))
    bias_prompt = jnp.stack([tab_first, tab_rest])

    kwin = 2 * WINDOW
    ti = jnp.arange(ls)[:, None]
    kjs = jnp.arange(kwin)[None, :]
    dist_s = w_buf + ti - kjs
    valid_s = (dist_s >= 0) & (dist_s < WINDOW) & (kjs < w_buf + ls)
    tab_s = _bias_table(rel_bias, dist_s, valid_s)
    tab_s = tab_s.reshape(N_KV_A, GROUP_A * ls, kwin)
    gb = 2 * _tile(ns, 16, 1)
    bias_s = jnp.tile(tab_s, (gb // N_KV_A, 1, 1))

    wrows = -(-(CONV_B - 1 + ls) // SUBLANE) * SUBLANE
    jj = jnp.arange(wrows)[None, :] - jnp.arange(ls)[:, None]
    tap_ok = (jj >= 0) & (jj < CONV_B)

    x_all = jnp.concatenate([x_prompt.reshape(tp, D_MODEL), x_sample.reshape(tsm, D_MODEL)], 0)
    xb_all = x_all.astype(BF16)
    p_all = jnp.concatenate([p_prompt.reshape(depth, tp, PLE_DIM), p_sample.reshape(depth, tsm, PLE_DIM)], 1)
    zero_s = jnp.zeros((nb_p, 1, 1, N_HEADS_C, HEAD_DIM_C, HEAD_DIM_C), F32)
    c_s = 16
    nsq_s = DN_UNIT // c_s
    assert ls + CONV_C - 1 <= c_s and ns % nsq_s == 0

    st_p, st_s = [], []
    for i in range(depth):
        lw = {"w_o_a": w_o_a_b[i], "w_o_b": w_o_b_b[i], "w_o_c": w_o_c_b[i], "w_out": w_out_b[i],
              "ln1_g": ln1_g[i][None], "ln1_b": ln1_b[i][None], "w_router": w_router_p[i],
              "b_router": b_router_p[i], "w_ple_gate": w_ple_gate_b[i], "w_ple_proj": w_ple_proj_b[i],
              "ln2_g": ln2_g[i][None], "ln2_b": ln2_b[i][None]}
        h = _in_proj(xb_all, w_in_p[i], b_in_p[i])
        hs = h[tp:].reshape(ns, ls, N_IN_PAD)

        sinks = sink_a[i].astype(F32)
        oa_p = _attn_prompt(h, sinks, bias_prompt, nb_p, seq)
        q_s = hs[:, :, COL_QA:COL_QA + Q_A].reshape(ns, ls, N_KV_A, GROUP_A, HEAD_DIM_A)
        q_s = q_s.transpose(0, 2, 3, 1, 4).reshape(ns * N_KV_A, GROUP_A * ls, HEAD_DIM_A)
        k_new = hs[:, :, COL_KA:COL_KA + KV_A].reshape(ns, ls, N_KV_A, HEAD_DIM_A)
        v_new = hs[:, :, COL_VA:COL_VA + KV_A].reshape(ns, ls, N_KV_A, HEAD_DIM_A)
        k_all = jnp.concatenate([cache_k_a[i], k_new], 1)
        v_all = jnp.concatenate([cache_v_a[i], v_new], 1)

        def keywin(t):
            t = jnp.pad(t, ((0, 0), (0, kwin - w_buf - ls), (0, 0), (0, 0)))
            return t.transpose(0, 2, 1, 3).reshape(ns * N_KV_A, kwin, HEAD_DIM_A)

        sink_s = jnp.tile(jnp.repeat(sinks.reshape(N_KV_A, GROUP_A), ls, axis=1)[:, :, None], (gb // N_KV_A, 1, 1))
        oa_s = _attn_sample(q_s, keywin(k_all), keywin(v_all), bias_s, sink_s)
        oa_s = oa_s.reshape(ns, N_KV_A, GROUP_A, ls, HEAD_DIM_A).transpose(0, 3, 1, 2, 4).reshape(tsm, Q_A)
        oa = (oa_p, oa_s.astype(BF16))

        cb_p, tail_p = _convb_prompt(h, w_dw_p[i], b_dw_b[i][None], ln_b_g[i][None], ln_b_b[i][None], nb_p, seq)
        wt = jnp.where(tap_ok[:, :, None], w_dw_b[i][jnp.clip(jj, 0, CONV_B - 1)], 0.0)
        cb_s, nconv_b_s = _convb_sample(hs, state_conv_b[i], wt, b_dw_b[i][None], ln_b_g[i][None], ln_b_b[i][None])
        cb = (cb_p, cb_s.transpose(1, 0, 2).reshape(tsm, C_B))

        nw = norm_c[i][None]
        nchunk = seq // DN_CHUNK
        prep_p = _dn_prep(h, h, w_cc_p[i], alog_rows[i], dtb_rows[i], qcol=COL_QKVC // V_C,
                          smcol=COL_SMALL // LANE, nbatch=nb_p, nchunk=nchunk, c=DN_CHUNK, lr=DN_CHUNK)
        oc_p, s_p = _dn_seq(prep_p, h, nw, zero_s, zcol=COL_Z // V_C, streams=nb_p, steps=nchunk,
                            c=DN_CHUNK, nsq=1, carry=True)
        s_p = s_p.reshape(nb_p, N_HEADS_C, HEAD_DIM_C, HEAD_DIM_C)
        lead = c_s - ls
        hist = jnp.concatenate([jnp.zeros((ns, lead - (CONV_C - 1), QKV_C), F32), state_conv_c[i],
                                hs[:, :, COL_QKVC:COL_QKVC + QKV_C]], 1).reshape(ns * c_s, QKV_C)
        pad_s = lambda col, wid: jnp.pad(hs[:, :, col:col + wid], ((0, 0), (lead, 0), (0, 0))).reshape(ns * c_s, wid)
        prep_s = _dn_prep(hist, pad_s(COL_SMALL, LANE), w_cc_p[i], alog_rows[i], dtb_rows[i], qcol=0, smcol=0,
                          nbatch=ns // nsq_s, nchunk=1, c=c_s, lr=ls)
        units_s = ns // nsq_s
        streams_s = 2 if units_s % 2 == 0 else 1
        s0_s = state_s_c.astype(F32).reshape(depth, streams_s, units_s // streams_s, nsq_s, N_HEADS_C, HEAD_DIM_C,
                                             HEAD_DIM_C)
        oc_s, s_s = _dn_seq(prep_s, pad_s(COL_Z, V_C), nw, s0_s, zcol=0, streams=streams_s,
                            steps=units_s // streams_s, c=c_s, nsq=nsq_s, carry=False, layer=i)
        s_s = s_s.reshape(ns, N_HEADS_C, HEAD_DIM_C, HEAD_DIM_C)
        oc = (oc_p, oc_s.reshape(ns, c_s, V_C)[:, lead:].reshape(tsm, V_C))

        x1, route, gate, counts = _merge((oa, cb, oc), h, x_all, lw, alpha)

        dest, last_blk, blk_e, n_active, n_blk = _route(route, counts, n_tok)
        xs = _dispatch(dest, last_blk, x1, n_blk * MOE_ROWS)
        yb = _moe(xs, blk_e, n_active, w_up, b_up4, w_down, b_down4, i)

        x_all, xb_all = _post(dest, yb, gate, x1, p_all[i], lw, alpha)

        def tail_rows(nrows, col, width):
            return jnp.stack([h[(b + 1) * seq - nrows:(b + 1) * seq, col:col + width] for b in range(nb_p)])

        kp_new = tail_rows(w_buf, COL_KA, KV_A)
        vp_new = tail_rows(w_buf, COL_VA, KV_A)
        ccp_new = tail_rows(CONV_C - 1, COL_QKVC, QKV_C)
        st_p.append((kp_new.reshape(nb_p, w_buf, N_KV_A, HEAD_DIM_A), vp_new.reshape(nb_p, w_buf, N_KV_A, HEAD_DIM_A),
                     tail_p[:, HALO_B - (CONV_B - 1):], ccp_new, s_p))
        ccs_ext = jnp.concatenate([state_conv_c[i], hs[:, :, COL_QKVC:COL_QKVC + QKV_C]], 1)
        st_s.append((k_all[:, ls:], v_all[:, ls:], nconv_b_s, ccs_ext[:, -(CONV_C - 1):], s_s))

    yp = x_all[:tp].reshape(nb_p, seq, D_MODEL)
    ys = x_all[tp:].reshape(ns, ls, D_MODEL)
    kp, vp, cbp, ccp, scp = [jnp.stack(z) for z in zip(*st_p)]
    ks_, vs_, cbs, ccs, scs = [jnp.stack(z) for z in zip(*st_s)]
    return (yp, ys, kp, vp, cbp, ccp, scp, ks_, vs_, cbs, ccs, scs)
```

```python
import functools
import math

import numpy as np
import jax
import jax.numpy as jnp
from jax import lax
from jax.experimental import pallas as pl
from jax.experimental.pallas import tpu as pltpu

F32 = jnp.float32
BF16 = jnp.bfloat16
HIGHEST = lax.Precision.HIGHEST

D_MODEL = 1024
N_HEADS_A, N_KV_A, HEAD_DIM_A = 8, 2, 64
GROUP_A = N_HEADS_A // N_KV_A
Q_A = N_HEADS_A * HEAD_DIM_A
KV_A = N_KV_A * HEAD_DIM_A
WINDOW = 128
Q_BLOCK = 128
NUM_BUCKETS = 32
MAX_DISTANCE = 128
NEG_INF = -1e30
C_B = 512
CONV_B = 31
N_HEADS_C, HEAD_DIM_C = 4, 128
V_C = N_HEADS_C * HEAD_DIM_C
QKV_C = 3 * V_C
CONV_C = 4
DN_CHUNK = 64
N_BRANCH = 3
IN_SIZES = (Q_A, KV_A, KV_A, 2 * C_B, QKV_C, V_C, N_HEADS_C, N_HEADS_C, N_BRANCH * D_MODEL)
N_EXPERTS = 32
TOP_K = 4
D_FF = 1024
SWIGLU_LIMIT = 7.0
SWIGLU_ALPHA = 1.702
PLE_DIM = 256
LN_EPS = 1e-5
RMS_EPS = 1e-6

LANE = 128
SUBLANE = 8
VMEM_LIMIT = 56 * 1024 * 1024

COL_GATE = 0
COL_GLU = 3 * D_MODEL
COL_QKVC = COL_GLU + 2 * C_B
COL_Z = COL_QKVC + QKV_C
COL_QA = COL_Z + V_C
COL_KA = COL_QA + Q_A
COL_VA = COL_KA + KV_A
COL_SMALL = COL_VA + KV_A
N_IN_PAD = COL_SMALL + LANE
MOE_ROWS = 512
MOE_SUB = 128
HALO_B = 32
HALO_C = 8


def _tile(n, target, mult=SUBLANE):
    best = None
    for t in range(mult, min(n, target) + 1, mult):
        if n % t == 0:
            best = t
    return best if best is not None else n


def _cparams(sem):
    return pltpu.CompilerParams(dimension_semantics=sem, vmem_limit_bytes=VMEM_LIMIT)


def _layer_norm(x, g, b):
    mu = jnp.mean(x, -1, keepdims=True)
    xc = x - mu
    var = jnp.mean(xc * xc, -1, keepdims=True)
    return xc * lax.rsqrt(var + LN_EPS) * g + b


def _sigmoid(x):
    return 1.0 / (1.0 + jnp.exp(-x))


def _mm_bias_kernel(x_ref, w_ref, b_ref, o_ref):
    o_ref[...] = jnp.dot(x_ref[...], w_ref[...], preferred_element_type=F32) + b_ref[...]


def _in_proj(x_bf, w_bf, b):
    m, k = x_bf.shape
    n = w_bf.shape[1]
    tm = _tile(m, 1536)
    tn = _tile(n, 1408, LANE)
    return pl.pallas_call(
        _mm_bias_kernel,
        grid=(n // tn, m // tm),
        in_specs=[pl.BlockSpec((tm, k), lambda j, i: (i, 0)),
                  pl.BlockSpec((k, tn), lambda j, i: (0, j)),
                  pl.BlockSpec((1, tn), lambda j, i: (0, j))],
        out_specs=pl.BlockSpec((tm, tn), lambda j, i: (i, j)),
        out_shape=jax.ShapeDtypeStruct((m, n), F32),
        compiler_params=_cparams(("arbitrary", "arbitrary")),
        name="in_proj",
    )(x_bf, w_bf, b)


def _attn_prompt_kernel(sink_ref, q_ref, kp_ref, kc_ref, vp_ref, vc_ref, bias_ref, o_ref):
    nt = (((1,), (1,)), ((), ()))
    q = (q_ref[...] * (HEAD_DIM_A ** -0.5)).astype(BF16)
    lo = lax.broadcasted_iota(jnp.int32, (2 * Q_BLOCK, KV_A), 1) < HEAD_DIM_A

    def placed(prev_ref, cur_ref):
        band = jnp.concatenate([prev_ref[...], cur_ref[...]], 0)
        own = (jnp.where(lo, band, 0.0), jnp.where(lo, 0.0, band))
        moved = (pltpu.roll(own[0], HEAD_DIM_A, 1), pltpu.roll(own[1], HEAD_DIM_A, 1))
        return ((own[0].astype(BF16), moved[0].astype(BF16)), (moved[1].astype(BF16), own[1].astype(BF16)))

    kz = placed(kp_ref, kc_ref)
    vz = placed(vp_ref, vc_ref)
    pairs = []
    for kv in range(N_KV_A):
        heads = range(kv * GROUP_A, (kv + 1) * GROUP_A)
        scores = [lax.dot_general(q[:, (h // 2) * LANE:(h // 2 + 1) * LANE], kz[kv][h % 2], nt,
                                  preferred_element_type=F32) + bias_ref[0, h] for h in heads]
        probs = []
        for h, s in zip(heads, scores):
            sink = sink_ref[h]
            m = jnp.maximum(jnp.max(s, -1, keepdims=True), sink)
            e = jnp.exp(s - m)
            inv = 1.0 / (jnp.sum(e, -1, keepdims=True) + jnp.exp(sink - m))
            probs.append((e * inv).astype(BF16))
        pv = [jnp.dot(p, vz[kv][h % 2], preferred_element_type=F32) for h, p in zip(heads, probs)]
        pairs += [pv[0] + pv[1], pv[2] + pv[3]]
    o_ref[...] = jnp.concatenate(pairs, -1).astype(o_ref.dtype)


def _attn_prompt(h, sinks, bias_tab, nbatch, seq):
    nb = seq // Q_BLOCK
    qcol = COL_QA // Q_A
    kcol = COL_KA // KV_A
    vcol = COL_VA // KV_A

    def cur(c):
        return lambda b, i: (b * nb + i, c)

    def prev(c):
        return lambda b, i: (jnp.maximum(b * nb + i - 1, 0), c)

    return pl.pallas_call(
        _attn_prompt_kernel,
        grid=(nbatch, nb),
        in_specs=[pl.BlockSpec(memory_space=pltpu.SMEM),
                  pl.BlockSpec((Q_BLOCK, Q_A), cur(qcol)),
                  pl.BlockSpec((Q_BLOCK, KV_A), prev(kcol)),
                  pl.BlockSpec((Q_BLOCK, KV_A), cur(kcol)),
                  pl.BlockSpec((Q_BLOCK, KV_A), prev(vcol)),
                  pl.BlockSpec((Q_BLOCK, KV_A), cur(vcol)),
                  pl.BlockSpec((1, N_HEADS_A, Q_BLOCK, 2 * Q_BLOCK), lambda b, i: (jnp.minimum(i, 1), 0, 0, 0))],
        out_specs=pl.BlockSpec((Q_BLOCK, Q_A), lambda b, i: (b * nb + i, 0)),
        out_shape=jax.ShapeDtypeStruct((nbatch * seq, Q_A), BF16),
        compiler_params=_cparams(("arbitrary", "arbitrary")),
        name="attn_prompt",
    )(sinks, h, h, h, h, h, bias_tab)


def _attn_sample_kernel(q_ref, k_ref, v_ref, bias_ref, sink_ref, o_ref):
    q = (q_ref[...] * (HEAD_DIM_A ** -0.5)).astype(BF16)
    s = jnp.einsum("bqd,bkd->bqk", q, k_ref[...].astype(BF16), preferred_element_type=F32) + bias_ref[...]
    sink = sink_ref[...]
    m = jnp.maximum(jnp.max(s, -1, keepdims=True), sink)
    e = jnp.exp(s - m)
    pr = e / (jnp.sum(e, -1, keepdims=True) + jnp.exp(sink - m))
    o_ref[...] = jnp.einsum("bqk,bkd->bqd", pr.astype(BF16), v_ref[...].astype(BF16), preferred_element_type=F32)


def _attn_sample(q, k, v, bias, sink):
    g, r, hd = q.shape
    kk = k.shape[1]
    gb = bias.shape[0]
    return pl.pallas_call(
        _attn_sample_kernel,
        grid=(g // gb,),
        in_specs=[pl.BlockSpec((gb, r, hd), lambda i: (i, 0, 0)),
                  pl.BlockSpec((gb, kk, hd), lambda i: (i, 0, 0)),
                  pl.BlockSpec((gb, kk, hd), lambda i: (i, 0, 0)),
                  pl.BlockSpec((gb, r, kk), lambda i: (0, 0, 0)),
                  pl.BlockSpec((gb, r, 1), lambda i: (0, 0, 0))],
        out_specs=pl.BlockSpec((gb, r, hd), lambda i: (i, 0, 0)),
        out_shape=jax.ShapeDtypeStruct((g, r, hd), F32),
        compiler_params=_cparams(("arbitrary",)),
        name="attn_sample",
    )(q, k, v, bias, sink)


CONV_ROWS = 64


def _convb_prompt_kernel(a_ref, b_ref, w_ref, bias_ref, g_ref, beta_ref, cb_ref, tail_ref, ubuf, shbuf, *, tl):
    @pl.when(pl.program_id(1) == 0)
    def _():
        ubuf[0:HALO_B, :] = jnp.zeros((HALO_B, C_B), F32)

    ubuf[HALO_B:HALO_B + tl, :] = a_ref[...] * _sigmoid(b_ref[...])
    off = HALO_B - (CONV_B - 1)
    span = tl + HALO_B - SUBLANE
    for s in range(1, SUBLANE):
        shbuf[s - 1, 0:span, :] = ubuf[s:s + span, :]

    def body(r, carry):
        base = pl.multiple_of(r * CONV_ROWS, CONV_ROWS)
        accs = []
        for c in range(C_B // LANE):
            cs = slice(c * LANE, (c + 1) * LANE)
            acc = jnp.zeros((CONV_ROWS, LANE), F32)
            for w in range(CONV_B):
                d = off + w
                start = pl.multiple_of(base + (d // SUBLANE) * SUBLANE, SUBLANE)
                if d % SUBLANE == 0:
                    win = ubuf[pl.ds(start, CONV_ROWS), cs]
                else:
                    win = shbuf[d % SUBLANE - 1, pl.ds(start, CONV_ROWS), cs]
                acc = acc + win * w_ref[w:w + 1, cs]
            accs.append(acc + bias_ref[:, cs])
        y = _layer_norm(jnp.concatenate(accs, -1), g_ref[...], beta_ref[...])
        cb_ref[pl.ds(base, CONV_ROWS), :] = (y * _sigmoid(y)).astype(cb_ref.dtype)
        return carry

    lax.fori_loop(0, tl // CONV_ROWS, body, 0)
    tail = ubuf[tl:tl + HALO_B, :]
    tail_ref[0] = tail
    ubuf[0:HALO_B, :] = tail


def _convb_prompt(h, w_dw, b_dw, ln_g, ln_b, nbatch, seq):
    tl = _tile(seq, 512, CONV_ROWS)
    nt = seq // tl
    acol = COL_GLU // C_B
    return pl.pallas_call(
        functools.partial(_convb_prompt_kernel, tl=tl),
        grid=(nbatch, nt),
        in_specs=[pl.BlockSpec((tl, C_B), lambda b, t: (b * nt + t, acol)),
                  pl.BlockSpec((tl, C_B), lambda b, t: (b * nt + t, acol + 1)),
                  pl.BlockSpec((HALO_B, C_B), lambda b, t: (0, 0)),
                  pl.BlockSpec((1, C_B), lambda b, t: (0, 0)),
                  pl.BlockSpec((1, C_B), lambda b, t: (0, 0)),
                  pl.BlockSpec((1, C_B), lambda b, t: (0, 0))],
        out_specs=[pl.BlockSpec((tl, C_B), lambda b, t: (b * nt + t, 0)),
                   pl.BlockSpec((1, HALO_B, C_B), lambda b, t: (b, 0, 0))],
        out_shape=[jax.ShapeDtypeStruct((nbatch * seq, C_B), BF16),
                   jax.ShapeDtypeStruct((nbatch, HALO_B, C_B), F32)],
        scratch_shapes=[pltpu.VMEM((HALO_B + tl, C_B), F32),
                        pltpu.VMEM((SUBLANE - 1, HALO_B + tl, C_B), F32)],
        compiler_params=_cparams(("arbitrary", "arbitrary")),
        name="convb_prompt",
    )(h, h, w_dw, b_dw, ln_g, ln_b)


def _convb_sample_kernel(a_ref, b_ref, st_ref, wt_ref, bias_ref, g_ref, beta_ref, cb_ref, nst_ref, ext, *, ls, wrows):
    ns = CONV_B - 1
    ext[:, ns - ns % SUBLANE:wrows, :] = jnp.zeros((ext.shape[0], wrows - (ns - ns % SUBLANE), C_B), F32)
    ext[:, 0:ns, :] = st_ref[...]
    ext[:, ns:ns + ls, :] = a_ref[...] * _sigmoid(b_ref[...])
    nst_ref[...] = ext[:, ls:ls + ns, :]
    win = ext[...]
    for t in range(ls):
        y = jnp.sum(win * wt_ref[t][None], axis=1) + bias_ref[...]
        y = _layer_norm(y, g_ref[...], beta_ref[...])
        cb_ref[t] = (y * _sigmoid(y)).astype(cb_ref.dtype)


def _convb_sample(hs, state, wt, b_dw, ln_g, ln_b):
    s, ls, _ = hs.shape
    wrows = wt.shape[1]
    sb = _tile(s, 16, 1)
    acol = COL_GLU // C_B
    ns = CONV_B - 1
    return pl.pallas_call(
        functools.partial(_convb_sample_kernel, ls=ls, wrows=wrows),
        grid=(s // sb,),
        in_specs=[pl.BlockSpec((sb, ls, C_B), lambda i: (i, 0, acol)),
                  pl.BlockSpec((sb, ls, C_B), lambda i: (i, 0, acol + 1)),
                  pl.BlockSpec((sb, ns, C_B), lambda i: (i, 0, 0)),
                  pl.BlockSpec((ls, wrows, C_B), lambda i: (0, 0, 0)),
                  pl.BlockSpec((1, C_B), lambda i: (0, 0)),
                  pl.BlockSpec((1, C_B), lambda i: (0, 0)),
                  pl.BlockSpec((1, C_B), lambda i: (0, 0))],
        out_specs=[pl.BlockSpec((ls, sb, C_B), lambda i: (0, i, 0)),
                   pl.BlockSpec((sb, ns, C_B), lambda i: (i, 0, 0))],
        out_shape=[jax.ShapeDtypeStruct((ls, s, C_B), BF16),
                   jax.ShapeDtypeStruct((s, ns, C_B), F32)],
        scratch_shapes=[pltpu.VMEM((sb, wrows, C_B), F32)],
        compiler_params=_cparams(("arbitrary",)),
        name="convb_sample",
    )(hs, hs, state, wt, b_dw, ln_g, ln_b)


def _softplus(x):
    return jnp.maximum(x, 0.0) + jnp.log(1.0 + jnp.exp(-jnp.abs(x)))


DN_UNIT = DN_CHUNK
DN_STACK = N_HEADS_C * DN_UNIT
DN_PAIR = 2 * DN_UNIT
DN_GL_ROWS = 16


def _dn_prep_kernel(q_ref, k_ref, v_ref, qp_ref, kp_ref, vp_ref, sm_ref, wc_ref, alog_ref, dtb_ref,
                    u_ref, w_ref, qd_ref, kdt_ref, qk_ref, gl_ref, cbuf, *, c, lr, ups):
    rows = ups * DN_UNIT
    halo = jnp.concatenate([qp_ref[...], kp_ref[...], vp_ref[...]], -1)
    cbuf[0:HALO_C, :] = jnp.where(pl.program_id(1) == 0, 0.0, halo)
    cbuf[HALO_C:HALO_C + rows, 0:V_C] = q_ref[...]
    cbuf[HALO_C:HALO_C + rows, V_C:2 * V_C] = k_ref[...]
    cbuf[HALO_C:HALO_C + rows, 2 * V_C:3 * V_C] = v_ref[...]
    off = HALO_C - (CONV_C - 1)
    y = cbuf[off:off + rows, :] * wc_ref[0:1, :]
    for w in range(1, CONV_C):
        y = y + cbuf[off + w:off + w + rows, :] * wc_ref[w:w + 1, :]
    y = y * _sigmoid(y)
    outs = (u_ref, w_ref, qd_ref, kdt_ref, qk_ref, gl_ref)
    _dn_prep_tables(y, sm_ref[...], alog_ref[...], dtb_ref[...], outs, ups=ups, c=c, lr=lr)


def _dn_prep_tables(y, small, alog, dtb, outs, *, ups, c, lr):
    u_ref, w_ref, qd_ref, kdt_ref, qk_ref, gl_ref = outs
    nt = (((1,), (1,)), ((), ()))
    hd = HEAD_DIM_C
    beta_f = _sigmoid(small)
    g_f = -jnp.exp(alog) * _softplus(small + dtb)
    if lr < c:
        live = lax.broadcasted_iota(jnp.int32, beta_f.shape, 0) % c >= c - lr
        beta_f = jnp.where(live, beta_f, 0.0)
        g_f = jnp.where(live, g_f, 0.0)

    r2 = lax.broadcasted_iota(jnp.int32, (2 * DN_UNIT, DN_UNIT), 0)
    c2 = lax.broadcasted_iota(jnp.int32, (2 * DN_UNIT, DN_UNIT), 1)
    rr = jnp.where(r2 >= DN_UNIT, r2 - DN_UNIT, r2)
    same2 = (rr // c) == (c2 // c)
    summer = jnp.where(same2 & ((r2 >= DN_UNIT) | (rr >= c2)), 1.0, 0.0).astype(F32)

    ri = lax.broadcasted_iota(jnp.int32, (DN_PAIR, DN_PAIR), 0)
    ci = lax.broadcasted_iota(jnp.int32, (DN_PAIR, DN_PAIR), 1)
    same = (ri // c) == (ci // c)
    tri = same & (ri >= ci)
    stri = same & (ri > ci)
    eye = jnp.where(ri == ci, 1.0, 0.0).astype(F32)
    items = [(slot, pair) for slot in range(ups) for pair in range(N_HEADS_C // 2)]
    st = []
    for slot, pair in items:
        unit_rows = slice(slot * DN_UNIT, (slot + 1) * DN_UNIT)
        yu = y[unit_rows]
        if pair == 0:
            cg = jnp.dot(summer, g_f[unit_rows], precision=HIGHEST, preferred_element_type=F32)
        heads = (2 * pair, 2 * pair + 1)
        stack = lambda first: jnp.concatenate([yu[:, first + h * hd:first + (h + 1) * hd] for h in heads], 0)
        cols = lambda x, first: jnp.concatenate([x[:, first + h:first + h + 1] for h in heads], 0)
        qs, ks, vs = stack(0), stack(V_C), stack(2 * V_C)
        qn = qs * lax.rsqrt(jnp.sum(qs * qs, -1, keepdims=True) + RMS_EPS) * (hd ** -0.5)
        kn = ks * lax.rsqrt(jnp.sum(ks * ks, -1, keepdims=True) + RMS_EPS)
        beta_s = cols(beta_f[unit_rows], 0)
        gcs = cols(cg[0:DN_UNIT], N_HEADS_C)
        gls = cols(cg[DN_UNIT:2 * DN_UNIT], N_HEADS_C)
        grow = jnp.broadcast_to(gcs, (DN_PAIR, LANE)).T[0:1, :]
        decay = jnp.where(tri, jnp.exp(jnp.where(tri, gcs - grow, 0.0)), 0.0)
        kb = kn.astype(BF16)
        kk = lax.dot_general(kb, kb, nt, preferred_element_type=F32)
        nmat = jnp.where(stri, -(beta_s * kk * decay), 0.0)
        egc = jnp.exp(gcs)
        st.append(dict(qn=qn, kn=kn, kb=kb, decay=decay, nmat=nmat, egc=egc, gcs=gcs, gls=gls,
                       rhs=jnp.concatenate([vs * beta_s, kn * (beta_s * egc)], -1).astype(BF16)))

    invs = [eye + s["nmat"] for s in st]
    nbs = [s["nmat"].astype(BF16) for s in st]
    for _ in range(int(math.log2(c)) - 1):
        nbs = [jnp.dot(nb, nb, preferred_element_type=F32).astype(BF16) for nb in nbs]
        invs = [inv + jnp.dot(inv.astype(BF16), nb, preferred_element_type=F32) for inv, nb in zip(invs, nbs)]
    sols = [jnp.dot(inv.astype(BF16), s["rhs"], preferred_element_type=F32) for inv, s in zip(invs, st)]

    gl_rows = {slot: [] for slot in range(ups)}
    for (slot, pair), s, sol in zip(items, st, sols):
        rows = slice(pair * DN_PAIR, (pair + 1) * DN_PAIR)
        u_ref[slot, rows, :] = sol[:, :hd]
        w_ref[slot, rows, :] = sol[:, hd:].astype(BF16)
        qk = jnp.where(tri, lax.dot_general(s["qn"].astype(BF16), s["kb"], nt, preferred_element_type=F32)
                       * s["decay"], 0.0)
        qk_ref[slot, rows, :] = qk.astype(BF16)
        qd_ref[slot, rows, :] = (s["qn"] * s["egc"]).astype(BF16)
        kdt_ref[slot, :, rows] = (s["kn"] * jnp.exp(s["gls"] - s["gcs"])).T.astype(BF16)
        glb = jnp.broadcast_to(jnp.exp(s["gls"]), (DN_PAIR, LANE))
        gl_rows[slot] += [glb[g * c:g * c + 1, :] for g in range(DN_PAIR // c)]
    groups = DN_STACK // c
    for slot in range(ups):
        if groups < DN_GL_ROWS:
            gl_rows[slot].append(jnp.ones((DN_GL_ROWS - groups, LANE), F32))
        gl_ref[slot] = jnp.concatenate(gl_rows[slot], 0)


def _dn_prep(src, small_src, wc, alog_row, dtb_row, *, qcol, smcol, nbatch, nchunk, c, lr):
    nu = nbatch * nchunk
    ups = 2 if (nchunk % 2 == 0 or (nchunk == 1 and nbatch % 2 == 0)) else 1
    if nchunk == 1:
        nbatch //= ups
    else:
        nchunk //= ups
    rows = ups * DN_UNIT
    per8 = rows // SUBLANE

    def cur(col):
        return pl.BlockSpec((rows, V_C), lambda b, n: (b * nchunk + n, col))

    def prev(col):
        return pl.BlockSpec((HALO_C, V_C), lambda b, n: (jnp.maximum((b * nchunk + n) * per8 - 1, 0), col))

    unit = lambda r, cdim: pl.BlockSpec((ups, r, cdim), lambda b, n: (b * nchunk + n, 0, 0))
    return pl.pallas_call(
        functools.partial(_dn_prep_kernel, c=c, lr=lr, ups=ups),
        grid=(nbatch, nchunk),
        in_specs=[cur(qcol), cur(qcol + 1), cur(qcol + 2), prev(qcol), prev(qcol + 1), prev(qcol + 2),
                  pl.BlockSpec((rows, LANE), lambda b, n: (b * nchunk + n, smcol)),
                  pl.BlockSpec((SUBLANE, QKV_C), lambda b, n: (0, 0)),
                  pl.BlockSpec((1, LANE), lambda b, n: (0, 0)),
                  pl.BlockSpec((1, LANE), lambda b, n: (0, 0))],
        out_specs=[unit(DN_STACK, HEAD_DIM_C), unit(DN_STACK, HEAD_DIM_C), unit(DN_STACK, HEAD_DIM_C),
                   unit(HEAD_DIM_C, DN_STACK), unit(DN_STACK, DN_PAIR), unit(DN_GL_ROWS, LANE)],
        out_shape=[jax.ShapeDtypeStruct((nu, DN_STACK, HEAD_DIM_C), F32),
                   jax.ShapeDtypeStruct((nu, DN_STACK, HEAD_DIM_C), BF16),
                   jax.ShapeDtypeStruct((nu, DN_STACK, HEAD_DIM_C), BF16),
                   jax.ShapeDtypeStruct((nu, HEAD_DIM_C, DN_STACK), BF16),
                   jax.ShapeDtypeStruct((nu, DN_STACK, DN_PAIR), BF16),
                   jax.ShapeDtypeStruct((nu, DN_GL_ROWS, LANE), F32)],
        scratch_shapes=[pltpu.VMEM((HALO_C + rows, QKV_C), F32)],
        compiler_params=_cparams(("arbitrary", "arbitrary")),
        name="deltanet_prep",
    )(src, src, src, src, src, src, small_src, wc, alog_row, dtb_row)


def _dn_seq_kernel(u_ref, w_ref, qd_ref, kdt_ref, qk_ref, gl_ref, *rest, c, nsq, streams, carry):
    z_refs = rest[:streams]
    nw_ref, s0_ref, o_ref, sout_ref, sbuf = rest[streams:]
    groups = DN_STACK // c
    hd = HEAD_DIM_C
    span = groups * hd
    step = pl.program_id(0)

    def load_state():
        for j in range(streams):
            for g in range(groups):
                sbuf[j * span + g * hd:j * span + (g + 1) * hd, :] = s0_ref[j, 0, g % nsq, g // nsq]

    def store_state():
        for j in range(streams):
            for g in range(groups):
                sout_ref[j, 0, g % nsq, g // nsq] = sbuf[j * span + g * hd:j * span + (g + 1) * hd, :]

    if carry:
        pl.when(step == 0)(load_state)
    else:
        load_state()

    col_group = lax.broadcasted_iota(jnp.int32, (hd, DN_STACK), 1) // c
    vbs, o_inter = [], []
    for j in range(streams):
        u = u_ref[j, 0]
        w = w_ref[j, 0]
        qd = qd_ref[j, 0]
        v_parts, o_parts = [], []
        for g in range(groups):
            rows = slice(g * c, (g + 1) * c)
            s_b = sbuf[j * span + g * hd:j * span + (g + 1) * hd, :].astype(BF16)
            r = jnp.dot(jnp.concatenate([w[rows], qd[rows]], 0), s_b, preferred_element_type=F32)
            v_parts.append(u[rows] - r[:c])
            o_parts.append(r[c:])
        vbs.append(jnp.concatenate(v_parts, 0).astype(BF16))
        o_inter.append(jnp.concatenate(o_parts, 0))

    o_all = []
    for j in range(streams):
        qk = qk_ref[j, 0]
        o_all.append(o_inter[j] + jnp.concatenate(
            [jnp.dot(qk[p * DN_PAIR:(p + 1) * DN_PAIR], vbs[j][p * DN_PAIR:(p + 1) * DN_PAIR],
                     preferred_element_type=F32) for p in range(DN_STACK // DN_PAIR)], 0))

    for j in range(streams):
        kdt = kdt_ref[j, 0]
        k_big = jnp.concatenate([jnp.where(col_group == g, kdt, jnp.zeros_like(kdt)) for g in range(groups)], 0)
        gl = gl_ref[j, 0]
        scale = jnp.concatenate([jnp.broadcast_to(gl[g:g + 1, :], (hd, LANE)) for g in range(groups)], 0)
        sbuf[j * span:(j + 1) * span, :] = (sbuf[j * span:(j + 1) * span, :] * scale
                                             + jnp.dot(k_big, vbs[j], preferred_element_type=F32))

    for j in range(streams):
        outs = []
        for h in range(N_HEADS_C):
            oh = o_all[j][h * DN_UNIT:(h + 1) * DN_UNIT]
            on = oh * lax.rsqrt(jnp.mean(oh * oh, -1, keepdims=True) + RMS_EPS) * nw_ref[...]
            zh = z_refs[j][:, h * hd:(h + 1) * hd]
            outs.append(on * (zh * _sigmoid(zh)))
        o_ref[j] = jnp.concatenate(outs, -1).astype(o_ref.dtype)

    if carry:
        pl.when(step == pl.num_programs(0) - 1)(store_state)
    else:
        store_state()


def _dn_seq(prep, z_src, nw, s0, *, zcol, streams, steps, c, nsq, carry, layer=None):
    groups = DN_STACK // c
    prep = [t.reshape((streams, steps) + t.shape[1:]) for t in prep]
    unit = lambda r, cdim: pl.BlockSpec((streams, 1, r, cdim), lambda i: (0, i, 0, 0))
    sdim = (streams, 1, nsq, N_HEADS_C, HEAD_DIM_C, HEAD_DIM_C)
    state = pl.BlockSpec(sdim, (lambda i: (0, 0, 0, 0, 0, 0)) if carry else (lambda i: (0, i, 0, 0, 0, 0)))
    state_in, state_shape = state, s0.shape
    if layer is not None:
        state_in = pl.BlockSpec((None,) + sdim, lambda i: (layer, 0, 0 if carry else i, 0, 0, 0, 0))
        state_shape = s0.shape[1:]
    z_specs = [pl.BlockSpec((DN_UNIT, V_C), functools.partial(lambda i, j: (j * steps + i, zcol), j=j))
               for j in range(streams)]
    o, s_out = pl.pallas_call(
        functools.partial(_dn_seq_kernel, c=c, nsq=nsq, streams=streams, carry=carry),
        grid=(steps,),
        in_specs=[unit(DN_STACK, HEAD_DIM_C), unit(DN_STACK, HEAD_DIM_C), unit(DN_STACK, HEAD_DIM_C),
                  unit(HEAD_DIM_C, DN_STACK), unit(DN_STACK, DN_PAIR), unit(DN_GL_ROWS, LANE)]
                 + z_specs + [pl.BlockSpec((1, HEAD_DIM_C), lambda i: (0, 0)), state_in],
        out_specs=[pl.BlockSpec((streams, DN_UNIT, V_C), lambda i: (0, i, 0)), state],
        out_shape=[jax.ShapeDtypeStruct((streams, steps * DN_UNIT, V_C), BF16),
                   jax.ShapeDtypeStruct(state_shape, F32)],
        scratch_shapes=[pltpu.VMEM((streams * groups * HEAD_DIM_C, HEAD_DIM_C), F32)],
        compiler_params=_cparams(("arbitrary",)),
        name="deltanet_seq",
    )(*prep, *([z_src] * streams), nw, s0)
    return o.reshape(streams * steps * DN_UNIT, V_C), s_out


def _merge_kernel(oap_ref, oas_ref, cbp_ref, cbs_ref, ocp_ref, ocs_ref, ga_ref, gb_ref, gc_ref, x_ref,
                  woa_ref, wob_ref, woc_ref, wout_ref, g1_ref, b1_ref, wr_ref, br_ref,
                  x1_ref, route_ref, gate_ref, cnt_ref, carry, *, alpha, prompt_tiles):
    @pl.when(pl.program_id(0) == 0)
    def _():
        carry[...] = jnp.zeros(carry.shape, F32)

    is_prompt = pl.program_id(0) < prompt_tiles
    pick = lambda p_ref, s_ref: jnp.where(is_prompt, p_ref[...], s_ref[...])
    br_a = jnp.dot(pick(oap_ref, oas_ref), woa_ref[...], preferred_element_type=F32)
    br_b = jnp.dot(pick(cbp_ref, cbs_ref), wob_ref[...], preferred_element_type=F32)
    br_c = jnp.dot(pick(ocp_ref, ocs_ref), woc_ref[...], preferred_element_type=F32)
    mixin = _sigmoid(ga_ref[...]) * br_a + _sigmoid(gb_ref[...]) * br_b + _sigmoid(gc_ref[...]) * br_c
    mix = jnp.dot(mixin.astype(BF16), wout_ref[...], preferred_element_type=F32)
    x1 = _layer_norm(alpha * x_ref[...] + mix, g1_ref[...], b1_ref[...])
    x1_ref[...] = x1
    logits = jnp.dot(x1.astype(BF16), wr_ref[...], preferred_element_type=F32) + br_ref[...]
    lane = lax.broadcasted_iota(jnp.int32, logits.shape, 1)
    logits = jnp.where(lane < N_EXPERTS, logits, -jnp.inf)
    route = jnp.zeros(logits.shape, jnp.int32)
    val_out = jnp.zeros(logits.shape, F32)
    member = jnp.zeros(logits.shape, F32)
    sels = []
    top0 = None
    den = None
    for k in range(TOP_K):
        m = jnp.max(logits, -1, keepdims=True)
        sel = jnp.min(jnp.where(logits == m, lane, LANE), -1, keepdims=True)
        if k == 0:
            top0 = m
            e = jnp.ones_like(m)
            den = e
        else:
            e = jnp.exp(m - top0)
            den = den + e
        hit = lane == sel
        route = jnp.where(lane == k, sel, route)
        val_out = jnp.where(lane == k, e, val_out)
        member = jnp.where(hit, 1.0, member)
        logits = jnp.where(hit, -jnp.inf, logits)
        sels.append(sel)
    gate_ref[...] = val_out / den
    tm = logits.shape[0]
    earlier = lax.broadcasted_iota(jnp.int32, (tm, tm), 0) > lax.broadcasted_iota(jnp.int32, (tm, tm), 1)
    cum = jnp.dot(jnp.where(earlier, 1.0, 0.0).astype(BF16), member.astype(BF16),
                  preferred_element_type=F32) + carry[...]
    for k in range(TOP_K):
        rank = jnp.sum(jnp.where(lane == sels[k], cum, 0.0), -1, keepdims=True)
        route = jnp.where(lane == TOP_K + k, rank.astype(jnp.int32), route)
    route_ref[...] = route
    carry[...] = carry[...] + jnp.sum(member, axis=0, keepdims=True)
    cnt_ref[...] = carry[...]


def _merge(branches, h, x, lw, alpha):
    m = x.shape[0]
    tp = branches[0][0].shape[0]
    tm = _tile(math.gcd(tp, m - tp), 512)
    np_tiles = tp // tm
    row = lambda i: (i, 0)
    full = lambda i: (0, 0)
    gcol = COL_GATE // D_MODEL
    p_spec = pl.BlockSpec((tm, Q_A), lambda i: (jnp.minimum(i, np_tiles - 1), 0))
    s_spec = pl.BlockSpec((tm, Q_A), lambda i: (jnp.maximum(i - np_tiles, 0), 0))
    return pl.pallas_call(
        functools.partial(_merge_kernel, alpha=alpha, prompt_tiles=np_tiles),
        grid=(m // tm,),
        in_specs=[p_spec, s_spec, p_spec, s_spec, p_spec, s_spec,
                  pl.BlockSpec((tm, D_MODEL), lambda i: (i, gcol)),
                  pl.BlockSpec((tm, D_MODEL), lambda i: (i, gcol + 1)),
                  pl.BlockSpec((tm, D_MODEL), lambda i: (i, gcol + 2)),
                  pl.BlockSpec((tm, D_MODEL), row),
                  pl.BlockSpec((Q_A, D_MODEL), full), pl.BlockSpec((C_B, D_MODEL), full),
                  pl.BlockSpec((V_C, D_MODEL), full), pl.BlockSpec((D_MODEL, D_MODEL), full),
                  pl.BlockSpec((1, D_MODEL), full), pl.BlockSpec((1, D_MODEL), full),
                  pl.BlockSpec((D_MODEL, LANE), full), pl.BlockSpec((1, LANE), full)],
        out_specs=[pl.BlockSpec((tm, D_MODEL), row), pl.BlockSpec((tm, LANE), row),
                   pl.BlockSpec((tm, LANE), row), pl.BlockSpec((1, LANE), full)],
        out_shape=[jax.ShapeDtypeStruct((m, D_MODEL), F32), jax.ShapeDtypeStruct((m, LANE), jnp.int32),
                   jax.ShapeDtypeStruct((m, LANE), F32), jax.ShapeDtypeStruct((1, LANE), F32)],
        scratch_shapes=[pltpu.VMEM((1, LANE), F32)],
        compiler_params=_cparams(("arbitrary",)),
        name="merge_router",
    )(*branches[0], *branches[1], *branches[2], h, h, h, x, lw["w_o_a"], lw["w_o_b"], lw["w_o_c"], lw["w_out"],
      lw["ln1_g"], lw["ln1_b"], lw["w_router"], lw["b_router"])


def _row_copy(src, src_row, dst, dst_row, sem):
    return pltpu.make_async_copy(src.at[pl.ds(src_row, 1)], dst.at[pl.ds(dst_row, 1)], sem)


def _dispatch_kernel(dest_ref, last_ref, x_ref, xs_ref, zbuf, sem, *, tt, n_blk):
    @pl.when(pl.program_id(0) == 0)
    def _():
        zbuf[...] = jnp.zeros(zbuf.shape, F32)

        def zero_block(first_row):
            return pltpu.make_async_copy(zbuf, xs_ref.at[pl.ds(pl.multiple_of(first_row, MOE_ROWS), MOE_ROWS)], sem)

        def start_tail(b, carry):
            zero_block(b * MOE_ROWS).start()
            return carry

        def wait_tail(b, carry):
            zero_block(0).wait()
            return carry

        n_active = last_ref[N_EXPERTS]
        for e in range(N_EXPERTS):
            zero_block(last_ref[e]).start()
        lax.fori_loop(n_active, n_blk, start_tail, 0)
        for e in range(N_EXPERTS):
            zero_block(0).wait()
        lax.fori_loop(n_active, n_blk, wait_tail, 0)

    def body(t, carry):
        for k in range(TOP_K):
            _row_copy(x_ref, t, xs_ref, dest_ref[t * TOP_K + k], sem).start(priority=k % 2)
        return carry

    lax.fori_loop(0, tt, body, 0, unroll=4)
    for k in range(TOP_K):
        pltpu.make_async_copy(x_ref, xs_ref.at[pl.ds(0, tt)], sem).wait()


def _dispatch(dest, last_blk, x1, n_rows):
    m = x1.shape[0]
    tt = _tile(m, 512)
    return pl.pallas_call(
        functools.partial(_dispatch_kernel, tt=tt, n_blk=n_rows // MOE_ROWS),
        grid=(m // tt,),
        in_specs=[pl.BlockSpec((tt * TOP_K,), lambda i: (i,), memory_space=pltpu.SMEM),
                  pl.BlockSpec(memory_space=pltpu.SMEM),
                  pl.BlockSpec((tt, D_MODEL), lambda i: (i, 0))],
        out_specs=pl.BlockSpec(memory_space=pl.ANY),
        out_shape=jax.ShapeDtypeStruct((n_rows, D_MODEL), F32),
        scratch_shapes=[pltpu.VMEM((MOE_ROWS, D_MODEL), F32), pltpu.SemaphoreType.DMA(())],
        compiler_params=pltpu.CompilerParams(dimension_semantics=("arbitrary",), vmem_limit_bytes=VMEM_LIMIT,
                                             disable_bounds_checks=True),
        name="moe_dispatch",
    )(dest, last_blk, x1)


def _moe_kernel(be_ref, na_ref, nv_ref, x_ref, wu_ref, bu_ref, wd_ref, bd_ref, o_ref, wu_b, wd_b):
    i = pl.program_id(0)
    active = i < na_ref[0]
    new_expert = jnp.logical_or(i == 0, be_ref[i] != be_ref[jnp.maximum(i - 1, 0)])

    @pl.when(jnp.logical_and(active, new_expert))
    def _():
        wu_b[...] = wu_ref[0, 0].astype(BF16)
        wd_b[...] = wd_ref[0, 0].astype(BF16)

    filled = (nv_ref[i] + MOE_SUB - 1) // MOE_SUB
    for q in range(1, MOE_ROWS // MOE_SUB + 1):
        rows = q * MOE_SUB

        @pl.when(jnp.logical_and(active, filled == q))
        def _(rows=rows):
            hu = jnp.dot(x_ref[0:rows, :].astype(BF16), wu_b[...], preferred_element_type=F32) + bu_ref[0, 0]
            gt = jnp.minimum(hu[:, :D_FF], SWIGLU_LIMIT)
            up = jnp.clip(hu[:, D_FF:], -SWIGLU_LIMIT, SWIGLU_LIMIT)
            act = (up + 1.0) * gt * _sigmoid(SWIGLU_ALPHA * gt)
            o_ref[0:rows, :] = jnp.dot(act.astype(BF16), wd_b[...], preferred_element_type=F32) + bd_ref[0, 0]
            if rows < MOE_ROWS:
                o_ref[rows:MOE_ROWS, :] = jnp.zeros((MOE_ROWS - rows, D_MODEL), F32)

    @pl.when(jnp.logical_not(active))
    def _():
        o_ref[...] = jnp.zeros(o_ref.shape, F32)


def _moe(xs, blk_e, n_active, n_valid, w_up, b_up, w_down, b_down, layer):
    nblk = xs.shape[0] // MOE_ROWS

    def blk(i, na):
        return jnp.minimum(i, jnp.maximum(na[0] - 1, 0))

    weights = lambda i, be, na, nv: (layer, be[blk(i, na)], 0, 0)
    grid_spec = pltpu.PrefetchScalarGridSpec(
        num_scalar_prefetch=3,
        grid=(nblk,),
        in_specs=[pl.BlockSpec((MOE_ROWS, D_MODEL), lambda i, be, na, nv: (blk(i, na), 0)),
                  pl.BlockSpec((1, 1, D_MODEL, 2 * D_FF), weights),
                  pl.BlockSpec((1, 1, 1, 2 * D_FF), weights),
                  pl.BlockSpec((1, 1, D_FF, D_MODEL), weights),
                  pl.BlockSpec((1, 1, 1, D_MODEL), weights)],
        out_specs=pl.BlockSpec((MOE_ROWS, D_MODEL), lambda i, be, na, nv: (i, 0)),
        scratch_shapes=[pltpu.VMEM((D_MODEL, 2 * D_FF), BF16), pltpu.VMEM((D_FF, D_MODEL), BF16)],
    )
    return pl.pallas_call(
        _moe_kernel,
        grid_spec=grid_spec,
        out_shape=jax.ShapeDtypeStruct((nblk * MOE_ROWS, D_MODEL), F32),
        compiler_params=_cparams(("arbitrary",)),
        name="moe_experts",
    )(blk_e, n_active, n_valid, xs, w_up, b_up, w_down, b_down)

def _post_kernel(dest_ref, yb_ref, gate_ref, x1_ref, p_ref, wg_ref, wp_ref, g2_ref, b2_ref, x_ref, xb_ref,
                 ybuf, sem, *, alpha, tm):
    def body(t, carry):
        for k in range(TOP_K):
            _row_copy(yb_ref, dest_ref[t * TOP_K + k], ybuf.at[k], t, sem).start(priority=k % 2)
        return carry

    lax.fori_loop(0, tm, body, 0, unroll=4)
    for k in range(TOP_K):
        pltpu.make_async_copy(yb_ref.at[pl.ds(0, tm)], ybuf.at[k], sem).wait()

    gate = gate_ref[...]
    ffn = gate[:, 0:1] * ybuf[0]
    for k in range(1, TOP_K):
        ffn = ffn + gate[:, k:k + 1] * ybuf[k]
    x2 = _layer_norm(alpha * x1_ref[...] + ffn, g2_ref[...], b2_ref[...])
    gl = jnp.dot(x2.astype(BF16), wg_ref[...], preferred_element_type=F32)
    pe = jnp.dot(p_ref[...].astype(BF16), wp_ref[...], preferred_element_type=F32)
    xo = x2 + _sigmoid(gl) * pe
    x_ref[...] = xo
    xb_ref[...] = xo.astype(BF16)


def _post(dest, yb, gate, x1, p, lw, alpha):
    m = x1.shape[0]
    tm = _tile(m, 512)
    row = lambda i: (i, 0)
    full = lambda i: (0, 0)
    return pl.pallas_call(
        functools.partial(_post_kernel, alpha=alpha, tm=tm),
        grid=(m // tm,),
        in_specs=[pl.BlockSpec((tm * TOP_K,), lambda i: (i,), memory_space=pltpu.SMEM),
                  pl.BlockSpec(memory_space=pl.ANY),
                  pl.BlockSpec((tm, LANE), row),
                  pl.BlockSpec((tm, D_MODEL), row), pl.BlockSpec((tm, PLE_DIM), row),
                  pl.BlockSpec((D_MODEL, D_MODEL), full), pl.BlockSpec((PLE_DIM, D_MODEL), full),
                  pl.BlockSpec((1, D_MODEL), full), pl.BlockSpec((1, D_MODEL), full)],
        out_specs=[pl.BlockSpec((tm, D_MODEL), row), pl.BlockSpec((tm, D_MODEL), row)],
        out_shape=[jax.ShapeDtypeStruct((m, D_MODEL), F32), jax.ShapeDtypeStruct((m, D_MODEL), BF16)],
        scratch_shapes=[pltpu.VMEM((TOP_K, tm, D_MODEL), F32), pltpu.SemaphoreType.DMA(())],
        compiler_params=pltpu.CompilerParams(dimension_semantics=("arbitrary",), vmem_limit_bytes=VMEM_LIMIT,
                                             disable_bounds_checks=True),
        name="combine_ple",
    )(dest, yb, gate, x1, p, lw["w_ple_gate"], lw["w_ple_proj"], lw["ln2_g"], lw["ln2_b"])


def _rel_bucket(n):
    max_exact = NUM_BUCKETS // 2
    nf = jnp.maximum(n, 1).astype(F32)
    large = max_exact + (jnp.log(nf / max_exact) / math.log(MAX_DISTANCE / max_exact)
                         * (NUM_BUCKETS - max_exact)).astype(jnp.int32)
    return jnp.where(n < max_exact, jnp.maximum(n, 0), jnp.minimum(large, NUM_BUCKETS - 1))


def _bias_table(rel_bias, dist, valid):
    b = rel_bias.astype(F32)[_rel_bucket(dist)]
    b = jnp.where(valid[..., None], b, NEG_INF)
    return jnp.moveaxis(b, -1, 0)


def _permute_in(w):
    o = np.concatenate([[0], np.cumsum(IN_SIZES)])
    seg = lambda i: w[..., o[i]:o[i + 1]]
    small = jnp.concatenate([seg(6), seg(7)], -1)
    small = jnp.pad(small, [(0, 0)] * (w.ndim - 1) + [(0, LANE - 2 * N_HEADS_C)])
    return jnp.concatenate([seg(8), seg(3), seg(4), seg(5), seg(0), seg(1), seg(2), small], -1)


def _route(route, counts, n_tok):
    n_blk = -(-n_tok * TOP_K // MOE_ROWS) + N_EXPERTS
    cnt = counts[0, :N_EXPERTS].astype(jnp.int32)
    padded = (cnt + MOE_ROWS - 1) // MOE_ROWS * MOE_ROWS
    pad_end = jnp.cumsum(padded)
    pad_start = pad_end - padded
    idx = route[:, 0:TOP_K]
    rank = route[:, TOP_K:2 * TOP_K]
    start_of = jnp.sum(jnp.where(idx[:, :, None] == jnp.arange(N_EXPERTS)[None, None, :],
                                 pad_start[None, None, :], 0), -1)
    dest = (start_of + rank).reshape(n_tok * TOP_K).astype(jnp.int32)
    blk_first = (jnp.arange(n_blk) * MOE_ROWS)[:, None]
    blk_e = jnp.minimum(jnp.sum((pad_end[None, :] <= blk_first).astype(jnp.int32), -1), N_EXPERTS - 1)
    n_active = (pad_end[-1:] // MOE_ROWS).astype(jnp.int32)
    n_valid = jnp.clip(cnt[blk_e] - (blk_first[:, 0] - pad_start[blk_e]), 0, MOE_ROWS).astype(jnp.int32)
    last_blk = jnp.concatenate([jnp.maximum(pad_end - MOE_ROWS, 0).astype(jnp.int32), n_active])
    return dest, last_blk, blk_e.astype(jnp.int32), n_active, n_valid, n_blk


def kernel(x_prompt, x_sample, cache_k_a, cache_v_a, state_conv_b, state_conv_c, state_s_c, p_prompt, p_sample,
           rel_bias, w_in, b_in, sink_a, w_o_a, w_dw_b, b_dw_b, ln_b_g, ln_b_b, w_o_b, w_conv_c, a_log_c,
           dt_bias_c, norm_c, w_o_c, w_out, ln1_g, ln1_b, w_router, b_router, w_up, b_up, w_down, b_down,
           ln2_g, ln2_b, w_ple_gate, w_ple_proj):
    depth = w_in.shape[0]
    nb_p, seq, _ = x_prompt.shape
    ns, ls, _ = x_sample.shape
    w_buf = cache_k_a.shape[2]
    alpha = (2 * depth) ** 0.25
    tp = nb_p * seq
    tsm = ns * ls
    n_tok = tp + tsm
    assert seq % DN_CHUNK == 0 and seq % Q_BLOCK == 0 and w_buf == WINDOW and ls <= 16

    w_in_p = _permute_in(w_in).astype(BF16)
    b_in_p = _permute_in(b_in)[:, None, :]
    w_o_a_b, w_o_b_b, w_o_c_b, w_out_b = (t.astype(BF16) for t in (w_o_a, w_o_b, w_o_c, w_out))
    w_ple_gate_b, w_ple_proj_b = w_ple_gate.astype(BF16), w_ple_proj.astype(BF16)
    b_up4, b_down4 = b_up[:, :, None, :], b_down[:, :, None, :]
    w_router_p = jnp.pad(w_router, ((0, 0), (0, 0), (0, LANE - N_EXPERTS))).astype(BF16)
    b_router_p = jnp.pad(b_router, ((0, 0), (0, LANE - N_EXPERTS)))[:, None, :]
    w_dw_p = jnp.pad(w_dw_b, ((0, 0), (0, HALO_B - CONV_B), (0, 0)))
    w_cc_p = jnp.pad(w_conv_c, ((0, 0), (0, SUBLANE - CONV_C), (0, 0)))
    lane_pad = lambda t: jnp.pad(t, ((0, 0), (N_HEADS_C, LANE - 2 * N_HEADS_C)))[:, None, :]
    alog_rows, dtb_rows = lane_pad(a_log_c), lane_pad(dt_bias_c)

    qi = jnp.arange(Q_BLOCK)[:, None]
    kj = jnp.arange(2 * Q_BLOCK)[None, :]
    dist = qi - kj + Q_BLOCK
    valid = (dist >= 0) & (dist < WINDOW)
    tab_rest = _bias_table(rel_bias, dist, valid)
    tab_first = _bias_table(rel_bias, dist, valid & (kj >= Q_BLOCK))
    bias_prompt = jnp.stack([tab_first, tab_rest])

    kwin = 2 * WINDOW
    ti = jnp.arange(ls)[:, None]
    kjs = jnp.arange(kwin)[None, :]
    dist_s = w_buf + ti - kjs
    valid_s = (dist_s >= 0) & (dist_s < WINDOW) & (kjs < w_buf + ls)
    tab_s = _bias_table(rel_bias, dist_s, valid_s)
    tab_s = tab_s.reshape(N_KV_A, GROUP_A * ls, kwin)
    gb = 2 * _tile(ns, 16, 1)
    bias_s = jnp.tile(tab_s, (gb // N_KV_A, 1, 1))

    wrows = -(-(CONV_B - 1 + ls) // SUBLANE) * SUBLANE
    jj = jnp.arange(wrows)[None, :] - jnp.arange(ls)[:, None]
    tap_ok = (jj >= 0) & (jj < CONV_B)

    x_all = jnp.concatenate([x_prompt.reshape(tp, D_MODEL), x_sample.reshape(tsm, D_MODEL)], 0)
    xb_all = x_all.astype(BF16)
    p_all = jnp.concatenate([p_prompt.reshape(depth, tp, PLE_DIM), p_sample.reshape(depth, tsm, PLE_DIM)], 1)
    zero_s = jnp.zeros((nb_p, 1, 1, N_HEADS_C, HEAD_DIM_C, HEAD_DIM_C), F32)
    c_s = 16
    nsq_s = DN_UNIT // c_s
    assert ls + CONV_C - 1 <= c_s and ns % nsq_s == 0

    st_p, st_s = [], []
    for i in range(depth):
        lw = {"w_o_a": w_o_a_b[i], "w_o_b": w_o_b_b[i], "w_o_c": w_o_c_b[i], "w_out": w_out_b[i],
              "ln1_g": ln1_g[i][None], "ln1_b": ln1_b[i][None], "w_router": w_router_p[i],
              "b_router": b_router_p[i], "w_ple_gate": w_ple_gate_b[i], "w_ple_proj": w_ple_proj_b[i],
              "ln2_g": ln2_g[i][None], "ln2_b": ln2_b[i][None]}
        h = _in_proj(xb_all, w_in_p[i], b_in_p[i])
        hs = h[tp:].reshape(ns, ls, N_IN_PAD)

        sinks = sink_a[i].astype(F32)
        oa_p = _attn_prompt(h, sinks, bias_prompt, nb_p, seq)
        q_s = hs[:, :, COL_QA:COL_QA + Q_A].reshape(ns, ls, N_KV_A, GROUP_A, HEAD_DIM_A)
        q_s = q_s.transpose(0, 2, 3, 1, 4).reshape(ns * N_KV_A, GROUP_A * ls, HEAD_DIM_A)
        k_new = hs[:, :, COL_KA:COL_KA + KV_A].reshape(ns, ls, N_KV_A, HEAD_DIM_A)
        v_new = hs[:, :, COL_VA:COL_VA + KV_A].reshape(ns, ls, N_KV_A, HEAD_DIM_A)
        k_all = jnp.concatenate([cache_k_a[i], k_new], 1)
        v_all = jnp.concatenate([cache_v_a[i], v_new], 1)

        def keywin(t):
            t = jnp.pad(t, ((0, 0), (0, kwin - w_buf - ls), (0, 0), (0, 0)))
            return t.transpose(0, 2, 1, 3).reshape(ns * N_KV_A, kwin, HEAD_DIM_A)

        sink_s = jnp.tile(jnp.repeat(sinks.reshape(N_KV_A, GROUP_A), ls, axis=1)[:, :, None], (gb // N_KV_A, 1, 1))
        oa_s = _attn_sample(q_s, keywin(k_all), keywin(v_all), bias_s, sink_s)
        oa_s = oa_s.reshape(ns, N_KV_A, GROUP_A, ls, HEAD_DIM_A).transpose(0, 3, 1, 2, 4).reshape(tsm, Q_A)
        oa = (oa_p, oa_s.astype(BF16))

        cb_p, tail_p = _convb_prompt(h, w_dw_p[i], b_dw_b[i][None], ln_b_g[i][None], ln_b_b[i][None], nb_p, seq)
        wt = jnp.where(tap_ok[:, :, None], w_dw_b[i][jnp.clip(jj, 0, CONV_B - 1)], 0.0)
        cb_s, nconv_b_s = _convb_sample(hs, state_conv_b[i], wt, b_dw_b[i][None], ln_b_g[i][None], ln_b_b[i][None])
        cb = (cb_p, cb_s.transpose(1, 0, 2).reshape(tsm, C_B))

        nw = norm_c[i][None]
        nchunk = seq // DN_CHUNK
        prep_p = _dn_prep(h, h, w_cc_p[i], alog_rows[i], dtb_rows[i], qcol=COL_QKVC // V_C,
                          smcol=COL_SMALL // LANE, nbatch=nb_p, nchunk=nchunk, c=DN_CHUNK, lr=DN_CHUNK)
        oc_p, s_p = _dn_seq(prep_p, h, nw, zero_s, zcol=COL_Z // V_C, streams=nb_p, steps=nchunk,
                            c=DN_CHUNK, nsq=1, carry=True)
        s_p = s_p.reshape(nb_p, N_HEADS_C, HEAD_DIM_C, HEAD_DIM_C)
        lead = c_s - ls
        hist = jnp.concatenate([jnp.zeros((ns, lead - (CONV_C - 1), QKV_C), F32), state_conv_c[i],
                                hs[:, :, COL_QKVC:COL_QKVC + QKV_C]], 1).reshape(ns * c_s, QKV_C)
        pad_s = lambda col, wid: jnp.pad(hs[:, :, col:col + wid], ((0, 0), (lead, 0), (0, 0))).reshape(ns * c_s, wid)
        prep_s = _dn_prep(hist, pad_s(COL_SMALL, LANE), w_cc_p[i], alog_rows[i], dtb_rows[i], qcol=0, smcol=0,
                          nbatch=ns // nsq_s, nchunk=1, c=c_s, lr=ls)
        units_s = ns // nsq_s
        streams_s = 2 if units_s % 2 == 0 else 1
        s0_s = state_s_c.astype(F32).reshape(depth, streams_s, units_s // streams_s, nsq_s, N_HEADS_C, HEAD_DIM_C,
                                             HEAD_DIM_C)
        oc_s, s_s = _dn_seq(prep_s, pad_s(COL_Z, V_C), nw, s0_s, zcol=0, streams=streams_s,
                            steps=units_s // streams_s, c=c_s, nsq=nsq_s, carry=False, layer=i)
        s_s = s_s.reshape(ns, N_HEADS_C, HEAD_DIM_C, HEAD_DIM_C)
        oc = (oc_p, oc_s.reshape(ns, c_s, V_C)[:, lead:].reshape(tsm, V_C))

        x1, route, gate, counts = _merge((oa, cb, oc), h, x_all, lw, alpha)

        dest, last_blk, blk_e, n_active, n_valid, n_blk = _route(route, counts, n_tok)
        xs = _dispatch(dest, last_blk, x1, n_blk * MOE_ROWS)
        yb = _moe(xs, blk_e, n_active, n_valid, w_up, b_up4, w_down, b_down4, i)

        x_all, xb_all = _post(dest, yb, gate, x1, p_all[i], lw, alpha)

        def tail_rows(nrows, col, width):
            return jnp.stack([h[(b + 1) * seq - nrows:(b + 1) * seq, col:col + width] for b in range(nb_p)])

        kp_new = tail_rows(w_buf, COL_KA, KV_A)
        vp_new = tail_rows(w_buf, COL_VA, KV_A)
        ccp_new = tail_rows(CONV_C - 1, COL_QKVC, QKV_C)
        st_p.append((kp_new.reshape(nb_p, w_buf, N_KV_A, HEAD_DIM_A), vp_new.reshape(nb_p, w_buf, N_KV_A, HEAD_DIM_A),
                     tail_p[:, HALO_B - (CONV_B - 1):], ccp_new, s_p))
        ccs_ext = jnp.concatenate([state_conv_c[i], hs[:, :, COL_QKVC:COL_QKVC + QKV_C]], 1)
        st_s.append((k_all[:, ls:], v_all[:, ls:], nconv_b_s, ccs_ext[:, -(CONV_C - 1):], s_s))

    yp = x_all[:tp].reshape(nb_p, seq, D_MODEL)
    ys = x_all[tp:].reshape(ns, ls, D_MODEL)
    kp, vp, cbp, ccp, scp = [jnp.stack(z) for z in zip(*st_p)]
    ks_, vs_, cbs, ccs, scs = [jnp.stack(z) for z in zip(*st_s)]
    return (yp, ys, kp, vp, cbp, ccp, scp, ks_, vs_, cbs, ccs, scs)
```

```python
import functools
import math

import numpy as np
import jax
import jax.numpy as jnp
from jax import lax
from jax.experimental import pallas as pl
from jax.experimental.pallas import tpu as pltpu

F32 = jnp.float32
BF16 = jnp.bfloat16
HIGHEST = lax.Precision.HIGHEST

D_MODEL = 1024
N_HEADS_A, N_KV_A, HEAD_DIM_A = 8, 2, 64
GROUP_A = N_HEADS_A // N_KV_A
Q_A = N_HEADS_A * HEAD_DIM_A
KV_A = N_KV_A * HEAD_DIM_A
WINDOW = 128
Q_BLOCK = 128
NUM_BUCKETS = 32
MAX_DISTANCE = 128
NEG_INF = -1e30
C_B = 512
CONV_B = 31
N_HEADS_C, HEAD_DIM_C = 4, 128
V_C = N_HEADS_C * HEAD_DIM_C
QKV_C = 3 * V_C
CONV_C = 4
DN_CHUNK = 64
N_BRANCH = 3
IN_SIZES = (Q_A, KV_A, KV_A, 2 * C_B, QKV_C, V_C, N_HEADS_C, N_HEADS_C, N_BRANCH * D_MODEL)
N_EXPERTS = 32
TOP_K = 4
D_FF = 1024
SWIGLU_LIMIT = 7.0
SWIGLU_ALPHA = 1.702
PLE_DIM = 256
LN_EPS = 1e-5
RMS_EPS = 1e-6

LANE = 128
SUBLANE = 8
VMEM_LIMIT = 56 * 1024 * 1024

COL_GATE = 0
COL_GLU = 3 * D_MODEL
COL_QKVC = COL_GLU + 2 * C_B
COL_Z = COL_QKVC + QKV_C
COL_QA = COL_Z + V_C
COL_KA = COL_QA + Q_A
COL_VA = COL_KA + KV_A
COL_SMALL = COL_VA + KV_A
N_IN_PAD = COL_SMALL + LANE
MOE_ROWS = 512
HALO_B = 32
HALO_C = 8


def _tile(n, target, mult=SUBLANE):
    best = None
    for t in range(mult, min(n, target) + 1, mult):
        if n % t == 0:
            best = t
    return best if best is not None else n


def _cparams(sem):
    return pltpu.CompilerParams(dimension_semantics=sem, vmem_limit_bytes=VMEM_LIMIT)


def _layer_norm(x, g, b):
    mu = jnp.mean(x, -1, keepdims=True)
    xc = x - mu
    var = jnp.mean(xc * xc, -1, keepdims=True)
    return xc * lax.rsqrt(var + LN_EPS) * g + b


def _sigmoid(x):
    return 1.0 / (1.0 + jnp.exp(-x))


def _round_bf16(x):
    return x.astype(BF16).astype(F32)


def _mm_bias_kernel(x_ref, w_ref, b_ref, o_ref):
    o_ref[...] = jnp.dot(x_ref[...], w_ref[...], preferred_element_type=F32) + b_ref[...]


def _in_proj(x_bf, w_bf, b):
    m, k = x_bf.shape
    n = w_bf.shape[1]
    tm = _tile(m, 1536)
    tn = _tile(n, 1408, LANE)
    return pl.pallas_call(
        _mm_bias_kernel,
        grid=(n // tn, m // tm),
        in_specs=[pl.BlockSpec((tm, k), lambda j, i: (i, 0)),
                  pl.BlockSpec((k, tn), lambda j, i: (0, j)),
                  pl.BlockSpec((1, tn), lambda j, i: (0, j))],
        out_specs=pl.BlockSpec((tm, tn), lambda j, i: (i, j)),
        out_shape=jax.ShapeDtypeStruct((m, n), F32),
        compiler_params=_cparams(("arbitrary", "arbitrary")),
        name="in_proj",
    )(x_bf, w_bf, b)


def _attn_prompt_kernel(sink_ref, q_ref, kp_ref, kc_ref, vp_ref, vc_ref, bias_ref, o_ref):
    nt = (((1,), (1,)), ((), ()))
    q = (q_ref[...] * (HEAD_DIM_A ** -0.5)).astype(BF16)
    lo = lax.broadcasted_iota(jnp.int32, (2 * Q_BLOCK, KV_A), 1) < HEAD_DIM_A

    def placed(prev_ref, cur_ref):
        band = jnp.concatenate([prev_ref[...], cur_ref[...]], 0)
        own = (jnp.where(lo, band, 0.0), jnp.where(lo, 0.0, band))
        moved = (pltpu.roll(own[0], HEAD_DIM_A, 1), pltpu.roll(own[1], HEAD_DIM_A, 1))
        return ((own[0].astype(BF16), moved[0].astype(BF16)), (moved[1].astype(BF16), own[1].astype(BF16)))

    kz = placed(kp_ref, kc_ref)
    vz = placed(vp_ref, vc_ref)
    pairs = []
    for kv in range(N_KV_A):
        heads = range(kv * GROUP_A, (kv + 1) * GROUP_A)
        scores = [lax.dot_general(q[:, (h // 2) * LANE:(h // 2 + 1) * LANE], kz[kv][h % 2], nt,
                                  preferred_element_type=F32) + bias_ref[0, h] for h in heads]
        probs = []
        for h, s in zip(heads, scores):
            sink = sink_ref[h]
            m = jnp.maximum(jnp.max(s, -1, keepdims=True), sink)
            e = jnp.exp(s - m)
            inv = 1.0 / (jnp.sum(e, -1, keepdims=True) + jnp.exp(sink - m))
            probs.append((e * inv).astype(BF16))
        pv = [jnp.dot(p, vz[kv][h % 2], preferred_element_type=F32) for h, p in zip(heads, probs)]
        pairs += [pv[0] + pv[1], pv[2] + pv[3]]
    o_ref[...] = jnp.concatenate(pairs, -1).astype(o_ref.dtype)


def _attn_prompt(h, sinks, bias_tab, nbatch, seq):
    nb = seq // Q_BLOCK
    qcol = COL_QA // Q_A
    kcol = COL_KA // KV_A
    vcol = COL_VA // KV_A

    def cur(c):
        return lambda b, i: (b * nb + i, c)

    def prev(c):
        return lambda b, i: (jnp.maximum(b * nb + i - 1, 0), c)

    return pl.pallas_call(
        _attn_prompt_kernel,
        grid=(nbatch, nb),
        in_specs=[pl.BlockSpec(memory_space=pltpu.SMEM),
                  pl.BlockSpec((Q_BLOCK, Q_A), cur(qcol)),
                  pl.BlockSpec((Q_BLOCK, KV_A), prev(kcol)),
                  pl.BlockSpec((Q_BLOCK, KV_A), cur(kcol)),
                  pl.BlockSpec((Q_BLOCK, KV_A), prev(vcol)),
                  pl.BlockSpec((Q_BLOCK, KV_A), cur(vcol)),
                  pl.BlockSpec((1, N_HEADS_A, Q_BLOCK, 2 * Q_BLOCK), lambda b, i: (jnp.minimum(i, 1), 0, 0, 0))],
        out_specs=pl.BlockSpec((Q_BLOCK, Q_A), lambda b, i: (b * nb + i, 0)),
        out_shape=jax.ShapeDtypeStruct((nbatch * seq, Q_A), BF16),
        compiler_params=_cparams(("arbitrary", "arbitrary")),
        name="attn_prompt",
    )(sinks, h, h, h, h, h, bias_tab)


def _attn_sample_kernel(q_ref, k_ref, v_ref, bias_ref, sink_ref, o_ref):
    q = (q_ref[...] * (HEAD_DIM_A ** -0.5)).astype(BF16)
    s = jnp.einsum("bqd,bkd->bqk", q, k_ref[...].astype(BF16), preferred_element_type=F32) + bias_ref[...]
    sink = sink_ref[...]
    m = jnp.maximum(jnp.max(s, -1, keepdims=True), sink)
    e = jnp.exp(s - m)
    pr = e / (jnp.sum(e, -1, keepdims=True) + jnp.exp(sink - m))
    o_ref[...] = jnp.einsum("bqk,bkd->bqd", pr.astype(BF16), v_ref[...].astype(BF16), preferred_element_type=F32)


def _attn_sample(q, k, v, bias, sink):
    g, r, hd = q.shape
    kk = k.shape[1]
    gb = bias.shape[0]
    return pl.pallas_call(
        _attn_sample_kernel,
        grid=(g // gb,),
        in_specs=[pl.BlockSpec((gb, r, hd), lambda i: (i, 0, 0)),
                  pl.BlockSpec((gb, kk, hd), lambda i: (i, 0, 0)),
                  pl.BlockSpec((gb, kk, hd), lambda i: (i, 0, 0)),
                  pl.BlockSpec((gb, r, kk), lambda i: (0, 0, 0)),
                  pl.BlockSpec((gb, r, 1), lambda i: (0, 0, 0))],
        out_specs=pl.BlockSpec((gb, r, hd), lambda i: (i, 0, 0)),
        out_shape=jax.ShapeDtypeStruct((g, r, hd), F32),
        compiler_params=_cparams(("arbitrary",)),
        name="attn_sample",
    )(q, k, v, bias, sink)


CONV_ROWS = 64


def _convb_prompt_kernel(a_ref, b_ref, w_ref, bias_ref, g_ref, beta_ref, cb_ref, tail_ref, ubuf, shbuf, *, tl):
    @pl.when(pl.program_id(1) == 0)
    def _():
        ubuf[0:HALO_B, :] = jnp.zeros((HALO_B, C_B), F32)

    u = a_ref[...] * _sigmoid(b_ref[...])
    tail_ref[0] = u[tl - HALO_B:tl, :]
    ubuf[HALO_B:HALO_B + tl, :] = _round_bf16(u)
    off = HALO_B - (CONV_B - 1)
    span = tl + HALO_B - SUBLANE
    for s in range(1, SUBLANE):
        shbuf[s - 1, 0:span, :] = ubuf[s:s + span, :]

    def body(r, carry):
        base = pl.multiple_of(r * CONV_ROWS, CONV_ROWS)
        accs = []
        for c in range(C_B // LANE):
            cs = slice(c * LANE, (c + 1) * LANE)
            acc = jnp.zeros((CONV_ROWS, LANE), F32)
            for w in range(CONV_B):
                d = off + w
                start = pl.multiple_of(base + (d // SUBLANE) * SUBLANE, SUBLANE)
                if d % SUBLANE == 0:
                    win = ubuf[pl.ds(start, CONV_ROWS), cs]
                else:
                    win = shbuf[d % SUBLANE - 1, pl.ds(start, CONV_ROWS), cs]
                acc = acc + win * w_ref[w:w + 1, cs]
            accs.append(acc + bias_ref[:, cs])
        y = _layer_norm(jnp.concatenate(accs, -1), g_ref[...], beta_ref[...])
        cb_ref[pl.ds(base, CONV_ROWS), :] = (y * _sigmoid(y)).astype(cb_ref.dtype)
        return carry

    lax.fori_loop(0, tl // CONV_ROWS, body, 0)
    ubuf[0:HALO_B, :] = ubuf[tl:tl + HALO_B, :]


def _convb_prompt(h, w_dw, b_dw, ln_g, ln_b, nbatch, seq):
    tl = _tile(seq, 512, CONV_ROWS)
    nt = seq // tl
    acol = COL_GLU // C_B
    return pl.pallas_call(
        functools.partial(_convb_prompt_kernel, tl=tl),
        grid=(nbatch, nt),
        in_specs=[pl.BlockSpec((tl, C_B), lambda b, t: (b * nt + t, acol)),
                  pl.BlockSpec((tl, C_B), lambda b, t: (b * nt + t, acol + 1)),
                  pl.BlockSpec((HALO_B, C_B), lambda b, t: (0, 0)),
                  pl.BlockSpec((1, C_B), lambda b, t: (0, 0)),
                  pl.BlockSpec((1, C_B), lambda b, t: (0, 0)),
                  pl.BlockSpec((1, C_B), lambda b, t: (0, 0))],
        out_specs=[pl.BlockSpec((tl, C_B), lambda b, t: (b * nt + t, 0)),
                   pl.BlockSpec((1, HALO_B, C_B), lambda b, t: (b, 0, 0))],
        out_shape=[jax.ShapeDtypeStruct((nbatch * seq, C_B), BF16),
                   jax.ShapeDtypeStruct((nbatch, HALO_B, C_B), F32)],
        scratch_shapes=[pltpu.VMEM((HALO_B + tl, C_B), F32),
                        pltpu.VMEM((SUBLANE - 1, HALO_B + tl, C_B), F32)],
        compiler_params=_cparams(("arbitrary", "arbitrary")),
        name="convb_prompt",
    )(h, h, w_dw, b_dw, ln_g, ln_b)


def _convb_sample_kernel(a_ref, b_ref, st_ref, wt_ref, bias_ref, g_ref, beta_ref, cb_ref, nst_ref, ext, *, ls, wrows):
    ns = CONV_B - 1
    ext[:, ns - ns % SUBLANE:wrows, :] = jnp.zeros((ext.shape[0], wrows - (ns - ns % SUBLANE), C_B), F32)
    ext[:, 0:ns, :] = st_ref[...]
    ext[:, ns:ns + ls, :] = a_ref[...] * _sigmoid(b_ref[...])
    nst_ref[...] = ext[:, ls:ls + ns, :]
    win = _round_bf16(ext[...])
    for t in range(ls):
        y = jnp.sum(win * wt_ref[t][None], axis=1) + bias_ref[...]
        y = _layer_norm(y, g_ref[...], beta_ref[...])
        cb_ref[t] = (y * _sigmoid(y)).astype(cb_ref.dtype)


def _convb_sample(hs, state, wt, b_dw, ln_g, ln_b):
    s, ls, _ = hs.shape
    wrows = wt.shape[1]
    sb = _tile(s, 16, 1)
    acol = COL_GLU // C_B
    ns = CONV_B - 1
    return pl.pallas_call(
        functools.partial(_convb_sample_kernel, ls=ls, wrows=wrows),
        grid=(s // sb,),
        in_specs=[pl.BlockSpec((sb, ls, C_B), lambda i: (i, 0, acol)),
                  pl.BlockSpec((sb, ls, C_B), lambda i: (i, 0, acol + 1)),
                  pl.BlockSpec((sb, ns, C_B), lambda i: (i, 0, 0)),
                  pl.BlockSpec((ls, wrows, C_B), lambda i: (0, 0, 0)),
                  pl.BlockSpec((1, C_B), lambda i: (0, 0)),
                  pl.BlockSpec((1, C_B), lambda i: (0, 0)),
                  pl.BlockSpec((1, C_B), lambda i: (0, 0))],
        out_specs=[pl.BlockSpec((ls, sb, C_B), lambda i: (0, i, 0)),
                   pl.BlockSpec((sb, ns, C_B), lambda i: (i, 0, 0))],
        out_shape=[jax.ShapeDtypeStruct((ls, s, C_B), BF16),
                   jax.ShapeDtypeStruct((s, ns, C_B), F32)],
        scratch_shapes=[pltpu.VMEM((sb, wrows, C_B), F32)],
        compiler_params=_cparams(("arbitrary",)),
        name="convb_sample",
    )(hs, hs, state, wt, b_dw, ln_g, ln_b)


def _softplus(x):
    return jnp.maximum(x, 0.0) + jnp.log(1.0 + jnp.exp(-jnp.abs(x)))


DN_UNIT = DN_CHUNK
DN_STACK = N_HEADS_C * DN_UNIT
DN_PAIR = 2 * DN_UNIT
DN_GL_ROWS = 16


def _dn_prep_kernel(q_ref, k_ref, v_ref, qp_ref, kp_ref, vp_ref, sm_ref, wc_ref, alog_ref, dtb_ref,
                    u_ref, w_ref, qd_ref, kdt_ref, qk_ref, gl_ref, cbuf, *, c, lr, ups):
    rows = ups * DN_UNIT
    halo = jnp.concatenate([qp_ref[...], kp_ref[...], vp_ref[...]], -1)
    cbuf[0:HALO_C, :] = _round_bf16(jnp.where(pl.program_id(1) == 0, 0.0, halo))
    cbuf[HALO_C:HALO_C + rows, 0:V_C] = _round_bf16(q_ref[...])
    cbuf[HALO_C:HALO_C + rows, V_C:2 * V_C] = _round_bf16(k_ref[...])
    cbuf[HALO_C:HALO_C + rows, 2 * V_C:3 * V_C] = _round_bf16(v_ref[...])
    off = HALO_C - (CONV_C - 1)
    y = cbuf[off:off + rows, :] * wc_ref[0:1, :]
    for w in range(1, CONV_C):
        y = y + cbuf[off + w:off + w + rows, :] * wc_ref[w:w + 1, :]
    y = y * _sigmoid(y)
    outs = (u_ref, w_ref, qd_ref, kdt_ref, qk_ref, gl_ref)
    _dn_prep_tables(y, sm_ref[...], alog_ref[...], dtb_ref[...], outs, ups=ups, c=c, lr=lr)


def _dn_prep_tables(y, small, alog, dtb, outs, *, ups, c, lr):
    u_ref, w_ref, qd_ref, kdt_ref, qk_ref, gl_ref = outs
    nt = (((1,), (1,)), ((), ()))
    hd = HEAD_DIM_C
    beta_f = _sigmoid(small)
    g_f = -jnp.exp(alog) * _softplus(small + dtb)
    if lr < c:
        live = lax.broadcasted_iota(jnp.int32, beta_f.shape, 0) % c >= c - lr
        beta_f = jnp.where(live, beta_f, 0.0)
        g_f = jnp.where(live, g_f, 0.0)

    r2 = lax.broadcasted_iota(jnp.int32, (2 * DN_UNIT, DN_UNIT), 0)
    c2 = lax.broadcasted_iota(jnp.int32, (2 * DN_UNIT, DN_UNIT), 1)
    rr = jnp.where(r2 >= DN_UNIT, r2 - DN_UNIT, r2)
    same2 = (rr // c) == (c2 // c)
    summer = jnp.where(same2 & ((r2 >= DN_UNIT) | (rr >= c2)), 1.0, 0.0).astype(F32)

    ri = lax.broadcasted_iota(jnp.int32, (DN_PAIR, DN_PAIR), 0)
    ci = lax.broadcasted_iota(jnp.int32, (DN_PAIR, DN_PAIR), 1)
    same = (ri // c) == (ci // c)
    tri = same & (ri >= ci)
    stri = same & (ri > ci)
    eye = jnp.where(ri == ci, 1.0, 0.0).astype(F32)
    items = [(slot, pair) for slot in range(ups) for pair in range(N_HEADS_C // 2)]
    st = []
    for slot, pair in items:
        unit_rows = slice(slot * DN_UNIT, (slot + 1) * DN_UNIT)
        yu = y[unit_rows]
        if pair == 0:
            cg = jnp.dot(summer, g_f[unit_rows], precision=HIGHEST, preferred_element_type=F32)
        heads = (2 * pair, 2 * pair + 1)
        stack = lambda first: jnp.concatenate([yu[:, first + h * hd:first + (h + 1) * hd] for h in heads], 0)
        cols = lambda x, first: jnp.concatenate([x[:, first + h:first + h + 1] for h in heads], 0)
        qs, ks, vs = stack(0), stack(V_C), stack(2 * V_C)
        qn = qs * lax.rsqrt(jnp.sum(qs * qs, -1, keepdims=True) + RMS_EPS) * (hd ** -0.5)
        kn = ks * lax.rsqrt(jnp.sum(ks * ks, -1, keepdims=True) + RMS_EPS)
        beta_s = cols(beta_f[unit_rows], 0)
        gcs = cols(cg[0:DN_UNIT], N_HEADS_C)
        gls = cols(cg[DN_UNIT:2 * DN_UNIT], N_HEADS_C)
        grow = jnp.broadcast_to(gcs, (DN_PAIR, LANE)).T[0:1, :]
        decay = jnp.where(tri, jnp.exp(jnp.where(tri, gcs - grow, 0.0)), 0.0)
        kb = kn.astype(BF16)
        kk = lax.dot_general(kb, kb, nt, preferred_element_type=F32)
        nmat = jnp.where(stri, -(beta_s * kk * decay), 0.0)
        egc = jnp.exp(gcs)
        st.append(dict(qn=qn, kn=kn, kb=kb, decay=decay, nmat=nmat, egc=egc, gcs=gcs, gls=gls,
                       rhs=jnp.concatenate([vs * beta_s, kn * (beta_s * egc)], -1).astype(BF16)))

    invs = [eye + s["nmat"] for s in st]
    nbs = [s["nmat"].astype(BF16) for s in st]
    for _ in range(int(math.log2(c)) - 1):
        nbs = [jnp.dot(nb, nb, preferred_element_type=F32).astype(BF16) for nb in nbs]
        invs = [inv + jnp.dot(inv.astype(BF16), nb, preferred_element_type=F32) for inv, nb in zip(invs, nbs)]
    sols = [jnp.dot(inv.astype(BF16), s["rhs"], preferred_element_type=F32) for inv, s in zip(invs, st)]

    gl_rows = {slot: [] for slot in range(ups)}
    for (slot, pair), s, sol in zip(items, st, sols):
        rows = slice(pair * DN_PAIR, (pair + 1) * DN_PAIR)
        u_ref[slot, rows, :] = sol[:, :hd]
        w_ref[slot, rows, :] = sol[:, hd:].astype(BF16)
        qk = jnp.where(tri, lax.dot_general(s["qn"].astype(BF16), s["kb"], nt, preferred_element_type=F32)
                       * s["decay"], 0.0)
        qk_ref[slot, rows, :] = qk.astype(BF16)
        qd_ref[slot, rows, :] = (s["qn"] * s["egc"]).astype(BF16)
        kdt_ref[slot, :, rows] = (s["kn"] * jnp.exp(s["gls"] - s["gcs"])).T.astype(BF16)
        glb = jnp.broadcast_to(jnp.exp(s["gls"]), (DN_PAIR, LANE))
        gl_rows[slot] += [glb[g * c:g * c + 1, :] for g in range(DN_PAIR // c)]
    groups = DN_STACK // c
    for slot in range(ups):
        if groups < DN_GL_ROWS:
            gl_rows[slot].append(jnp.ones((DN_GL_ROWS - groups, LANE), F32))
        gl_ref[slot] = jnp.concatenate(gl_rows[slot], 0)


def _dn_prep(src, small_src, wc, alog_row, dtb_row, *, qcol, smcol, nbatch, nchunk, c, lr):
    nu = nbatch * nchunk
    ups = 2 if (nchunk % 2 == 0 or (nchunk == 1 and nbatch % 2 == 0)) else 1
    if nchunk == 1:
        nbatch //= ups
    else:
        nchunk //= ups
    rows = ups * DN_UNIT
    per8 = rows // SUBLANE

    def cur(col):
        return pl.BlockSpec((rows, V_C), lambda b, n: (b * nchunk + n, col))

    def prev(col):
        return pl.BlockSpec((HALO_C, V_C), lambda b, n: (jnp.maximum((b * nchunk + n) * per8 - 1, 0), col))

    unit = lambda r, cdim: pl.BlockSpec((ups, r, cdim), lambda b, n: (b * nchunk + n, 0, 0))
    return pl.pallas_call(
        functools.partial(_dn_prep_kernel, c=c, lr=lr, ups=ups),
        grid=(nbatch, nchunk),
        in_specs=[cur(qcol), cur(qcol + 1), cur(qcol + 2), prev(qcol), prev(qcol + 1), prev(qcol + 2),
                  pl.BlockSpec((rows, LANE), lambda b, n: (b * nchunk + n, smcol)),
                  pl.BlockSpec((SUBLANE, QKV_C), lambda b, n: (0, 0)),
                  pl.BlockSpec((1, LANE), lambda b, n: (0, 0)),
                  pl.BlockSpec((1, LANE), lambda b, n: (0, 0))],
        out_specs=[unit(DN_STACK, HEAD_DIM_C), unit(DN_STACK, HEAD_DIM_C), unit(DN_STACK, HEAD_DIM_C),
                   unit(HEAD_DIM_C, DN_STACK), unit(DN_STACK, DN_PAIR), unit(DN_GL_ROWS, LANE)],
        out_shape=[jax.ShapeDtypeStruct((nu, DN_STACK, HEAD_DIM_C), F32),
                   jax.ShapeDtypeStruct((nu, DN_STACK, HEAD_DIM_C), BF16),
                   jax.ShapeDtypeStruct((nu, DN_STACK, HEAD_DIM_C), BF16),
                   jax.ShapeDtypeStruct((nu, HEAD_DIM_C, DN_STACK), BF16),
                   jax.ShapeDtypeStruct((nu, DN_STACK, DN_PAIR), BF16),
                   jax.ShapeDtypeStruct((nu, DN_GL_ROWS, LANE), F32)],
        scratch_shapes=[pltpu.VMEM((HALO_C + rows, QKV_C), F32)],
        compiler_params=_cparams(("arbitrary", "arbitrary")),
        name="deltanet_prep",
    )(src, src, src, src, src, src, small_src, wc, alog_row, dtb_row)


def _dn_seq_kernel(u_ref, w_ref, qd_ref, kdt_ref, qk_ref, gl_ref, *rest, c, nsq, streams, carry):
    z_refs = rest[:streams]
    nw_ref, s0_ref, o_ref, sout_ref, sbuf = rest[streams:]
    groups = DN_STACK // c
    hd = HEAD_DIM_C
    span = groups * hd
    step = pl.program_id(0)

    def load_state():
        for j in range(streams):
            for g in range(groups):
                sbuf[j * span + g * hd:j * span + (g + 1) * hd, :] = s0_ref[j, 0, g % nsq, g // nsq]

    def store_state():
        for j in range(streams):
            for g in range(groups):
                sout_ref[j, 0, g % nsq, g // nsq] = sbuf[j * span + g * hd:j * span + (g + 1) * hd, :]

    if carry:
        pl.when(step == 0)(load_state)
    else:
        load_state()

    col_group = lax.broadcasted_iota(jnp.int32, (hd, DN_STACK), 1) // c
    vbs, o_inter = [], []
    for j in range(streams):
        u = u_ref[j, 0]
        w = w_ref[j, 0]
        qd = qd_ref[j, 0]
        v_parts, o_parts = [], []
        for g in range(groups):
            rows = slice(g * c, (g + 1) * c)
            s_b = sbuf[j * span + g * hd:j * span + (g + 1) * hd, :].astype(BF16)
            r = jnp.dot(jnp.concatenate([w[rows], qd[rows]], 0), s_b, preferred_element_type=F32)
            v_parts.append(u[rows] - r[:c])
            o_parts.append(r[c:])
        vbs.append(jnp.concatenate(v_parts, 0).astype(BF16))
        o_inter.append(jnp.concatenate(o_parts, 0))

    o_all = []
    for j in range(streams):
        qk = qk_ref[j, 0]
        o_all.append(o_inter[j] + jnp.concatenate(
            [jnp.dot(qk[p * DN_PAIR:(p + 1) * DN_PAIR], vbs[j][p * DN_PAIR:(p + 1) * DN_PAIR],
                     preferred_element_type=F32) for p in range(DN_STACK // DN_PAIR)], 0))

    for j in range(streams):
        kdt = kdt_ref[j, 0]
        k_big = jnp.concatenate([jnp.where(col_group == g, kdt, jnp.zeros_like(kdt)) for g in range(groups)], 0)
        gl = gl_ref[j, 0]
        scale = jnp.concatenate([jnp.broadcast_to(gl[g:g + 1, :], (hd, LANE)) for g in range(groups)], 0)
        sbuf[j * span:(j + 1) * span, :] = (sbuf[j * span:(j + 1) * span, :] * scale
                                             + jnp.dot(k_big, vbs[j], preferred_element_type=F32))

    for j in range(streams):
        outs = []
        for h in range(N_HEADS_C):
            oh = o_all[j][h * DN_UNIT:(h + 1) * DN_UNIT]
            on = oh * lax.rsqrt(jnp.mean(oh * oh, -1, keepdims=True) + RMS_EPS) * nw_ref[...]
            zh = z_refs[j][:, h * hd:(h + 1) * hd]
            outs.append(on * (zh * _sigmoid(zh)))
        o_ref[j] = jnp.concatenate(outs, -1).astype(o_ref.dtype)

    if carry:
        pl.when(step == pl.num_programs(0) - 1)(store_state)
    else:
        store_state()


def _dn_seq(prep, z_src, nw, s0, *, zcol, streams, steps, c, nsq, carry, layer=None):
    groups = DN_STACK // c
    prep = [t.reshape((streams, steps) + t.shape[1:]) for t in prep]
    unit = lambda r, cdim: pl.BlockSpec((streams, 1, r, cdim), lambda i: (0, i, 0, 0))
    sdim = (streams, 1, nsq, N_HEADS_C, HEAD_DIM_C, HEAD_DIM_C)
    state = pl.BlockSpec(sdim, (lambda i: (0, 0, 0, 0, 0, 0)) if carry else (lambda i: (0, i, 0, 0, 0, 0)))
    state_in, state_shape = state, s0.shape
    if layer is not None:
        state_in = pl.BlockSpec((None,) + sdim, lambda i: (layer, 0, 0 if carry else i, 0, 0, 0, 0))
        state_shape = s0.shape[1:]
    z_specs = [pl.BlockSpec((DN_UNIT, V_C), functools.partial(lambda i, j: (j * steps + i, zcol), j=j))
               for j in range(streams)]
    o, s_out = pl.pallas_call(
        functools.partial(_dn_seq_kernel, c=c, nsq=nsq, streams=streams, carry=carry),
        grid=(steps,),
        in_specs=[unit(DN_STACK, HEAD_DIM_C), unit(DN_STACK, HEAD_DIM_C), unit(DN_STACK, HEAD_DIM_C),
                  unit(HEAD_DIM_C, DN_STACK), unit(DN_STACK, DN_PAIR), unit(DN_GL_ROWS, LANE)]
                 + z_specs + [pl.BlockSpec((1, HEAD_DIM_C), lambda i: (0, 0)), state_in],
        out_specs=[pl.BlockSpec((streams, DN_UNIT, V_C), lambda i: (0, i, 0)), state],
        out_shape=[jax.ShapeDtypeStruct((streams, steps * DN_UNIT, V_C), BF16),
                   jax.ShapeDtypeStruct(state_shape, F32)],
        scratch_shapes=[pltpu.VMEM((streams * groups * HEAD_DIM_C, HEAD_DIM_C), F32)],
        compiler_params=_cparams(("arbitrary",)),
        name="deltanet_seq",
    )(*prep, *([z_src] * streams), nw, s0)
    return o.reshape(streams * steps * DN_UNIT, V_C), s_out


def _merge_kernel(oap_ref, oas_ref, cbp_ref, cbs_ref, ocp_ref, ocs_ref, ga_ref, gb_ref, gc_ref, x_ref,
                  woa_ref, wob_ref, woc_ref, wout_ref, g1_ref, b1_ref, wr_ref, br_ref,
                  x1_ref, route_ref, gate_ref, cnt_ref, carry, *, alpha, prompt_tiles):
    @pl.when(pl.program_id(0) == 0)
    def _():
        carry[...] = jnp.zeros(carry.shape, F32)

    is_prompt = pl.program_id(0) < prompt_tiles
    pick = lambda p_ref, s_ref: jnp.where(is_prompt, p_ref[...], s_ref[...])
    br_a = jnp.dot(pick(oap_ref, oas_ref), woa_ref[...], preferred_element_type=F32)
    br_b = jnp.dot(pick(cbp_ref, cbs_ref), wob_ref[...], preferred_element_type=F32)
    br_c = jnp.dot(pick(ocp_ref, ocs_ref), woc_ref[...], preferred_element_type=F32)
    mixin = _sigmoid(ga_ref[...]) * br_a + _sigmoid(gb_ref[...]) * br_b + _sigmoid(gc_ref[...]) * br_c
    mix = jnp.dot(mixin.astype(BF16), wout_ref[...], preferred_element_type=F32)
    x1 = _layer_norm(alpha * x_ref[...] + mix, g1_ref[...], b1_ref[...])
    x1_ref[...] = x1
    logits = jnp.dot(x1.astype(BF16), wr_ref[...], preferred_element_type=F32) + br_ref[...]
    lane = lax.broadcasted_iota(jnp.int32, logits.shape, 1)
    logits = jnp.where(lane < N_EXPERTS, logits, -jnp.inf)
    route = jnp.zeros(logits.shape, jnp.int32)
    val_out = jnp.zeros(logits.shape, F32)
    member = jnp.zeros(logits.shape, F32)
    sels = []
    top0 = None
    den = None
    for k in range(TOP_K):
        m = jnp.max(logits, -1, keepdims=True)
        sel = jnp.min(jnp.where(logits == m, lane, LANE), -1, keepdims=True)
        if k == 0:
            top0 = m
            e = jnp.ones_like(m)
            den = e
        else:
            e = jnp.exp(m - top0)
            den = den + e
        hit = lane == sel
        route = jnp.where(lane == k, sel, route)
        val_out = jnp.where(lane == k, e, val_out)
        member = jnp.where(hit, 1.0, member)
        logits = jnp.where(hit, -jnp.inf, logits)
        sels.append(sel)
    gate_ref[...] = val_out / den
    tm = logits.shape[0]
    earlier = lax.broadcasted_iota(jnp.int32, (tm, tm), 0) > lax.broadcasted_iota(jnp.int32, (tm, tm), 1)
    cum = jnp.dot(jnp.where(earlier, 1.0, 0.0).astype(BF16), member.astype(BF16),
                  preferred_element_type=F32) + carry[...]
    for k in range(TOP_K):
        rank = jnp.sum(jnp.where(lane == sels[k], cum, 0.0), -1, keepdims=True)
        route = jnp.where(lane == TOP_K + k, rank.astype(jnp.int32), route)
    route_ref[...] = route
    carry[...] = carry[...] + jnp.sum(member, axis=0, keepdims=True)
    cnt_ref[...] = carry[...]


def _merge(branches, h, x, lw, alpha):
    m = x.shape[0]
    tp = branches[0][0].shape[0]
    tm = _tile(math.gcd(tp, m - tp), 512)
    np_tiles = tp // tm
    row = lambda i: (i, 0)
    full = lambda i: (0, 0)
    gcol = COL_GATE // D_MODEL
    p_spec = pl.BlockSpec((tm, Q_A), lambda i: (jnp.minimum(i, np_tiles - 1), 0))
    s_spec = pl.BlockSpec((tm, Q_A), lambda i: (jnp.maximum(i - np_tiles, 0), 0))
    return pl.pallas_call(
        functools.partial(_merge_kernel, alpha=alpha, prompt_tiles=np_tiles),
        grid=(m // tm,),
        in_specs=[p_spec, s_spec, p_spec, s_spec, p_spec, s_spec,
                  pl.BlockSpec((tm, D_MODEL), lambda i: (i, gcol)),
                  pl.BlockSpec((tm, D_MODEL), lambda i: (i, gcol + 1)),
                  pl.BlockSpec((tm, D_MODEL), lambda i: (i, gcol + 2)),
                  pl.BlockSpec((tm, D_MODEL), row),
                  pl.BlockSpec((Q_A, D_MODEL), full), pl.BlockSpec((C_B, D_MODEL), full),
                  pl.BlockSpec((V_C, D_MODEL), full), pl.BlockSpec((D_MODEL, D_MODEL), full),
                  pl.BlockSpec((1, D_MODEL), full), pl.BlockSpec((1, D_MODEL), full),
                  pl.BlockSpec((D_MODEL, LANE), full), pl.BlockSpec((1, LANE), full)],
        out_specs=[pl.BlockSpec((tm, D_MODEL), row), pl.BlockSpec((tm, LANE), row),
                   pl.BlockSpec((tm, LANE), row), pl.BlockSpec((1, LANE), full)],
        out_shape=[jax.ShapeDtypeStruct((m, D_MODEL), F32), jax.ShapeDtypeStruct((m, LANE), jnp.int32),
                   jax.ShapeDtypeStruct((m, LANE), F32), jax.ShapeDtypeStruct((1, LANE), F32)],
        scratch_shapes=[pltpu.VMEM((1, LANE), F32)],
        compiler_params=_cparams(("arbitrary",)),
        name="merge_router",
    )(*branches[0], *branches[1], *branches[2], h, h, h, x, lw["w_o_a"], lw["w_o_b"], lw["w_o_c"], lw["w_out"],
      lw["ln1_g"], lw["ln1_b"], lw["w_router"], lw["b_router"])


def _row_copy(src, src_row, dst, dst_row, sem):
    return pltpu.make_async_copy(src.at[pl.ds(src_row, 1)], dst.at[pl.ds(dst_row, 1)], sem)


def _dispatch_kernel(dest_ref, last_ref, x_ref, xs_ref, zbuf, sem, *, tt, n_blk):
    @pl.when(pl.program_id(0) == 0)
    def _():
        zbuf[...] = jnp.zeros(zbuf.shape, F32)

        def zero_block(first_row):
            return pltpu.make_async_copy(zbuf, xs_ref.at[pl.ds(pl.multiple_of(first_row, MOE_ROWS), MOE_ROWS)], sem)

        def start_tail(b, carry):
            zero_block(b * MOE_ROWS).start()
            return carry

        def wait_tail(b, carry):
            zero_block(0).wait()
            return carry

        n_active = last_ref[N_EXPERTS]
        for e in range(N_EXPERTS):
            zero_block(last_ref[e]).start()
        lax.fori_loop(n_active, n_blk, start_tail, 0)
        for e in range(N_EXPERTS):
            zero_block(0).wait()
        lax.fori_loop(n_active, n_blk, wait_tail, 0)

    def body(t, carry):
        for k in range(TOP_K):
            _row_copy(x_ref, t, xs_ref, dest_ref[t * TOP_K + k], sem).start(priority=k % 2)
        return carry

    lax.fori_loop(0, tt, body, 0, unroll=4)
    for k in range(TOP_K):
        pltpu.make_async_copy(x_ref, xs_ref.at[pl.ds(0, tt)], sem).wait()


def _dispatch(dest, last_blk, x1, n_rows):
    m = x1.shape[0]
    tt = _tile(m, 512)
    return pl.pallas_call(
        functools.partial(_dispatch_kernel, tt=tt, n_blk=n_rows // MOE_ROWS),
        grid=(m // tt,),
        in_specs=[pl.BlockSpec((tt * TOP_K,), lambda i: (i,), memory_space=pltpu.SMEM),
                  pl.BlockSpec(memory_space=pltpu.SMEM),
                  pl.BlockSpec((tt, D_MODEL), lambda i: (i, 0))],
        out_specs=pl.BlockSpec(memory_space=pl.ANY),
        out_shape=jax.ShapeDtypeStruct((n_rows, D_MODEL), F32),
        scratch_shapes=[pltpu.VMEM((MOE_ROWS, D_MODEL), F32), pltpu.SemaphoreType.DMA(())],
        compiler_params=pltpu.CompilerParams(dimension_semantics=("arbitrary",), vmem_limit_bytes=VMEM_LIMIT,
                                             disable_bounds_checks=True),
        name="moe_dispatch",
    )(dest, last_blk, x1)


def _moe_kernel(be_ref, na_ref, x_ref, wu_ref, bu_ref, wd_ref, bd_ref, o_ref, wu_b, wd_b):
    i = pl.program_id(0)
    active = i < na_ref[0]
    new_expert = jnp.logical_or(i == 0, be_ref[i] != be_ref[jnp.maximum(i - 1, 0)])

    @pl.when(jnp.logical_and(active, new_expert))
    def _():
        wu_b[...] = wu_ref[0, 0].astype(BF16)
        wd_b[...] = wd_ref[0, 0].astype(BF16)

    @pl.when(active)
    def _():
        hu = jnp.dot(x_ref[...].astype(BF16), wu_b[...], preferred_element_type=F32) + bu_ref[0, 0]
        gt = jnp.minimum(hu[:, :D_FF], SWIGLU_LIMIT)
        up = jnp.clip(hu[:, D_FF:], -SWIGLU_LIMIT, SWIGLU_LIMIT)
        act = (up + 1.0) * gt * _sigmoid(SWIGLU_ALPHA * gt)
        o_ref[...] = jnp.dot(act.astype(BF16), wd_b[...], preferred_element_type=F32) + bd_ref[0, 0]

    @pl.when(jnp.logical_not(active))
    def _():
        o_ref[...] = jnp.zeros(o_ref.shape, F32)


def _moe(xs, blk_e, n_active, w_up, b_up, w_down, b_down, layer):
    nblk = xs.shape[0] // MOE_ROWS

    def blk(i, na):
        return jnp.minimum(i, jnp.maximum(na[0] - 1, 0))

    weights = lambda i, be, na: (layer, be[blk(i, na)], 0, 0)
    grid_spec = pltpu.PrefetchScalarGridSpec(
        num_scalar_prefetch=2,
        grid=(nblk,),
        in_specs=[pl.BlockSpec((MOE_ROWS, D_MODEL), lambda i, be, na: (blk(i, na), 0)),
                  pl.BlockSpec((1, 1, D_MODEL, 2 * D_FF), weights),
                  pl.BlockSpec((1, 1, 1, 2 * D_FF), weights),
                  pl.BlockSpec((1, 1, D_FF, D_MODEL), weights),
                  pl.BlockSpec((1, 1, 1, D_MODEL), weights)],
        out_specs=pl.BlockSpec((MOE_ROWS, D_MODEL), lambda i, be, na: (i, 0)),
        scratch_shapes=[pltpu.VMEM((D_MODEL, 2 * D_FF), BF16), pltpu.VMEM((D_FF, D_MODEL), BF16)],
    )
    return pl.pallas_call(
        _moe_kernel,
        grid_spec=grid_spec,
        out_shape=jax.ShapeDtypeStruct((nblk * MOE_ROWS, D_MODEL), F32),
        compiler_params=_cparams(("arbitrary",)),
        name="moe_experts",
    )(blk_e, n_active, xs, w_up, b_up, w_down, b_down)

def _post_kernel(dest_ref, yb_ref, gate_ref, x1_ref, p_ref, wg_ref, wp_ref, g2_ref, b2_ref, x_ref, xb_ref,
                 ybuf, sem, *, alpha, tm):
    def body(t, carry):
        for k in range(TOP_K):
            _row_copy(yb_ref, dest_ref[t * TOP_K + k], ybuf.at[k], t, sem).start(priority=k % 2)
        return carry

    lax.fori_loop(0, tm, body, 0, unroll=4)
    for k in range(TOP_K):
        pltpu.make_async_copy(yb_ref.at[pl.ds(0, tm)], ybuf.at[k], sem).wait()

    gate = gate_ref[...]
    ffn = gate[:, 0:1] * ybuf[0]
    for k in range(1, TOP_K):
        ffn = ffn + gate[:, k:k + 1] * ybuf[k]
    x2 = _layer_norm(alpha * x1_ref[...] + ffn, g2_ref[...], b2_ref[...])
    gl = jnp.dot(x2.astype(BF16), wg_ref[...], preferred_element_type=F32)
    pe = jnp.dot(p_ref[...].astype(BF16), wp_ref[...], preferred_element_type=F32)
    xo = x2 + _sigmoid(gl) * pe
    x_ref[...] = xo
    xb_ref[...] = xo.astype(BF16)


def _post(dest, yb, gate, x1, p, lw, alpha):
    m = x1.shape[0]
    tm = _tile(m, 512)
    row = lambda i: (i, 0)
    full = lambda i: (0, 0)
    return pl.pallas_call(
        functools.partial(_post_kernel, alpha=alpha, tm=tm),
        grid=(m // tm,),
        in_specs=[pl.BlockSpec((tm * TOP_K,), lambda i: (i,), memory_space=pltpu.SMEM),
                  pl.BlockSpec(memory_space=pl.ANY),
                  pl.BlockSpec((tm, LANE), row),
                  pl.BlockSpec((tm, D_MODEL), row), pl.BlockSpec((tm, PLE_DIM), row),
                  pl.BlockSpec((D_MODEL, D_MODEL), full), pl.BlockSpec((PLE_DIM, D_MODEL), full),
                  pl.BlockSpec((1, D_MODEL), full), pl.BlockSpec((1, D_MODEL), full)],
        out_specs=[pl.BlockSpec((tm, D_MODEL), row), pl.BlockSpec((tm, D_MODEL), row)],
        out_shape=[jax.ShapeDtypeStruct((m, D_MODEL), F32), jax.ShapeDtypeStruct((m, D_MODEL), BF16)],
        scratch_shapes=[pltpu.VMEM((TOP_K, tm, D_MODEL), F32), pltpu.SemaphoreType.DMA(())],
        compiler_params=pltpu.CompilerParams(dimension_semantics=("arbitrary",), vmem_limit_bytes=VMEM_LIMIT,
                                             disable_bounds_checks=True),
        name="combine_ple",
    )(dest, yb, gate, x1, p, lw["w_ple_gate"], lw["w_ple_proj"], lw["ln2_g"], lw["ln2_b"])


def _post_next_kernel(dfirst_ref, dnext_ref, yb_ref, gate_ref, x1_ref, p_ref, wg_ref, wp_ref, g2_ref, b2_ref,
                      win_ref, bin_ref, x_ref, h_ref, ybuf, sem, *, alpha, tm, n_tiles, tn):
    i = pl.program_id(0)
    slot = i % 2

    def issue(dest_ref, to_slot):
        def body(t, carry):
            for k in range(TOP_K):
                _row_copy(yb_ref, dest_ref[t * TOP_K + k], ybuf.at[to_slot, k], t,
                          sem.at[to_slot]).start(priority=k % 2)
            return carry

        lax.fori_loop(0, tm, body, 0, unroll=4)

    @pl.when(i == 0)
    def _():
        issue(dfirst_ref, 0)

    @pl.when(i + 1 < n_tiles)
    def _():
        issue(dnext_ref, 1 - slot)

    for k in range(TOP_K):
        pltpu.make_async_copy(yb_ref.at[pl.ds(0, tm)], ybuf.at[slot, k], sem.at[slot]).wait()

    gate = gate_ref[...]
    ffn = gate[:, 0:1] * ybuf[slot, 0]
    for k in range(1, TOP_K):
        ffn = ffn + gate[:, k:k + 1] * ybuf[slot, k]
    x2 = _layer_norm(alpha * x1_ref[...] + ffn, g2_ref[...], b2_ref[...])
    gl = jnp.dot(x2.astype(BF16), wg_ref[...], preferred_element_type=F32)
    pe = jnp.dot(p_ref[...].astype(BF16), wp_ref[...], preferred_element_type=F32)
    xo = x2 + _sigmoid(gl) * pe
    x_ref[...] = xo
    xb = xo.astype(BF16)
    for j in range(N_IN_PAD // tn):
        cs = slice(j * tn, (j + 1) * tn)
        h_ref[:, cs] = jnp.dot(xb, win_ref[:, cs], preferred_element_type=F32) + bin_ref[:, cs]


def _post_next(dest, yb, gate, x1, p, lw, alpha, w_in_next, b_in_next):
    m = x1.shape[0]
    tm = _tile(m, 256)
    n_tiles = m // tm
    tn = _tile(N_IN_PAD, 1408, LANE)
    row = lambda i: (i, 0)
    full = lambda i: (0, 0)
    once = pl.Buffered(1)
    return pl.pallas_call(
        functools.partial(_post_next_kernel, alpha=alpha, tm=tm, n_tiles=n_tiles, tn=tn),
        grid=(n_tiles,),
        in_specs=[pl.BlockSpec((tm * TOP_K,), lambda i: (0,), memory_space=pltpu.SMEM),
                  pl.BlockSpec((tm * TOP_K,), lambda i: (jnp.minimum(i + 1, n_tiles - 1),), memory_space=pltpu.SMEM),
                  pl.BlockSpec(memory_space=pl.ANY),
                  pl.BlockSpec((tm, LANE), row),
                  pl.BlockSpec((tm, D_MODEL), row), pl.BlockSpec((tm, PLE_DIM), row),
                  pl.BlockSpec((D_MODEL, D_MODEL), full, pipeline_mode=once),
                  pl.BlockSpec((PLE_DIM, D_MODEL), full, pipeline_mode=once),
                  pl.BlockSpec((1, D_MODEL), full), pl.BlockSpec((1, D_MODEL), full),
                  pl.BlockSpec((D_MODEL, N_IN_PAD), full, pipeline_mode=once),
                  pl.BlockSpec((1, N_IN_PAD), full)],
        out_specs=[pl.BlockSpec((tm, D_MODEL), row), pl.BlockSpec((tm, N_IN_PAD), row)],
        out_shape=[jax.ShapeDtypeStruct((m, D_MODEL), F32), jax.ShapeDtypeStruct((m, N_IN_PAD), F32)],
        scratch_shapes=[pltpu.VMEM((2, TOP_K, tm, D_MODEL), F32), pltpu.SemaphoreType.DMA((2,))],
        compiler_params=pltpu.CompilerParams(dimension_semantics=("arbitrary",), vmem_limit_bytes=VMEM_LIMIT,
                                             disable_bounds_checks=True),
        name="combine_ple_in_proj",
    )(dest, dest, yb, gate, x1, p, lw["w_ple_gate"], lw["w_ple_proj"], lw["ln2_g"], lw["ln2_b"],
      w_in_next, b_in_next)


def _rel_bucket(n):
    max_exact = NUM_BUCKETS // 2
    nf = jnp.maximum(n, 1).astype(F32)
    large = max_exact + (jnp.log(nf / max_exact) / math.log(MAX_DISTANCE / max_exact)
                         * (NUM_BUCKETS - max_exact)).astype(jnp.int32)
    return jnp.where(n < max_exact, jnp.maximum(n, 0), jnp.minimum(large, NUM_BUCKETS - 1))


def _bias_table(rel_bias, dist, valid):
    hit = _rel_bucket(dist)[..., None] == jnp.arange(NUM_BUCKETS)
    b = jnp.sum(jnp.where(hit[..., None], rel_bias.astype(F32), 0.0), axis=-2)
    b = jnp.where(valid[..., None], b, NEG_INF)
    return jnp.moveaxis(b, -1, 0)


def _permute_in(w):
    o = np.concatenate([[0], np.cumsum(IN_SIZES)])
    seg = lambda i: w[..., o[i]:o[i + 1]]
    small = jnp.concatenate([seg(6), seg(7)], -1)
    small = jnp.pad(small, [(0, 0)] * (w.ndim - 1) + [(0, LANE - 2 * N_HEADS_C)])
    return jnp.concatenate([seg(8), seg(3), seg(4), seg(5), seg(0), seg(1), seg(2), small], -1)


def _route(route, counts, n_tok):
    n_blk = -(-n_tok * TOP_K // MOE_ROWS) + N_EXPERTS
    cnt = counts[0, :N_EXPERTS].astype(jnp.int32)
    padded = (cnt + MOE_ROWS - 1) // MOE_ROWS * MOE_ROWS
    pad_end = jnp.cumsum(padded)
    pad_start = pad_end - padded
    idx = route[:, 0:TOP_K]
    rank = route[:, TOP_K:2 * TOP_K]
    start_of = jnp.sum(jnp.where(idx[:, :, None] == jnp.arange(N_EXPERTS)[None, None, :],
                                 pad_start[None, None, :], 0), -1)
    dest = (start_of + rank).reshape(n_tok * TOP_K).astype(jnp.int32)
    blk_first = (jnp.arange(n_blk) * MOE_ROWS)[:, None]
    blk_e = jnp.minimum(jnp.sum((pad_end[None, :] <= blk_first).astype(jnp.int32), -1), N_EXPERTS - 1)
    n_active = (pad_end[-1:] // MOE_ROWS).astype(jnp.int32)
    last_blk = jnp.concatenate([jnp.maximum(pad_end - MOE_ROWS, 0).astype(jnp.int32), n_active])
    return dest, last_blk, blk_e.astype(jnp.int32), n_active, n_blk


def kernel(x_prompt, x_sample, cache_k_a, cache_v_a, state_conv_b, state_conv_c, state_s_c, p_prompt, p_sample,
           rel_bias, w_in, b_in, sink_a, w_o_a, w_dw_b, b_dw_b, ln_b_g, ln_b_b, w_o_b, w_conv_c, a_log_c,
           dt_bias_c, norm_c, w_o_c, w_out, ln1_g, ln1_b, w_router, b_router, w_up, b_up, w_down, b_down,
           ln2_g, ln2_b, w_ple_gate, w_ple_proj):
    depth = w_in.shape[0]
    nb_p, seq, _ = x_prompt.shape
    ns, ls, _ = x_sample.shape
    w_buf = cache_k_a.shape[2]
    alpha = (2 * depth) ** 0.25
    tp = nb_p * seq
    tsm = ns * ls
    n_tok = tp + tsm
    assert seq % DN_CHUNK == 0 and seq % Q_BLOCK == 0 and w_buf == WINDOW and ls <= 16

    w_in_p = _permute_in(w_in).astype(BF16)
    b_in_p = _permute_in(b_in)[:, None, :]
    w_o_a_b, w_o_b_b, w_o_c_b, w_out_b = (t.astype(BF16) for t in (w_o_a, w_o_b, w_o_c, w_out))
    w_ple_gate_b, w_ple_proj_b = w_ple_gate.astype(BF16), w_ple_proj.astype(BF16)
    b_up4, b_down4 = b_up[:, :, None, :], b_down[:, :, None, :]
    w_router_p = jnp.pad(w_router, ((0, 0), (0, 0), (0, LANE - N_EXPERTS))).astype(BF16)
    b_router_p = jnp.pad(b_router, ((0, 0), (0, LANE - N_EXPERTS)))[:, None, :]
    w_dw_r = _round_bf16(w_dw_b)
    w_dw_p = jnp.pad(w_dw_r, ((0, 0), (0, HALO_B - CONV_B), (0, 0)))
    w_cc_p = jnp.pad(_round_bf16(w_conv_c), ((0, 0), (0, SUBLANE - CONV_C), (0, 0)))
    lane_pad = lambda t: jnp.pad(t, ((0, 0), (N_HEADS_C, LANE - 2 * N_HEADS_C)))[:, None, :]
    alog_rows, dtb_rows = lane_pad(a_log_c), lane_pad(dt_bias_c)

    qi = jnp.arange(Q_BLOCK)[:, None]
    kj = jnp.arange(2 * Q_BLOCK)[None, :]
    dist = qi - kj + Q_BLOCK
    valid = (dist >= 0) & (dist < WINDOW)
    tab_rest = _bias_table(rel_bias, dist, valid)
    tab_first = _bias_table(rel_bias, dist, valid & (kj >= Q_BLOCK))
    bias_prompt = jnp.stack([tab_first, tab_rest])

    kwin = 2 * WINDOW
    ti = jnp.arange(ls)[:, None]
    kjs = jnp.arange(kwin)[None, :]
    dist_s = w_buf + ti - kjs
    valid_s = (dist_s >= 0) & (dist_s < WINDOW) & (kjs < w_buf + ls)
    tab_s = _bias_table(rel_bias, dist_s, valid_s)
    tab_s = tab_s.reshape(N_KV_A, GROUP_A * ls, kwin)
    gb = 2 * _tile(ns, 16, 1)
    bias_s = jnp.tile(tab_s, (gb // N_KV_A, 1, 1))

    wrows = -(-(CONV_B - 1 + ls) // SUBLANE) * SUBLANE
    jj = jnp.arange(wrows)[None, :] - jnp.arange(ls)[:, None]
    tap_ok = (jj >= 0) & (jj < CONV_B)

    x_all = jnp.concatenate([x_prompt.reshape(tp, D_MODEL), x_sample.reshape(tsm, D_MODEL)], 0)
    xb_all = x_all.astype(BF16)
    p_all = jnp.concatenate([p_prompt.reshape(depth, tp, PLE_DIM), p_sample.reshape(depth, tsm, PLE_DIM)], 1)
    zero_s = jnp.zeros((nb_p, 1, 1, N_HEADS_C, HEAD_DIM_C, HEAD_DIM_C), F32)
    c_s = 16
    nsq_s = DN_UNIT // c_s
    assert ls + CONV_C - 1 <= c_s and ns % nsq_s == 0

    st_p, st_s = [], []
    h = _in_proj(xb_all, w_in_p[0], b_in_p[0])
    for i in range(depth):
        lw = {"w_o_a": w_o_a_b[i], "w_o_b": w_o_b_b[i], "w_o_c": w_o_c_b[i], "w_out": w_out_b[i],
              "ln1_g": ln1_g[i][None], "ln1_b": ln1_b[i][None], "w_router": w_router_p[i],
              "b_router": b_router_p[i], "w_ple_gate": w_ple_gate_b[i], "w_ple_proj": w_ple_proj_b[i],
              "ln2_g": ln2_g[i][None], "ln2_b": ln2_b[i][None]}
        hs = h[tp:].reshape(ns, ls, N_IN_PAD)

        sinks = sink_a[i].astype(F32)
        oa_p = _attn_prompt(h, sinks, bias_prompt, nb_p, seq)
        q_s = hs[:, :, COL_QA:COL_QA + Q_A].reshape(ns, ls, N_KV_A, GROUP_A, HEAD_DIM_A)
        q_s = q_s.transpose(0, 2, 3, 1, 4).reshape(ns * N_KV_A, GROUP_A * ls, HEAD_DIM_A)
        k_new = hs[:, :, COL_KA:COL_KA + KV_A].reshape(ns, ls, N_KV_A, HEAD_DIM_A)
        v_new = hs[:, :, COL_VA:COL_VA + KV_A].reshape(ns, ls, N_KV_A, HEAD_DIM_A)
        k_all = jnp.concatenate([cache_k_a[i], k_new], 1)
        v_all = jnp.concatenate([cache_v_a[i], v_new], 1)

        def keywin(t):
            t = jnp.pad(t, ((0, 0), (0, kwin - w_buf - ls), (0, 0), (0, 0)))
            return t.transpose(0, 2, 1, 3).reshape(ns * N_KV_A, kwin, HEAD_DIM_A)

        sink_s = jnp.tile(jnp.repeat(sinks.reshape(N_KV_A, GROUP_A), ls, axis=1)[:, :, None], (gb // N_KV_A, 1, 1))
        oa_s = _attn_sample(q_s, keywin(k_all), keywin(v_all), bias_s, sink_s)
        oa_s = oa_s.reshape(ns, N_KV_A, GROUP_A, ls, HEAD_DIM_A).transpose(0, 3, 1, 2, 4).reshape(tsm, Q_A)
        oa = (oa_p, oa_s.astype(BF16))

        cb_p, tail_p = _convb_prompt(h, w_dw_p[i], b_dw_b[i][None], ln_b_g[i][None], ln_b_b[i][None], nb_p, seq)
        wt = jnp.where(tap_ok[:, :, None], w_dw_r[i][jnp.clip(jj, 0, CONV_B - 1)], 0.0)
        cb_s, nconv_b_s = _convb_sample(hs, state_conv_b[i], wt, b_dw_b[i][None], ln_b_g[i][None], ln_b_b[i][None])
        cb = (cb_p, cb_s.transpose(1, 0, 2).reshape(tsm, C_B))

        nw = norm_c[i][None]
        nchunk = seq // DN_CHUNK
        prep_p = _dn_prep(h, h, w_cc_p[i], alog_rows[i], dtb_rows[i], qcol=COL_QKVC // V_C,
                          smcol=COL_SMALL // LANE, nbatch=nb_p, nchunk=nchunk, c=DN_CHUNK, lr=DN_CHUNK)
        oc_p, s_p = _dn_seq(prep_p, h, nw, zero_s, zcol=COL_Z // V_C, streams=nb_p, steps=nchunk,
                            c=DN_CHUNK, nsq=1, carry=True)
        s_p = s_p.reshape(nb_p, N_HEADS_C, HEAD_DIM_C, HEAD_DIM_C)
        lead = c_s - ls
        hist = jnp.concatenate([jnp.zeros((ns, lead - (CONV_C - 1), QKV_C), F32), state_conv_c[i],
                                hs[:, :, COL_QKVC:COL_QKVC + QKV_C]], 1).reshape(ns * c_s, QKV_C)
        pad_s = lambda col, wid: jnp.pad(hs[:, :, col:col + wid], ((0, 0), (lead, 0), (0, 0))).reshape(ns * c_s, wid)
        prep_s = _dn_prep(hist, pad_s(COL_SMALL, LANE), w_cc_p[i], alog_rows[i], dtb_rows[i], qcol=0, smcol=0,
                          nbatch=ns // nsq_s, nchunk=1, c=c_s, lr=ls)
        units_s = ns // nsq_s
        streams_s = 2 if units_s % 2 == 0 else 1
        s0_s = state_s_c.astype(F32).reshape(depth, streams_s, units_s // streams_s, nsq_s, N_HEADS_C, HEAD_DIM_C,
                                             HEAD_DIM_C)
        oc_s, s_s = _dn_seq(prep_s, pad_s(COL_Z, V_C), nw, s0_s, zcol=0, streams=streams_s,
                            steps=units_s // streams_s, c=c_s, nsq=nsq_s, carry=False, layer=i)
        s_s = s_s.reshape(ns, N_HEADS_C, HEAD_DIM_C, HEAD_DIM_C)
        oc = (oc_p, oc_s.reshape(ns, c_s, V_C)[:, lead:].reshape(tsm, V_C))

        x1, route, gate, counts = _merge((oa, cb, oc), h, x_all, lw, alpha)

        dest, last_blk, blk_e, n_active, n_blk = _route(route, counts, n_tok)
        xs = _dispatch(dest, last_blk, x1, n_blk * MOE_ROWS)
        yb = _moe(xs, blk_e, n_active, w_up, b_up4, w_down, b_down4, i)

        if i + 1 < depth:
            x_all, h_next = _post_next(dest, yb, gate, x1, p_all[i], lw, alpha, w_in_p[i + 1], b_in_p[i + 1])
        else:
            x_all, _ = _post(dest, yb, gate, x1, p_all[i], lw, alpha)
            h_next = None

        def tail_rows(nrows, col, width):
            return jnp.stack([h[(b + 1) * seq - nrows:(b + 1) * seq, col:col + width] for b in range(nb_p)])

        kp_new = tail_rows(w_buf, COL_KA, KV_A)
        vp_new = tail_rows(w_buf, COL_VA, KV_A)
        ccp_new = tail_rows(CONV_C - 1, COL_QKVC, QKV_C)
        st_p.append((kp_new.reshape(nb_p, w_buf, N_KV_A, HEAD_DIM_A), vp_new.reshape(nb_p, w_buf, N_KV_A, HEAD_DIM_A),
                     tail_p[:, HALO_B - (CONV_B - 1):], ccp_new, s_p))
        ccs_ext = jnp.concatenate([state_conv_c[i], hs[:, :, COL_QKVC:COL_QKVC + QKV_C]], 1)
        st_s.append((k_all[:, ls:], v_all[:, ls:], nconv_b_s, ccs_ext[:, -(CONV_C - 1):], s_s))
        h = h_next

    yp = x_all[:tp].reshape(nb_p, seq, D_MODEL)
    ys = x_all[tp:].reshape(ns, ls, D_MODEL)
    kp, vp, cbp, ccp, scp = [jnp.stack(z) for z in zip(*st_p)]
    ks_, vs_, cbs, ccs, scs = [jnp.stack(z) for z in zip(*st_s)]
    return (yp, ys, kp, vp, cbp, ccp, scp, ks_, vs_, cbs, ccs, scs)
```

```python
import functools
import math

import numpy as np
import jax
import jax.numpy as jnp
from jax import lax
from jax.experimental import pallas as pl
from jax.experimental.pallas import tpu as pltpu

F32 = jnp.float32
BF16 = jnp.bfloat16
HIGHEST = lax.Precision.HIGHEST

D_MODEL = 1024
N_HEADS_A, N_KV_A, HEAD_DIM_A = 8, 2, 64
GROUP_A = N_HEADS_A // N_KV_A
Q_A = N_HEADS_A * HEAD_DIM_A
KV_A = N_KV_A * HEAD_DIM_A
WINDOW = 128
Q_BLOCK = 128
NUM_BUCKETS = 32
MAX_DISTANCE = 128
NEG_INF = -1e30
C_B = 512
CONV_B = 31
N_HEADS_C, HEAD_DIM_C = 4, 128
V_C = N_HEADS_C * HEAD_DIM_C
QKV_C = 3 * V_C
CONV_C = 4
DN_CHUNK = 64
N_BRANCH = 3
IN_SIZES = (Q_A, KV_A, KV_A, 2 * C_B, QKV_C, V_C, N_HEADS_C, N_HEADS_C, N_BRANCH * D_MODEL)
N_EXPERTS = 32
TOP_K = 4
D_FF = 1024
SWIGLU_LIMIT = 7.0
SWIGLU_ALPHA = 1.702
PLE_DIM = 256
LN_EPS = 1e-5
RMS_EPS = 1e-6

LANE = 128
SUBLANE = 8
VMEM_LIMIT = 56 * 1024 * 1024

COL_GATE = 0
COL_GLU = 3 * D_MODEL
COL_QKVC = COL_GLU + 2 * C_B
COL_Z = COL_QKVC + QKV_C
COL_QA = COL_Z + V_C
COL_KA = COL_QA + Q_A
COL_VA = COL_KA + KV_A
COL_SMALL = COL_VA + KV_A
N_IN_PAD = COL_SMALL + LANE
MOE_ROWS = 512
HALO_B = 32
HALO_C = 8


def _tile(n, target, mult=SUBLANE):
    best = None
    for t in range(mult, min(n, target) + 1, mult):
        if n % t == 0:
            best = t
    return best if best is not None else n


def _cparams(sem):
    return pltpu.CompilerParams(dimension_semantics=sem, vmem_limit_bytes=VMEM_LIMIT)


def _layer_norm(x, g, b):
    mu = jnp.mean(x, -1, keepdims=True)
    xc = x - mu
    var = jnp.mean(xc * xc, -1, keepdims=True)
    return xc * lax.rsqrt(var + LN_EPS) * g + b


def _sigmoid(x):
    return 1.0 / (1.0 + jnp.exp(-x))


def _round_bf16(x):
    return x.astype(BF16).astype(F32)


def _mm_bias_kernel(x_ref, w_ref, b_ref, o_ref):
    o_ref[...] = jnp.dot(x_ref[...], w_ref[...], preferred_element_type=F32) + b_ref[...]


def _in_proj(x_bf, w_bf, b):
    m, k = x_bf.shape
    n = w_bf.shape[1]
    tm = _tile(m, 1536)
    tn = _tile(n, 1408, LANE)
    return pl.pallas_call(
        _mm_bias_kernel,
        grid=(n // tn, m // tm),
        in_specs=[pl.BlockSpec((tm, k), lambda j, i: (i, 0)),
                  pl.BlockSpec((k, tn), lambda j, i: (0, j)),
                  pl.BlockSpec((1, tn), lambda j, i: (0, j))],
        out_specs=pl.BlockSpec((tm, tn), lambda j, i: (i, j)),
        out_shape=jax.ShapeDtypeStruct((m, n), F32),
        compiler_params=_cparams(("arbitrary", "arbitrary")),
        name="in_proj",
    )(x_bf, w_bf, b)


def _attn_prompt_kernel(sink_ref, q_ref, kp_ref, kc_ref, vp_ref, vc_ref, bias_ref, o_ref):
    nt = (((1,), (1,)), ((), ()))
    q = (q_ref[...] * (HEAD_DIM_A ** -0.5)).astype(BF16)
    lo = lax.broadcasted_iota(jnp.int32, (2 * Q_BLOCK, KV_A), 1) < HEAD_DIM_A

    def placed(prev_ref, cur_ref):
        band = jnp.concatenate([prev_ref[...], cur_ref[...]], 0)
        own = (jnp.where(lo, band, 0.0), jnp.where(lo, 0.0, band))
        moved = (pltpu.roll(own[0], HEAD_DIM_A, 1), pltpu.roll(own[1], HEAD_DIM_A, 1))
        return ((own[0].astype(BF16), moved[0].astype(BF16)), (moved[1].astype(BF16), own[1].astype(BF16)))

    kz = placed(kp_ref, kc_ref)
    vz = placed(vp_ref, vc_ref)
    pairs = []
    for kv in range(N_KV_A):
        heads = range(kv * GROUP_A, (kv + 1) * GROUP_A)
        scores = [lax.dot_general(q[:, (h // 2) * LANE:(h // 2 + 1) * LANE], kz[kv][h % 2], nt,
                                  preferred_element_type=F32) + bias_ref[0, h] for h in heads]
        probs = []
        for h, s in zip(heads, scores):
            sink = sink_ref[h]
            m = jnp.maximum(jnp.max(s, -1, keepdims=True), sink)
            e = jnp.exp(s - m)
            inv = 1.0 / (jnp.sum(e, -1, keepdims=True) + jnp.exp(sink - m))
            probs.append((e * inv).astype(BF16))
        pv = [jnp.dot(p, vz[kv][h % 2], preferred_element_type=F32) for h, p in zip(heads, probs)]
        pairs += [pv[0] + pv[1], pv[2] + pv[3]]
    o_ref[...] = jnp.concatenate(pairs, -1).astype(o_ref.dtype)


def _attn_prompt(h, sinks, bias_tab, nbatch, seq):
    nb = seq // Q_BLOCK
    qcol = COL_QA // Q_A
    kcol = COL_KA // KV_A
    vcol = COL_VA // KV_A

    def cur(c):
        return lambda b, i: (b * nb + i, c)

    def prev(c):
        return lambda b, i: (jnp.maximum(b * nb + i - 1, 0), c)

    return pl.pallas_call(
        _attn_prompt_kernel,
        grid=(nbatch, nb),
        in_specs=[pl.BlockSpec(memory_space=pltpu.SMEM),
                  pl.BlockSpec((Q_BLOCK, Q_A), cur(qcol)),
                  pl.BlockSpec((Q_BLOCK, KV_A), prev(kcol)),
                  pl.BlockSpec((Q_BLOCK, KV_A), cur(kcol)),
                  pl.BlockSpec((Q_BLOCK, KV_A), prev(vcol)),
                  pl.BlockSpec((Q_BLOCK, KV_A), cur(vcol)),
                  pl.BlockSpec((1, N_HEADS_A, Q_BLOCK, 2 * Q_BLOCK), lambda b, i: (jnp.minimum(i, 1), 0, 0, 0))],
        out_specs=pl.BlockSpec((Q_BLOCK, Q_A), lambda b, i: (b * nb + i, 0)),
        out_shape=jax.ShapeDtypeStruct((nbatch * seq, Q_A), BF16),
        compiler_params=_cparams(("arbitrary", "arbitrary")),
        name="attn_prompt",
    )(sinks, h, h, h, h, h, bias_tab)


def _attn_sample_kernel(q_ref, k_ref, v_ref, bias_ref, sink_ref, o_ref):
    q = (q_ref[...] * (HEAD_DIM_A ** -0.5)).astype(BF16)
    s = jnp.einsum("bqd,bkd->bqk", q, k_ref[...].astype(BF16), preferred_element_type=F32) + bias_ref[...]
    sink = sink_ref[...]
    m = jnp.maximum(jnp.max(s, -1, keepdims=True), sink)
    e = jnp.exp(s - m)
    pr = e / (jnp.sum(e, -1, keepdims=True) + jnp.exp(sink - m))
    o_ref[...] = jnp.einsum("bqk,bkd->bqd", pr.astype(BF16), v_ref[...].astype(BF16), preferred_element_type=F32)


def _attn_sample(q, k, v, bias, sink):
    g, r, hd = q.shape
    kk = k.shape[1]
    gb = bias.shape[0]
    return pl.pallas_call(
        _attn_sample_kernel,
        grid=(g // gb,),
        in_specs=[pl.BlockSpec((gb, r, hd), lambda i: (i, 0, 0)),
                  pl.BlockSpec((gb, kk, hd), lambda i: (i, 0, 0)),
                  pl.BlockSpec((gb, kk, hd), lambda i: (i, 0, 0)),
                  pl.BlockSpec((gb, r, kk), lambda i: (0, 0, 0)),
                  pl.BlockSpec((gb, r, 1), lambda i: (0, 0, 0))],
        out_specs=pl.BlockSpec((gb, r, hd), lambda i: (i, 0, 0)),
        out_shape=jax.ShapeDtypeStruct((g, r, hd), F32),
        compiler_params=_cparams(("arbitrary",)),
        name="attn_sample",
    )(q, k, v, bias, sink)


CONV_ROWS = 64


def _convb_prompt_kernel(a_ref, b_ref, w_ref, bias_ref, g_ref, beta_ref, cb_ref, tail_ref, ubuf, shbuf, *, tl):
    @pl.when(pl.program_id(1) == 0)
    def _():
        ubuf[0:HALO_B, :] = jnp.zeros((HALO_B, C_B), F32)

    u = a_ref[...] * _sigmoid(b_ref[...])
    tail_ref[0] = u[tl - HALO_B:tl, :]
    ubuf[HALO_B:HALO_B + tl, :] = _round_bf16(u)
    off = HALO_B - (CONV_B - 1)
    span = tl + HALO_B - SUBLANE
    for s in range(1, SUBLANE):
        shbuf[s - 1, 0:span, :] = ubuf[s:s + span, :]

    def body(r, carry):
        base = pl.multiple_of(r * CONV_ROWS, CONV_ROWS)
        accs = []
        for c in range(C_B // LANE):
            cs = slice(c * LANE, (c + 1) * LANE)
            acc = jnp.zeros((CONV_ROWS, LANE), F32)
            for w in range(CONV_B):
                d = off + w
                start = pl.multiple_of(base + (d // SUBLANE) * SUBLANE, SUBLANE)
                if d % SUBLANE == 0:
                    win = ubuf[pl.ds(start, CONV_ROWS), cs]
                else:
                    win = shbuf[d % SUBLANE - 1, pl.ds(start, CONV_ROWS), cs]
                acc = acc + win * w_ref[w:w + 1, cs]
            accs.append(acc + bias_ref[:, cs])
        y = _layer_norm(jnp.concatenate(accs, -1), g_ref[...], beta_ref[...])
        cb_ref[pl.ds(base, CONV_ROWS), :] = (y * _sigmoid(y)).astype(cb_ref.dtype)
        return carry

    lax.fori_loop(0, tl // CONV_ROWS, body, 0)
    ubuf[0:HALO_B, :] = ubuf[tl:tl + HALO_B, :]


def _convb_prompt(h, w_dw, b_dw, ln_g, ln_b, nbatch, seq):
    tl = _tile(seq, 512, CONV_ROWS)
    nt = seq // tl
    acol = COL_GLU // C_B
    return pl.pallas_call(
        functools.partial(_convb_prompt_kernel, tl=tl),
        grid=(nbatch, nt),
        in_specs=[pl.BlockSpec((tl, C_B), lambda b, t: (b * nt + t, acol)),
                  pl.BlockSpec((tl, C_B), lambda b, t: (b * nt + t, acol + 1)),
                  pl.BlockSpec((HALO_B, C_B), lambda b, t: (0, 0)),
                  pl.BlockSpec((1, C_B), lambda b, t: (0, 0)),
                  pl.BlockSpec((1, C_B), lambda b, t: (0, 0)),
                  pl.BlockSpec((1, C_B), lambda b, t: (0, 0))],
        out_specs=[pl.BlockSpec((tl, C_B), lambda b, t: (b * nt + t, 0)),
                   pl.BlockSpec((1, HALO_B, C_B), lambda b, t: (b, 0, 0))],
        out_shape=[jax.ShapeDtypeStruct((nbatch * seq, C_B), BF16),
                   jax.ShapeDtypeStruct((nbatch, HALO_B, C_B), F32)],
        scratch_shapes=[pltpu.VMEM((HALO_B + tl, C_B), F32),
                        pltpu.VMEM((SUBLANE - 1, HALO_B + tl, C_B), F32)],
        compiler_params=_cparams(("arbitrary", "arbitrary")),
        name="convb_prompt",
    )(h, h, w_dw, b_dw, ln_g, ln_b)


def _convb_sample_kernel(a_ref, b_ref, st_ref, wt_ref, bias_ref, g_ref, beta_ref, cb_ref, nst_ref, ext, *, ls, wrows):
    ns = CONV_B - 1
    ext[:, ns - ns % SUBLANE:wrows, :] = jnp.zeros((ext.shape[0], wrows - (ns - ns % SUBLANE), C_B), F32)
    ext[:, 0:ns, :] = st_ref[...]
    ext[:, ns:ns + ls, :] = a_ref[...] * _sigmoid(b_ref[...])
    nst_ref[...] = ext[:, ls:ls + ns, :]
    win = _round_bf16(ext[...])
    for t in range(ls):
        y = jnp.sum(win * wt_ref[t][None], axis=1) + bias_ref[...]
        y = _layer_norm(y, g_ref[...], beta_ref[...])
        cb_ref[t] = (y * _sigmoid(y)).astype(cb_ref.dtype)


def _convb_sample(hs, state, wt, b_dw, ln_g, ln_b):
    s, ls, _ = hs.shape
    wrows = wt.shape[1]
    sb = _tile(s, 16, 1)
    acol = COL_GLU // C_B
    ns = CONV_B - 1
    return pl.pallas_call(
        functools.partial(_convb_sample_kernel, ls=ls, wrows=wrows),
        grid=(s // sb,),
        in_specs=[pl.BlockSpec((sb, ls, C_B), lambda i: (i, 0, acol)),
                  pl.BlockSpec((sb, ls, C_B), lambda i: (i, 0, acol + 1)),
                  pl.BlockSpec((sb, ns, C_B), lambda i: (i, 0, 0)),
                  pl.BlockSpec((ls, wrows, C_B), lambda i: (0, 0, 0)),
                  pl.BlockSpec((1, C_B), lambda i: (0, 0)),
                  pl.BlockSpec((1, C_B), lambda i: (0, 0)),
                  pl.BlockSpec((1, C_B), lambda i: (0, 0))],
        out_specs=[pl.BlockSpec((ls, sb, C_B), lambda i: (0, i, 0)),
                   pl.BlockSpec((sb, ns, C_B), lambda i: (i, 0, 0))],
        out_shape=[jax.ShapeDtypeStruct((ls, s, C_B), BF16),
                   jax.ShapeDtypeStruct((s, ns, C_B), F32)],
        scratch_shapes=[pltpu.VMEM((sb, wrows, C_B), F32)],
        compiler_params=_cparams(("arbitrary",)),
        name="convb_sample",
    )(hs, hs, state, wt, b_dw, ln_g, ln_b)


def _softplus(x):
    return jnp.maximum(x, 0.0) + jnp.log(1.0 + jnp.exp(-jnp.abs(x)))


DN_UNIT = DN_CHUNK
DN_STACK = N_HEADS_C * DN_UNIT
DN_PAIR = 2 * DN_UNIT
DN_GL_ROWS = 16


def _dn_prep_kernel(q_ref, k_ref, v_ref, qp_ref, kp_ref, vp_ref, sm_ref, wc_ref, alog_ref, dtb_ref,
                    u_ref, w_ref, qd_ref, kdt_ref, qk_ref, gl_ref, cbuf, *, c, lr, ups):
    rows = ups * DN_UNIT
    halo = jnp.concatenate([qp_ref[...], kp_ref[...], vp_ref[...]], -1)
    cbuf[0:HALO_C, :] = _round_bf16(jnp.where(pl.program_id(1) == 0, 0.0, halo))
    cbuf[HALO_C:HALO_C + rows, 0:V_C] = _round_bf16(q_ref[...])
    cbuf[HALO_C:HALO_C + rows, V_C:2 * V_C] = _round_bf16(k_ref[...])
    cbuf[HALO_C:HALO_C + rows, 2 * V_C:3 * V_C] = _round_bf16(v_ref[...])
    off = HALO_C - (CONV_C - 1)
    y = cbuf[off:off + rows, :] * wc_ref[0:1, :]
    for w in range(1, CONV_C):
        y = y + cbuf[off + w:off + w + rows, :] * wc_ref[w:w + 1, :]
    y = y * _sigmoid(y)
    outs = (u_ref, w_ref, qd_ref, kdt_ref, qk_ref, gl_ref)
    _dn_prep_tables(y, sm_ref[...], alog_ref[...], dtb_ref[...], outs, ups=ups, c=c, lr=lr)


def _dn_prep_tables(y, small, alog, dtb, outs, *, ups, c, lr):
    u_ref, w_ref, qd_ref, kdt_ref, qk_ref, gl_ref = outs
    nt = (((1,), (1,)), ((), ()))
    hd = HEAD_DIM_C
    beta_f = _sigmoid(small)
    g_f = -jnp.exp(alog) * _softplus(small + dtb)
    if lr < c:
        live = lax.broadcasted_iota(jnp.int32, beta_f.shape, 0) % c >= c - lr
        beta_f = jnp.where(live, beta_f, 0.0)
        g_f = jnp.where(live, g_f, 0.0)

    r2 = lax.broadcasted_iota(jnp.int32, (2 * DN_UNIT, DN_UNIT), 0)
    c2 = lax.broadcasted_iota(jnp.int32, (2 * DN_UNIT, DN_UNIT), 1)
    rr = jnp.where(r2 >= DN_UNIT, r2 - DN_UNIT, r2)
    same2 = (rr // c) == (c2 // c)
    summer = jnp.where(same2 & ((r2 >= DN_UNIT) | (rr >= c2)), 1.0, 0.0).astype(F32)

    ri = lax.broadcasted_iota(jnp.int32, (DN_PAIR, DN_PAIR), 0)
    ci = lax.broadcasted_iota(jnp.int32, (DN_PAIR, DN_PAIR), 1)
    same = (ri // c) == (ci // c)
    tri = same & (ri >= ci)
    stri = same & (ri > ci)
    eye = jnp.where(ri == ci, 1.0, 0.0).astype(F32)
    items = [(slot, pair) for slot in range(ups) for pair in range(N_HEADS_C // 2)]
    st = []
    for slot, pair in items:
        unit_rows = slice(slot * DN_UNIT, (slot + 1) * DN_UNIT)
        yu = y[unit_rows]
        if pair == 0:
            cg = jnp.dot(summer, g_f[unit_rows], precision=HIGHEST, preferred_element_type=F32)
        heads = (2 * pair, 2 * pair + 1)
        stack = lambda first: jnp.concatenate([yu[:, first + h * hd:first + (h + 1) * hd] for h in heads], 0)
        cols = lambda x, first: jnp.concatenate([x[:, first + h:first + h + 1] for h in heads], 0)
        qs, ks, vs = stack(0), stack(V_C), stack(2 * V_C)
        qn = qs * lax.rsqrt(jnp.sum(qs * qs, -1, keepdims=True) + RMS_EPS) * (hd ** -0.5)
        kn = ks * lax.rsqrt(jnp.sum(ks * ks, -1, keepdims=True) + RMS_EPS)
        beta_s = cols(beta_f[unit_rows], 0)
        gcs = cols(cg[0:DN_UNIT], N_HEADS_C)
        gls = cols(cg[DN_UNIT:2 * DN_UNIT], N_HEADS_C)
        grow = jnp.broadcast_to(gcs, (DN_PAIR, LANE)).T[0:1, :]
        decay = jnp.where(tri, jnp.exp(jnp.where(tri, gcs - grow, 0.0)), 0.0)
        kb = kn.astype(BF16)
        kk = lax.dot_general(kb, kb, nt, preferred_element_type=F32)
        nmat = jnp.where(stri, -(beta_s * kk * decay), 0.0)
        egc = jnp.exp(gcs)
        st.append(dict(qn=qn, kn=kn, kb=kb, decay=decay, nmat=nmat, egc=egc, gcs=gcs, gls=gls,
                       rhs=jnp.concatenate([vs * beta_s, kn * (beta_s * egc)], -1).astype(BF16)))

    invs = [eye + s["nmat"] for s in st]
    nbs = [s["nmat"].astype(BF16) for s in st]
    for _ in range(int(math.log2(c)) - 1):
        nbs = [jnp.dot(nb, nb, preferred_element_type=F32).astype(BF16) for nb in nbs]
        invs = [inv + jnp.dot(inv.astype(BF16), nb, preferred_element_type=F32) for inv, nb in zip(invs, nbs)]
    sols = [jnp.dot(inv.astype(BF16), s["rhs"], preferred_element_type=F32) for inv, s in zip(invs, st)]

    gl_rows = {slot: [] for slot in range(ups)}
    for (slot, pair), s, sol in zip(items, st, sols):
        rows = slice(pair * DN_PAIR, (pair + 1) * DN_PAIR)
        u_ref[slot, rows, :] = sol[:, :hd]
        w_ref[slot, rows, :] = sol[:, hd:].astype(BF16)
        qk = jnp.where(tri, lax.dot_general(s["qn"].astype(BF16), s["kb"], nt, preferred_element_type=F32)
                       * s["decay"], 0.0)
        qk_ref[slot, rows, :] = qk.astype(BF16)
        qd_ref[slot, rows, :] = (s["qn"] * s["egc"]).astype(BF16)
        kdt_ref[slot, :, rows] = (s["kn"] * jnp.exp(s["gls"] - s["gcs"])).T.astype(BF16)
        glb = jnp.broadcast_to(jnp.exp(s["gls"]), (DN_PAIR, LANE))
        gl_rows[slot] += [glb[g * c:g * c + 1, :] for g in range(DN_PAIR // c)]
    groups = DN_STACK // c
    for slot in range(ups):
        if groups < DN_GL_ROWS:
            gl_rows[slot].append(jnp.ones((DN_GL_ROWS - groups, LANE), F32))
        gl_ref[slot] = jnp.concatenate(gl_rows[slot], 0)


def _dn_prep(src, small_src, wc, alog_row, dtb_row, *, qcol, smcol, nbatch, nchunk, c, lr):
    nu = nbatch * nchunk
    split = nbatch if nchunk == 1 else nchunk
    ups = max(u for u in (1, 2, 4) if split % u == 0)
    if nchunk == 1:
        nbatch //= ups
    else:
        nchunk //= ups
    rows = ups * DN_UNIT
    per8 = rows // SUBLANE

    def cur(col):
        return pl.BlockSpec((rows, V_C), lambda b, n: (b * nchunk + n, col))

    def prev(col):
        return pl.BlockSpec((HALO_C, V_C), lambda b, n: (jnp.maximum((b * nchunk + n) * per8 - 1, 0), col))

    unit = lambda r, cdim: pl.BlockSpec((ups, r, cdim), lambda b, n: (b * nchunk + n, 0, 0))
    return pl.pallas_call(
        functools.partial(_dn_prep_kernel, c=c, lr=lr, ups=ups),
        grid=(nbatch, nchunk),
        in_specs=[cur(qcol), cur(qcol + 1), cur(qcol + 2), prev(qcol), prev(qcol + 1), prev(qcol + 2),
                  pl.BlockSpec((rows, LANE), lambda b, n: (b * nchunk + n, smcol)),
                  pl.BlockSpec((SUBLANE, QKV_C), lambda b, n: (0, 0)),
                  pl.BlockSpec((1, LANE), lambda b, n: (0, 0)),
                  pl.BlockSpec((1, LANE), lambda b, n: (0, 0))],
        out_specs=[unit(DN_STACK, HEAD_DIM_C), unit(DN_STACK, HEAD_DIM_C), unit(DN_STACK, HEAD_DIM_C),
                   unit(HEAD_DIM_C, DN_STACK), unit(DN_STACK, DN_PAIR), unit(DN_GL_ROWS, LANE)],
        out_shape=[jax.ShapeDtypeStruct((nu, DN_STACK, HEAD_DIM_C), F32),
                   jax.ShapeDtypeStruct((nu, DN_STACK, HEAD_DIM_C), BF16),
                   jax.ShapeDtypeStruct((nu, DN_STACK, HEAD_DIM_C), BF16),
                   jax.ShapeDtypeStruct((nu, HEAD_DIM_C, DN_STACK), BF16),
                   jax.ShapeDtypeStruct((nu, DN_STACK, DN_PAIR), BF16),
                   jax.ShapeDtypeStruct((nu, DN_GL_ROWS, LANE), F32)],
        scratch_shapes=[pltpu.VMEM((HALO_C + rows, QKV_C), F32)],
        compiler_params=_cparams(("arbitrary", "arbitrary")),
        name="deltanet_prep",
    )(src, src, src, src, src, src, small_src, wc, alog_row, dtb_row)


def _dn_seq_kernel(u_ref, w_ref, qd_ref, kdt_ref, qk_ref, gl_ref, *rest, c, nsq, streams, carry):
    z_refs = rest[:streams]
    nw_ref, s0_ref, o_ref, sout_ref, sbuf = rest[streams:]
    groups = DN_STACK // c
    hd = HEAD_DIM_C
    span = groups * hd
    step = pl.program_id(0)

    def load_state():
        for j in range(streams):
            for g in range(groups):
                sbuf[j * span + g * hd:j * span + (g + 1) * hd, :] = s0_ref[j, 0, g % nsq, g // nsq]

    def store_state():
        for j in range(streams):
            for g in range(groups):
                sout_ref[j, 0, g % nsq, g // nsq] = sbuf[j * span + g * hd:j * span + (g + 1) * hd, :]

    if carry:
        pl.when(step == 0)(load_state)
    else:
        load_state()

    col_group = lax.broadcasted_iota(jnp.int32, (hd, DN_STACK), 1) // c
    vbs, o_inter = [], []
    for j in range(streams):
        u = u_ref[j, 0]
        w = w_ref[j, 0]
        qd = qd_ref[j, 0]
        v_parts, o_parts = [], []
        for g in range(groups):
            rows = slice(g * c, (g + 1) * c)
            s_b = sbuf[j * span + g * hd:j * span + (g + 1) * hd, :].astype(BF16)
            r = jnp.dot(jnp.concatenate([w[rows], qd[rows]], 0), s_b, preferred_element_type=F32)
            v_parts.append(u[rows] - r[:c])
            o_parts.append(r[c:])
        vbs.append(jnp.concatenate(v_parts, 0).astype(BF16))
        o_inter.append(jnp.concatenate(o_parts, 0))

    o_all = []
    for j in range(streams):
        qk = qk_ref[j, 0]
        o_all.append(o_inter[j] + jnp.concatenate(
            [jnp.dot(qk[p * DN_PAIR:(p + 1) * DN_PAIR], vbs[j][p * DN_PAIR:(p + 1) * DN_PAIR],
                     preferred_element_type=F32) for p in range(DN_STACK // DN_PAIR)], 0))

    for j in range(streams):
        kdt = kdt_ref[j, 0]
        k_big = jnp.concatenate([jnp.where(col_group == g, kdt, jnp.zeros_like(kdt)) for g in range(groups)], 0)
        gl = gl_ref[j, 0]
        scale = jnp.concatenate([jnp.broadcast_to(gl[g:g + 1, :], (hd, LANE)) for g in range(groups)], 0)
        sbuf[j * span:(j + 1) * span, :] = (sbuf[j * span:(j + 1) * span, :] * scale
                                             + jnp.dot(k_big, vbs[j], preferred_element_type=F32))

    for j in range(streams):
        outs = []
        for h in range(N_HEADS_C):
            oh = o_all[j][h * DN_UNIT:(h + 1) * DN_UNIT]
            on = oh * lax.rsqrt(jnp.mean(oh * oh, -1, keepdims=True) + RMS_EPS) * nw_ref[...]
            zh = z_refs[j][:, h * hd:(h + 1) * hd]
            outs.append(on * (zh * _sigmoid(zh)))
        o_ref[j] = jnp.concatenate(outs, -1).astype(o_ref.dtype)

    if carry:
        pl.when(step == pl.num_programs(0) - 1)(store_state)
    else:
        store_state()


def _dn_seq(prep, z_src, nw, s0, *, zcol, streams, steps, c, nsq, carry, layer=None):
    groups = DN_STACK // c
    prep = [t.reshape((streams, steps) + t.shape[1:]) for t in prep]
    unit = lambda r, cdim: pl.BlockSpec((streams, 1, r, cdim), lambda i: (0, i, 0, 0))
    sdim = (streams, 1, nsq, N_HEADS_C, HEAD_DIM_C, HEAD_DIM_C)
    state = pl.BlockSpec(sdim, (lambda i: (0, 0, 0, 0, 0, 0)) if carry else (lambda i: (0, i, 0, 0, 0, 0)))
    state_in, state_shape = state, s0.shape
    if layer is not None:
        state_in = pl.BlockSpec((None,) + sdim, lambda i: (layer, 0, 0 if carry else i, 0, 0, 0, 0))
        state_shape = s0.shape[1:]
    z_specs = [pl.BlockSpec((DN_UNIT, V_C), functools.partial(lambda i, j: (j * steps + i, zcol), j=j))
               for j in range(streams)]
    o, s_out = pl.pallas_call(
        functools.partial(_dn_seq_kernel, c=c, nsq=nsq, streams=streams, carry=carry),
        grid=(steps,),
        in_specs=[unit(DN_STACK, HEAD_DIM_C), unit(DN_STACK, HEAD_DIM_C), unit(DN_STACK, HEAD_DIM_C),
                  unit(HEAD_DIM_C, DN_STACK), unit(DN_STACK, DN_PAIR), unit(DN_GL_ROWS, LANE)]
                 + z_specs + [pl.BlockSpec((1, HEAD_DIM_C), lambda i: (0, 0)), state_in],
        out_specs=[pl.BlockSpec((streams, DN_UNIT, V_C), lambda i: (0, i, 0)), state],
        out_shape=[jax.ShapeDtypeStruct((streams, steps * DN_UNIT, V_C), BF16),
                   jax.ShapeDtypeStruct(state_shape, F32)],
        scratch_shapes=[pltpu.VMEM((streams * groups * HEAD_DIM_C, HEAD_DIM_C), F32)],
        compiler_params=_cparams(("arbitrary",)),
        name="deltanet_seq",
    )(*prep, *([z_src] * streams), nw, s0)
    return o.reshape(streams * steps * DN_UNIT, V_C), s_out


def _merge_kernel(oap_ref, oas_ref, cbp_ref, cbs_ref, ocp_ref, ocs_ref, ga_ref, gb_ref, gc_ref, x_ref,
                  woa_ref, wob_ref, woc_ref, wout_ref, g1_ref, b1_ref, wr_ref, br_ref,
                  x1_ref, route_ref, gate_ref, cnt_ref, carry, *, alpha, prompt_tiles):
    @pl.when(pl.program_id(0) == 0)
    def _():
        carry[...] = jnp.zeros(carry.shape, F32)

    is_prompt = pl.program_id(0) < prompt_tiles
    pick = lambda p_ref, s_ref: jnp.where(is_prompt, p_ref[...], s_ref[...])
    br_a = jnp.dot(pick(oap_ref, oas_ref), woa_ref[...], preferred_element_type=F32)
    br_b = jnp.dot(pick(cbp_ref, cbs_ref), wob_ref[...], preferred_element_type=F32)
    br_c = jnp.dot(pick(ocp_ref, ocs_ref), woc_ref[...], preferred_element_type=F32)
    mixin = _sigmoid(ga_ref[...]) * br_a + _sigmoid(gb_ref[...]) * br_b + _sigmoid(gc_ref[...]) * br_c
    mix = jnp.dot(mixin.astype(BF16), wout_ref[...], preferred_element_type=F32)
    x1 = _layer_norm(alpha * x_ref[...] + mix, g1_ref[...], b1_ref[...])
    x1_ref[...] = x1
    logits = jnp.dot(x1.astype(BF16), wr_ref[...], preferred_element_type=F32) + br_ref[...]
    lane = lax.broadcasted_iota(jnp.int32, logits.shape, 1)
    logits = jnp.where(lane < N_EXPERTS, logits, -jnp.inf)
    route = jnp.zeros(logits.shape, jnp.int32)
    val_out = jnp.zeros(logits.shape, F32)
    member = jnp.zeros(logits.shape, F32)
    sels = []
    top0 = None
    den = None
    for k in range(TOP_K):
        m = jnp.max(logits, -1, keepdims=True)
        sel = jnp.min(jnp.where(logits == m, lane, LANE), -1, keepdims=True)
        if k == 0:
            top0 = m
            e = jnp.ones_like(m)
            den = e
        else:
            e = jnp.exp(m - top0)
            den = den + e
        hit = lane == sel
        route = jnp.where(lane == k, sel, route)
        val_out = jnp.where(lane == k, e, val_out)
        member = jnp.where(hit, 1.0, member)
        logits = jnp.where(hit, -jnp.inf, logits)
        sels.append(sel)
    gate_ref[...] = val_out / den
    tm = logits.shape[0]
    earlier = lax.broadcasted_iota(jnp.int32, (tm, tm), 0) > lax.broadcasted_iota(jnp.int32, (tm, tm), 1)
    cum = jnp.dot(jnp.where(earlier, 1.0, 0.0).astype(BF16), member.astype(BF16),
                  preferred_element_type=F32) + carry[...]
    for k in range(TOP_K):
        rank = jnp.sum(jnp.where(lane == sels[k], cum, 0.0), -1, keepdims=True)
        route = jnp.where(lane == TOP_K + k, rank.astype(jnp.int32), route)
    route_ref[...] = route
    carry[...] = carry[...] + jnp.sum(member, axis=0, keepdims=True)
    cnt_ref[...] = carry[...]


def _merge(branches, h, x, lw, alpha):
    m = x.shape[0]
    tp = branches[0][0].shape[0]
    tm = _tile(math.gcd(tp, m - tp), 512)
    np_tiles = tp // tm
    row = lambda i: (i, 0)
    full = lambda i: (0, 0)
    gcol = COL_GATE // D_MODEL
    p_spec = pl.BlockSpec((tm, Q_A), lambda i: (jnp.minimum(i, np_tiles - 1), 0))
    s_spec = pl.BlockSpec((tm, Q_A), lambda i: (jnp.maximum(i - np_tiles, 0), 0))
    return pl.pallas_call(
        functools.partial(_merge_kernel, alpha=alpha, prompt_tiles=np_tiles),
        grid=(m // tm,),
        in_specs=[p_spec, s_spec, p_spec, s_spec, p_spec, s_spec,
                  pl.BlockSpec((tm, D_MODEL), lambda i: (i, gcol)),
                  pl.BlockSpec((tm, D_MODEL), lambda i: (i, gcol + 1)),
                  pl.BlockSpec((tm, D_MODEL), lambda i: (i, gcol + 2)),
                  pl.BlockSpec((tm, D_MODEL), row),
                  pl.BlockSpec((Q_A, D_MODEL), full), pl.BlockSpec((C_B, D_MODEL), full),
                  pl.BlockSpec((V_C, D_MODEL), full), pl.BlockSpec((D_MODEL, D_MODEL), full),
                  pl.BlockSpec((1, D_MODEL), full), pl.BlockSpec((1, D_MODEL), full),
                  pl.BlockSpec((D_MODEL, LANE), full), pl.BlockSpec((1, LANE), full)],
        out_specs=[pl.BlockSpec((tm, D_MODEL), row), pl.BlockSpec((tm, LANE), row),
                   pl.BlockSpec((tm, LANE), row), pl.BlockSpec((1, LANE), full)],
        out_shape=[jax.ShapeDtypeStruct((m, D_MODEL), F32), jax.ShapeDtypeStruct((m, LANE), jnp.int32),
                   jax.ShapeDtypeStruct((m, LANE), F32), jax.ShapeDtypeStruct((1, LANE), F32)],
        scratch_shapes=[pltpu.VMEM((1, LANE), F32)],
        compiler_params=_cparams(("arbitrary",)),
        name="merge_router",
    )(*branches[0], *branches[1], *branches[2], h, h, h, x, lw["w_o_a"], lw["w_o_b"], lw["w_o_c"], lw["w_out"],
      lw["ln1_g"], lw["ln1_b"], lw["w_router"], lw["b_router"])


def _row_copy(src, src_row, dst, dst_row, sem):
    return pltpu.make_async_copy(src.at[pl.ds(src_row, 1)], dst.at[pl.ds(dst_row, 1)], sem)


def _dispatch_kernel(dest_ref, last_ref, x_ref, xs_ref, zbuf, sem, *, tt, n_blk):
    @pl.when(pl.program_id(0) == 0)
    def _():
        zbuf[...] = jnp.zeros(zbuf.shape, F32)

        def zero_block(first_row):
            return pltpu.make_async_copy(zbuf, xs_ref.at[pl.ds(pl.multiple_of(first_row, MOE_ROWS), MOE_ROWS)], sem)

        def start_tail(b, carry):
            zero_block(b * MOE_ROWS).start()
            return carry

        def wait_tail(b, carry):
            zero_block(0).wait()
            return carry

        n_active = last_ref[N_EXPERTS]
        for e in range(N_EXPERTS):
            zero_block(last_ref[e]).start()
        lax.fori_loop(n_active, n_blk, start_tail, 0)
        for e in range(N_EXPERTS):
            zero_block(0).wait()
        lax.fori_loop(n_active, n_blk, wait_tail, 0)

    def body(t, carry):
        for k in range(TOP_K):
            _row_copy(x_ref, t, xs_ref, dest_ref[t * TOP_K + k], sem).start(priority=k % 2)
        return carry

    lax.fori_loop(0, tt, body, 0, unroll=4)
    for k in range(TOP_K):
        pltpu.make_async_copy(x_ref, xs_ref.at[pl.ds(0, tt)], sem).wait()


def _dispatch(dest, last_blk, x1, n_rows):
    m = x1.shape[0]
    tt = _tile(m, 512)
    return pl.pallas_call(
        functools.partial(_dispatch_kernel, tt=tt, n_blk=n_rows // MOE_ROWS),
        grid=(m // tt,),
        in_specs=[pl.BlockSpec((tt * TOP_K,), lambda i: (i,), memory_space=pltpu.SMEM),
                  pl.BlockSpec(memory_space=pltpu.SMEM),
                  pl.BlockSpec((tt, D_MODEL), lambda i: (i, 0))],
        out_specs=pl.BlockSpec(memory_space=pl.ANY),
        out_shape=jax.ShapeDtypeStruct((n_rows, D_MODEL), F32),
        scratch_shapes=[pltpu.VMEM((MOE_ROWS, D_MODEL), F32), pltpu.SemaphoreType.DMA(())],
        compiler_params=pltpu.CompilerParams(dimension_semantics=("arbitrary",), vmem_limit_bytes=VMEM_LIMIT,
                                             disable_bounds_checks=True),
        name="moe_dispatch",
    )(dest, last_blk, x1)


def _moe_kernel(be_ref, na_ref, x_ref, wu_ref, bu_ref, wd_ref, bd_ref, o_ref, wu_b, wd_b):
    i = pl.program_id(0)
    active = i < na_ref[0]
    new_expert = jnp.logical_or(i == 0, be_ref[i] != be_ref[jnp.maximum(i - 1, 0)])

    @pl.when(jnp.logical_and(active, new_expert))
    def _():
        wu_b[...] = wu_ref[0, 0].astype(BF16)
        wd_b[...] = wd_ref[0, 0].astype(BF16)

    @pl.when(active)
    def _():
        hu = jnp.dot(x_ref[...].astype(BF16), wu_b[...], preferred_element_type=F32) + bu_ref[0, 0]
        gt = jnp.minimum(hu[:, :D_FF], SWIGLU_LIMIT)
        up = jnp.clip(hu[:, D_FF:], -SWIGLU_LIMIT, SWIGLU_LIMIT)
        act = (up + 1.0) * gt * _sigmoid(SWIGLU_ALPHA * gt)
        o_ref[...] = jnp.dot(act.astype(BF16), wd_b[...], preferred_element_type=F32) + bd_ref[0, 0]

    @pl.when(jnp.logical_not(active))
    def _():
        o_ref[...] = jnp.zeros(o_ref.shape, F32)


def _moe(xs, blk_e, n_active, w_up, b_up, w_down, b_down, layer):
    nblk = xs.shape[0] // MOE_ROWS

    def blk(i, na):
        return jnp.minimum(i, jnp.maximum(na[0] - 1, 0))

    weights = lambda i, be, na: (layer, be[blk(i, na)], 0, 0)
    grid_spec = pltpu.PrefetchScalarGridSpec(
        num_scalar_prefetch=2,
        grid=(nblk,),
        in_specs=[pl.BlockSpec((MOE_ROWS, D_MODEL), lambda i, be, na: (blk(i, na), 0)),
                  pl.BlockSpec((1, 1, D_MODEL, 2 * D_FF), weights),
                  pl.BlockSpec((1, 1, 1, 2 * D_FF), weights),
                  pl.BlockSpec((1, 1, D_FF, D_MODEL), weights),
                  pl.BlockSpec((1, 1, 1, D_MODEL), weights)],
        out_specs=pl.BlockSpec((MOE_ROWS, D_MODEL), lambda i, be, na: (i, 0)),
        scratch_shapes=[pltpu.VMEM((D_MODEL, 2 * D_FF), BF16), pltpu.VMEM((D_FF, D_MODEL), BF16)],
    )
    return pl.pallas_call(
        _moe_kernel,
        grid_spec=grid_spec,
        out_shape=jax.ShapeDtypeStruct((nblk * MOE_ROWS, D_MODEL), F32),
        compiler_params=_cparams(("arbitrary",)),
        name="moe_experts",
    )(blk_e, n_active, xs, w_up, b_up, w_down, b_down)

def _post_kernel(dest_ref, yb_ref, gate_ref, x1_ref, p_ref, wg_ref, wp_ref, g2_ref, b2_ref, x_ref, xb_ref,
                 ybuf, sem, *, alpha, tm):
    def body(t, carry):
        for k in range(TOP_K):
            _row_copy(yb_ref, dest_ref[t * TOP_K + k], ybuf.at[k], t, sem).start(priority=k % 2)
        return carry

    lax.fori_loop(0, tm, body, 0, unroll=4)
    for k in range(TOP_K):
        pltpu.make_async_copy(yb_ref.at[pl.ds(0, tm)], ybuf.at[k], sem).wait()

    gate = gate_ref[...]
    ffn = gate[:, 0:1] * ybuf[0]
    for k in range(1, TOP_K):
        ffn = ffn + gate[:, k:k + 1] * ybuf[k]
    x2 = _layer_norm(alpha * x1_ref[...] + ffn, g2_ref[...], b2_ref[...])
    gl = jnp.dot(x2.astype(BF16), wg_ref[...], preferred_element_type=F32)
    pe = jnp.dot(p_ref[...].astype(BF16), wp_ref[...], preferred_element_type=F32)
    xo = x2 + _sigmoid(gl) * pe
    x_ref[...] = xo
    xb_ref[...] = xo.astype(BF16)


def _post(dest, yb, gate, x1, p, lw, alpha):
    m = x1.shape[0]
    tm = _tile(m, 512)
    row = lambda i: (i, 0)
    full = lambda i: (0, 0)
    return pl.pallas_call(
        functools.partial(_post_kernel, alpha=alpha, tm=tm),
        grid=(m // tm,),
        in_specs=[pl.BlockSpec((tm * TOP_K,), lambda i: (i,), memory_space=pltpu.SMEM),
                  pl.BlockSpec(memory_space=pl.ANY),
                  pl.BlockSpec((tm, LANE), row),
                  pl.BlockSpec((tm, D_MODEL), row), pl.BlockSpec((tm, PLE_DIM), row),
                  pl.BlockSpec((D_MODEL, D_MODEL), full), pl.BlockSpec((PLE_DIM, D_MODEL), full),
                  pl.BlockSpec((1, D_MODEL), full), pl.BlockSpec((1, D_MODEL), full)],
        out_specs=[pl.BlockSpec((tm, D_MODEL), row), pl.BlockSpec((tm, D_MODEL), row)],
        out_shape=[jax.ShapeDtypeStruct((m, D_MODEL), F32), jax.ShapeDtypeStruct((m, D_MODEL), BF16)],
        scratch_shapes=[pltpu.VMEM((TOP_K, tm, D_MODEL), F32), pltpu.SemaphoreType.DMA(())],
        compiler_params=pltpu.CompilerParams(dimension_semantics=("arbitrary",), vmem_limit_bytes=VMEM_LIMIT,
                                             disable_bounds_checks=True),
        name="combine_ple",
    )(dest, yb, gate, x1, p, lw["w_ple_gate"], lw["w_ple_proj"], lw["ln2_g"], lw["ln2_b"])


def _post_next_kernel(dfirst_ref, dnext_ref, yb_ref, gate_ref, x1_ref, p_ref, wg_ref, wp_ref, g2_ref, b2_ref,
                      win_ref, bin_ref, x_ref, h_ref, ybuf, sem, *, alpha, tm, n_tiles, tn):
    i = pl.program_id(0)
    slot = i % 2

    def issue(dest_ref, to_slot):
        def body(t, carry):
            for k in range(TOP_K):
                _row_copy(yb_ref, dest_ref[t * TOP_K + k], ybuf.at[to_slot, k], t,
                          sem.at[to_slot]).start(priority=k % 2)
            return carry

        lax.fori_loop(0, tm, body, 0, unroll=4)

    @pl.when(i == 0)
    def _():
        issue(dfirst_ref, 0)

    @pl.when(i + 1 < n_tiles)
    def _():
        issue(dnext_ref, 1 - slot)

    for k in range(TOP_K):
        pltpu.make_async_copy(yb_ref.at[pl.ds(0, tm)], ybuf.at[slot, k], sem.at[slot]).wait()

    gate = gate_ref[...]
    ffn = gate[:, 0:1] * ybuf[slot, 0]
    for k in range(1, TOP_K):
        ffn = ffn + gate[:, k:k + 1] * ybuf[slot, k]
    x2 = _layer_norm(alpha * x1_ref[...] + ffn, g2_ref[...], b2_ref[...])
    gl = jnp.dot(x2.astype(BF16), wg_ref[...], preferred_element_type=F32)
    pe = jnp.dot(p_ref[...].astype(BF16), wp_ref[...], preferred_element_type=F32)
    xo = x2 + _sigmoid(gl) * pe
    x_ref[...] = xo
    xb = xo.astype(BF16)
    for j in range(N_IN_PAD // tn):
        cs = slice(j * tn, (j + 1) * tn)
        h_ref[:, cs] = jnp.dot(xb, win_ref[:, cs], preferred_element_type=F32) + bin_ref[:, cs]


def _post_next(dest, yb, gate, x1, p, lw, alpha, w_in_next, b_in_next):
    m = x1.shape[0]
    tm = _tile(m, 256)
    n_tiles = m // tm
    tn = _tile(N_IN_PAD, 1408, LANE)
    row = lambda i: (i, 0)
    full = lambda i: (0, 0)
    once = pl.Buffered(1)
    return pl.pallas_call(
        functools.partial(_post_next_kernel, alpha=alpha, tm=tm, n_tiles=n_tiles, tn=tn),
        grid=(n_tiles,),
        in_specs=[pl.BlockSpec((tm * TOP_K,), lambda i: (0,), memory_space=pltpu.SMEM),
                  pl.BlockSpec((tm * TOP_K,), lambda i: (jnp.minimum(i + 1, n_tiles - 1),), memory_space=pltpu.SMEM),
                  pl.BlockSpec(memory_space=pl.ANY),
                  pl.BlockSpec((tm, LANE), row),
                  pl.BlockSpec((tm, D_MODEL), row), pl.BlockSpec((tm, PLE_DIM), row),
                  pl.BlockSpec((D_MODEL, D_MODEL), full, pipeline_mode=once),
                  pl.BlockSpec((PLE_DIM, D_MODEL), full, pipeline_mode=once),
                  pl.BlockSpec((1, D_MODEL), full), pl.BlockSpec((1, D_MODEL), full),
                  pl.BlockSpec((D_MODEL, N_IN_PAD), full, pipeline_mode=once),
                  pl.BlockSpec((1, N_IN_PAD), full)],
        out_specs=[pl.BlockSpec((tm, D_MODEL), row), pl.BlockSpec((tm, N_IN_PAD), row)],
        out_shape=[jax.ShapeDtypeStruct((m, D_MODEL), F32), jax.ShapeDtypeStruct((m, N_IN_PAD), F32)],
        scratch_shapes=[pltpu.VMEM((2, TOP_K, tm, D_MODEL), F32), pltpu.SemaphoreType.DMA((2,))],
        compiler_params=pltpu.CompilerParams(dimension_semantics=("arbitrary",), vmem_limit_bytes=VMEM_LIMIT,
                                             disable_bounds_checks=True),
        name="combine_ple_in_proj",
    )(dest, dest, yb, gate, x1, p, lw["w_ple_gate"], lw["w_ple_proj"], lw["ln2_g"], lw["ln2_b"],
      w_in_next, b_in_next)


def _rel_bucket(n):
    max_exact = NUM_BUCKETS // 2
    nf = jnp.maximum(n, 1).astype(F32)
    large = max_exact + (jnp.log(nf / max_exact) / math.log(MAX_DISTANCE / max_exact)
                         * (NUM_BUCKETS - max_exact)).astype(jnp.int32)
    return jnp.where(n < max_exact, jnp.maximum(n, 0), jnp.minimum(large, NUM_BUCKETS - 1))


def _bias_table(rel_bias, dist, valid):
    hit = _rel_bucket(dist)[..., None] == jnp.arange(NUM_BUCKETS)
    b = jnp.sum(jnp.where(hit[..., None], rel_bias.astype(F32), 0.0), axis=-2)
    b = jnp.where(valid[..., None], b, NEG_INF)
    return jnp.moveaxis(b, -1, 0)


def _permute_in(w):
    o = np.concatenate([[0], np.cumsum(IN_SIZES)])
    seg = lambda i: w[..., o[i]:o[i + 1]]
    small = jnp.concatenate([seg(6), seg(7)], -1)
    small = jnp.pad(small, [(0, 0)] * (w.ndim - 1) + [(0, LANE - 2 * N_HEADS_C)])
    return jnp.concatenate([seg(8), seg(3), seg(4), seg(5), seg(0), seg(1), seg(2), small], -1)


def _route(route, counts, n_tok):
    n_blk = -(-n_tok * TOP_K // MOE_ROWS) + N_EXPERTS
    cnt = counts[0, :N_EXPERTS].astype(jnp.int32)
    padded = (cnt + MOE_ROWS - 1) // MOE_ROWS * MOE_ROWS
    pad_end = jnp.cumsum(padded)
    pad_start = pad_end - padded
    idx = route[:, 0:TOP_K]
    rank = route[:, TOP_K:2 * TOP_K]
    start_of = jnp.sum(jnp.where(idx[:, :, None] == jnp.arange(N_EXPERTS)[None, None, :],
                                 pad_start[None, None, :], 0), -1)
    dest = (start_of + rank).reshape(n_tok * TOP_K).astype(jnp.int32)
    blk_first = (jnp.arange(n_blk) * MOE_ROWS)[:, None]
    blk_e = jnp.minimum(jnp.sum((pad_end[None, :] <= blk_first).astype(jnp.int32), -1), N_EXPERTS - 1)
    n_active = (pad_end[-1:] // MOE_ROWS).astype(jnp.int32)
    last_blk = jnp.concatenate([jnp.maximum(pad_end - MOE_ROWS, 0).astype(jnp.int32), n_active])
    return dest, last_blk, blk_e.astype(jnp.int32), n_active, n_blk


def kernel(x_prompt, x_sample, cache_k_a, cache_v_a, state_conv_b, state_conv_c, state_s_c, p_prompt, p_sample,
           rel_bias, w_in, b_in, sink_a, w_o_a, w_dw_b, b_dw_b, ln_b_g, ln_b_b, w_o_b, w_conv_c, a_log_c,
           dt_bias_c, norm_c, w_o_c, w_out, ln1_g, ln1_b, w_router, b_router, w_up, b_up, w_down, b_down,
           ln2_g, ln2_b, w_ple_gate, w_ple_proj):
    depth = w_in.shape[0]
    nb_p, seq, _ = x_prompt.shape
    ns, ls, _ = x_sample.shape
    w_buf = cache_k_a.shape[2]
    alpha = (2 * depth) ** 0.25
    tp = nb_p * seq
    tsm = ns * ls
    n_tok = tp + tsm
    assert seq % DN_CHUNK == 0 and seq % Q_BLOCK == 0 and w_buf == WINDOW and ls <= 16

    w_in_p = _permute_in(w_in).astype(BF16)
    b_in_p = _permute_in(b_in)[:, None, :]
    w_o_a_b, w_o_b_b, w_o_c_b, w_out_b = (t.astype(BF16) for t in (w_o_a, w_o_b, w_o_c, w_out))
    w_ple_gate_b, w_ple_proj_b = w_ple_gate.astype(BF16), w_ple_proj.astype(BF16)
    b_up4, b_down4 = b_up[:, :, None, :], b_down[:, :, None, :]
    w_router_p = jnp.pad(w_router, ((0, 0), (0, 0), (0, LANE - N_EXPERTS))).astype(BF16)
    b_router_p = jnp.pad(b_router, ((0, 0), (0, LANE - N_EXPERTS)))[:, None, :]
    w_dw_r = _round_bf16(w_dw_b)
    w_dw_p = jnp.pad(w_dw_r, ((0, 0), (0, HALO_B - CONV_B), (0, 0)))
    w_cc_p = jnp.pad(_round_bf16(w_conv_c), ((0, 0), (0, SUBLANE - CONV_C), (0, 0)))
    lane_pad = lambda t: jnp.pad(t, ((0, 0), (N_HEADS_C, LANE - 2 * N_HEADS_C)))[:, None, :]
    alog_rows, dtb_rows = lane_pad(a_log_c), lane_pad(dt_bias_c)

    qi = jnp.arange(Q_BLOCK)[:, None]
    kj = jnp.arange(2 * Q_BLOCK)[None, :]
    dist = qi - kj + Q_BLOCK
    valid = (dist >= 0) & (dist < WINDOW)
    tab_rest = _bias_table(rel_bias, dist, valid)
    tab_first = _bias_table(rel_bias, dist, valid & (kj >= Q_BLOCK))
    bias_prompt = jnp.stack([tab_first, tab_rest])

    kwin = 2 * WINDOW
    ti = jnp.arange(ls)[:, None]
    kjs = jnp.arange(kwin)[None, :]
    dist_s = w_buf + ti - kjs
    valid_s = (dist_s >= 0) & (dist_s < WINDOW) & (kjs < w_buf + ls)
    tab_s = _bias_table(rel_bias, dist_s, valid_s)
    tab_s = tab_s.reshape(N_KV_A, GROUP_A * ls, kwin)
    gb = 2 * _tile(ns, 16, 1)
    bias_s = jnp.tile(tab_s, (gb // N_KV_A, 1, 1))

    wrows = -(-(CONV_B - 1 + ls) // SUBLANE) * SUBLANE
    jj = jnp.arange(wrows)[None, :] - jnp.arange(ls)[:, None]
    tap_ok = (jj >= 0) & (jj < CONV_B)

    x_all = jnp.concatenate([x_prompt.reshape(tp, D_MODEL), x_sample.reshape(tsm, D_MODEL)], 0)
    xb_all = x_all.astype(BF16)
    p_all = jnp.concatenate([p_prompt.reshape(depth, tp, PLE_DIM), p_sample.reshape(depth, tsm, PLE_DIM)], 1)
    zero_s = jnp.zeros((nb_p, 1, 1, N_HEADS_C, HEAD_DIM_C, HEAD_DIM_C), F32)
    c_s = 16
    nsq_s = DN_UNIT // c_s
    assert ls + CONV_C - 1 <= c_s and ns % nsq_s == 0

    st_p, st_s = [], []
    h = _in_proj(xb_all, w_in_p[0], b_in_p[0])
    for i in range(depth):
        lw = {"w_o_a": w_o_a_b[i], "w_o_b": w_o_b_b[i], "w_o_c": w_o_c_b[i], "w_out": w_out_b[i],
              "ln1_g": ln1_g[i][None], "ln1_b": ln1_b[i][None], "w_router": w_router_p[i],
              "b_router": b_router_p[i], "w_ple_gate": w_ple_gate_b[i], "w_ple_proj": w_ple_proj_b[i],
              "ln2_g": ln2_g[i][None], "ln2_b": ln2_b[i][None]}
        hs = h[tp:].reshape(ns, ls, N_IN_PAD)

        sinks = sink_a[i].astype(F32)
        oa_p = _attn_prompt(h, sinks, bias_prompt, nb_p, seq)
        q_s = hs[:, :, COL_QA:COL_QA + Q_A].reshape(ns, ls, N_KV_A, GROUP_A, HEAD_DIM_A)
        q_s = q_s.transpose(0, 2, 3, 1, 4).reshape(ns * N_KV_A, GROUP_A * ls, HEAD_DIM_A)
        k_new = hs[:, :, COL_KA:COL_KA + KV_A].reshape(ns, ls, N_KV_A, HEAD_DIM_A)
        v_new = hs[:, :, COL_VA:COL_VA + KV_A].reshape(ns, ls, N_KV_A, HEAD_DIM_A)
        k_all = jnp.concatenate([cache_k_a[i], k_new], 1)
        v_all = jnp.concatenate([cache_v_a[i], v_new], 1)

        def keywin(t):
            t = jnp.pad(t, ((0, 0), (0, kwin - w_buf - ls), (0, 0), (0, 0)))
            return t.transpose(0, 2, 1, 3).reshape(ns * N_KV_A, kwin, HEAD_DIM_A)

        sink_s = jnp.tile(jnp.repeat(sinks.reshape(N_KV_A, GROUP_A), ls, axis=1)[:, :, None], (gb // N_KV_A, 1, 1))
        oa_s = _attn_sample(q_s, keywin(k_all), keywin(v_all), bias_s, sink_s)
        oa_s = oa_s.reshape(ns, N_KV_A, GROUP_A, ls, HEAD_DIM_A).transpose(0, 3, 1, 2, 4).reshape(tsm, Q_A)
        oa = (oa_p, oa_s.astype(BF16))

        cb_p, tail_p = _convb_prompt(h, w_dw_p[i], b_dw_b[i][None], ln_b_g[i][None], ln_b_b[i][None], nb_p, seq)
        wt = jnp.where(tap_ok[:, :, None], w_dw_r[i][jnp.clip(jj, 0, CONV_B - 1)], 0.0)
        cb_s, nconv_b_s = _convb_sample(hs, state_conv_b[i], wt, b_dw_b[i][None], ln_b_g[i][None], ln_b_b[i][None])
        cb = (cb_p, cb_s.transpose(1, 0, 2).reshape(tsm, C_B))

        nw = norm_c[i][None]
        nchunk = seq // DN_CHUNK
        prep_p = _dn_prep(h, h, w_cc_p[i], alog_rows[i], dtb_rows[i], qcol=COL_QKVC // V_C,
                          smcol=COL_SMALL // LANE, nbatch=nb_p, nchunk=nchunk, c=DN_CHUNK, lr=DN_CHUNK)
        oc_p, s_p = _dn_seq(prep_p, h, nw, zero_s, zcol=COL_Z // V_C, streams=nb_p, steps=nchunk,
                            c=DN_CHUNK, nsq=1, carry=True)
        s_p = s_p.reshape(nb_p, N_HEADS_C, HEAD_DIM_C, HEAD_DIM_C)
        lead = c_s - ls
        hist = jnp.concatenate([jnp.zeros((ns, lead - (CONV_C - 1), QKV_C), F32), state_conv_c[i],
                                hs[:, :, COL_QKVC:COL_QKVC + QKV_C]], 1).reshape(ns * c_s, QKV_C)
        pad_s = lambda col, wid: jnp.pad(hs[:, :, col:col + wid], ((0, 0), (lead, 0), (0, 0))).reshape(ns * c_s, wid)
        prep_s = _dn_prep(hist, pad_s(COL_SMALL, LANE), w_cc_p[i], alog_rows[i], dtb_rows[i], qcol=0, smcol=0,
                          nbatch=ns // nsq_s, nchunk=1, c=c_s, lr=ls)
        units_s = ns // nsq_s
        streams_s = 2 if units_s % 2 == 0 else 1
        s0_s = state_s_c.astype(F32).reshape(depth, streams_s, units_s // streams_s, nsq_s, N_HEADS_C, HEAD_DIM_C,
                                             HEAD_DIM_C)
        oc_s, s_s = _dn_seq(prep_s, pad_s(COL_Z, V_C), nw, s0_s, zcol=0, streams=streams_s,
                            steps=units_s // streams_s, c=c_s, nsq=nsq_s, carry=False, layer=i)
        s_s = s_s.reshape(ns, N_HEADS_C, HEAD_DIM_C, HEAD_DIM_C)
        oc = (oc_p, oc_s.reshape(ns, c_s, V_C)[:, lead:].reshape(tsm, V_C))

        x1, route, gate, counts = _merge((oa, cb, oc), h, x_all, lw, alpha)

        dest, last_blk, blk_e, n_active, n_blk = _route(route, counts, n_tok)
        xs = _dispatch(dest, last_blk, x1, n_blk * MOE_ROWS)
        yb = _moe(xs, blk_e, n_active, w_up, b_up4, w_down, b_down4, i)

        if i + 1 < depth:
            x_all, h_next = _post_next(dest, yb, gate, x1, p_all[i], lw, alpha, w_in_p[i + 1], b_in_p[i + 1])
        else:
            x_all, _ = _post(dest, yb, gate, x1, p_all[i], lw, alpha)
            h_next = None

        def tail_rows(nrows, col, width):
            return jnp.stack([h[(b + 1) * seq - nrows:(b + 1) * seq, col:col + width] for b in range(nb_p)])

        kp_new = tail_rows(w_buf, COL_KA, KV_A)
        vp_new = tail_rows(w_buf, COL_VA, KV_A)
        ccp_new = tail_rows(CONV_C - 1, COL_QKVC, QKV_C)
        st_p.append((kp_new.reshape(nb_p, w_buf, N_KV_A, HEAD_DIM_A), vp_new.reshape(nb_p, w_buf, N_KV_A, HEAD_DIM_A),
                     tail_p[:, HALO_B - (CONV_B - 1):], ccp_new, s_p))
        ccs_ext = jnp.concatenate([state_conv_c[i], hs[:, :, COL_QKVC:COL_QKVC + QKV_C]], 1)
        st_s.append((k_all[:, ls:], v_all[:, ls:], nconv_b_s, ccs_ext[:, -(CONV_C - 1):], s_s))
        h = h_next

    yp = x_all[:tp].reshape(nb_p, seq, D_MODEL)
    ys = x_all[tp:].reshape(ns, ls, D_MODEL)
    kp, vp, cbp, ccp, scp = [jnp.stack(z) for z in zip(*st_p)]
    ks_, vs_, cbs, ccs, scs = [jnp.stack(z) for z in zip(*st_s)]
    return (yp, ys, kp, vp, cbp, ccp, scp, ks_, vs_, cbs, ccs, scs)
```

```python
import functools
import math

import numpy as np
import jax
import jax.numpy as jnp
from jax import lax
from jax.experimental import pallas as pl
from jax.experimental.pallas import tpu as pltpu

F32 = jnp.float32
BF16 = jnp.bfloat16
HIGHEST = lax.Precision.HIGHEST

D_MODEL = 1024
N_HEADS_A, N_KV_A, HEAD_DIM_A = 8, 2, 64
GROUP_A = N_HEADS_A // N_KV_A
Q_A = N_HEADS_A * HEAD_DIM_A
KV_A = N_KV_A * HEAD_DIM_A
WINDOW = 128
Q_BLOCK = 128
NUM_BUCKETS = 32
MAX_DISTANCE = 128
NEG_INF = -1e30
C_B = 512
CONV_B = 31
N_HEADS_C, HEAD_DIM_C = 4, 128
V_C = N_HEADS_C * HEAD_DIM_C
QKV_C = 3 * V_C
CONV_C = 4
DN_CHUNK = 64
N_BRANCH = 3
IN_SIZES = (Q_A, KV_A, KV_A, 2 * C_B, QKV_C, V_C, N_HEADS_C, N_HEADS_C, N_BRANCH * D_MODEL)
N_EXPERTS = 32
TOP_K = 4
D_FF = 1024
SWIGLU_LIMIT = 7.0
SWIGLU_ALPHA = 1.702
PLE_DIM = 256
LN_EPS = 1e-5
RMS_EPS = 1e-6

LANE = 128
SUBLANE = 8
VMEM_LIMIT = 56 * 1024 * 1024

COL_GATE = 0
COL_GLU = 3 * D_MODEL
COL_QKVC = COL_GLU + 2 * C_B
COL_Z = COL_QKVC + QKV_C
COL_QA = COL_Z + V_C
COL_KA = COL_QA + Q_A
COL_VA = COL_KA + KV_A
COL_SMALL = COL_VA + KV_A
N_IN_PAD = COL_SMALL + LANE
MOE_ROWS = 512
MOE_CHUNKS = 4
HALO_B = 32
HALO_C = 8


def _tile(n, target, mult=SUBLANE):
    best = None
    for t in range(mult, min(n, target) + 1, mult):
        if n % t == 0:
            best = t
    return best if best is not None else n


def _cparams(sem):
    return pltpu.CompilerParams(dimension_semantics=sem, vmem_limit_bytes=VMEM_LIMIT)


def _layer_norm(x, g, b):
    mu = jnp.mean(x, -1, keepdims=True)
    xc = x - mu
    var = jnp.mean(xc * xc, -1, keepdims=True)
    return xc * lax.rsqrt(var + LN_EPS) * g + b


def _sigmoid(x):
    return 1.0 / (1.0 + jnp.exp(-x))


def _round_bf16(x):
    return x.astype(BF16).astype(F32)


def _mm_bias_kernel(x_ref, w_ref, b_ref, o_ref):
    o_ref[...] = jnp.dot(x_ref[...], w_ref[...], preferred_element_type=F32) + b_ref[...]


def _in_proj(x_bf, w_bf, b):
    m, k = x_bf.shape
    n = w_bf.shape[1]
    tm = _tile(m, 1536)
    tn = _tile(n, 1408, LANE)
    return pl.pallas_call(
        _mm_bias_kernel,
        grid=(n // tn, m // tm),
        in_specs=[pl.BlockSpec((tm, k), lambda j, i: (i, 0)),
                  pl.BlockSpec((k, tn), lambda j, i: (0, j)),
                  pl.BlockSpec((1, tn), lambda j, i: (0, j))],
        out_specs=pl.BlockSpec((tm, tn), lambda j, i: (i, j)),
        out_shape=jax.ShapeDtypeStruct((m, n), F32),
        compiler_params=_cparams(("arbitrary", "arbitrary")),
        name="in_proj",
    )(x_bf, w_bf, b)


def _attn_prompt_kernel(sink_ref, q_ref, kp_ref, kc_ref, vp_ref, vc_ref, bias_ref, o_ref):
    nt = (((1,), (1,)), ((), ()))
    q = (q_ref[...] * (HEAD_DIM_A ** -0.5)).astype(BF16)
    lo = lax.broadcasted_iota(jnp.int32, (2 * Q_BLOCK, KV_A), 1) < HEAD_DIM_A

    def placed(prev_ref, cur_ref):
        band = jnp.concatenate([prev_ref[...], cur_ref[...]], 0)
        own = (jnp.where(lo, band, 0.0), jnp.where(lo, 0.0, band))
        moved = (pltpu.roll(own[0], HEAD_DIM_A, 1), pltpu.roll(own[1], HEAD_DIM_A, 1))
        return ((own[0].astype(BF16), moved[0].astype(BF16)), (moved[1].astype(BF16), own[1].astype(BF16)))

    kz = placed(kp_ref, kc_ref)
    vz = placed(vp_ref, vc_ref)
    pairs = []
    for kv in range(N_KV_A):
        heads = range(kv * GROUP_A, (kv + 1) * GROUP_A)
        scores = [lax.dot_general(q[:, (h // 2) * LANE:(h // 2 + 1) * LANE], kz[kv][h % 2], nt,
                                  preferred_element_type=F32) + bias_ref[0, h] for h in heads]
        probs = []
        for h, s in zip(heads, scores):
            sink = sink_ref[h]
            m = jnp.maximum(jnp.max(s, -1, keepdims=True), sink)
            e = jnp.exp(s - m)
            inv = 1.0 / (jnp.sum(e, -1, keepdims=True) + jnp.exp(sink - m))
            probs.append((e * inv).astype(BF16))
        pv = [jnp.dot(p, vz[kv][h % 2], preferred_element_type=F32) for h, p in zip(heads, probs)]
        pairs += [pv[0] + pv[1], pv[2] + pv[3]]
    o_ref[...] = jnp.concatenate(pairs, -1).astype(o_ref.dtype)


def _attn_prompt(h, sinks, bias_tab, nbatch, seq):
    nb = seq // Q_BLOCK
    qcol = COL_QA // Q_A
    kcol = COL_KA // KV_A
    vcol = COL_VA // KV_A

    def cur(c):
        return lambda b, i: (b * nb + i, c)

    def prev(c):
        return lambda b, i: (jnp.maximum(b * nb + i - 1, 0), c)

    return pl.pallas_call(
        _attn_prompt_kernel,
        grid=(nbatch, nb),
        in_specs=[pl.BlockSpec(memory_space=pltpu.SMEM),
                  pl.BlockSpec((Q_BLOCK, Q_A), cur(qcol)),
                  pl.BlockSpec((Q_BLOCK, KV_A), prev(kcol)),
                  pl.BlockSpec((Q_BLOCK, KV_A), cur(kcol)),
                  pl.BlockSpec((Q_BLOCK, KV_A), prev(vcol)),
                  pl.BlockSpec((Q_BLOCK, KV_A), cur(vcol)),
                  pl.BlockSpec((1, N_HEADS_A, Q_BLOCK, 2 * Q_BLOCK), lambda b, i: (jnp.minimum(i, 1), 0, 0, 0))],
        out_specs=pl.BlockSpec((Q_BLOCK, Q_A), lambda b, i: (b * nb + i, 0)),
        out_shape=jax.ShapeDtypeStruct((nbatch * seq, Q_A), BF16),
        compiler_params=_cparams(("arbitrary", "arbitrary")),
        name="attn_prompt",
    )(sinks, h, h, h, h, h, bias_tab)


def _attn_sample_kernel(q_ref, k_ref, v_ref, bias_ref, sink_ref, o_ref):
    q = (q_ref[...] * (HEAD_DIM_A ** -0.5)).astype(BF16)
    s = jnp.einsum("bqd,bkd->bqk", q, k_ref[...].astype(BF16), preferred_element_type=F32) + bias_ref[...]
    sink = sink_ref[...]
    m = jnp.maximum(jnp.max(s, -1, keepdims=True), sink)
    e = jnp.exp(s - m)
    pr = e / (jnp.sum(e, -1, keepdims=True) + jnp.exp(sink - m))
    o_ref[...] = jnp.einsum("bqk,bkd->bqd", pr.astype(BF16), v_ref[...].astype(BF16), preferred_element_type=F32)


def _attn_sample(q, k, v, bias, sink):
    g, r, hd = q.shape
    kk = k.shape[1]
    gb = bias.shape[0]
    return pl.pallas_call(
        _attn_sample_kernel,
        grid=(g // gb,),
        in_specs=[pl.BlockSpec((gb, r, hd), lambda i: (i, 0, 0)),
                  pl.BlockSpec((gb, kk, hd), lambda i: (i, 0, 0)),
                  pl.BlockSpec((gb, kk, hd), lambda i: (i, 0, 0)),
                  pl.BlockSpec((gb, r, kk), lambda i: (0, 0, 0)),
                  pl.BlockSpec((gb, r, 1), lambda i: (0, 0, 0))],
        out_specs=pl.BlockSpec((gb, r, hd), lambda i: (i, 0, 0)),
        out_shape=jax.ShapeDtypeStruct((g, r, hd), F32),
        compiler_params=_cparams(("arbitrary",)),
        name="attn_sample",
    )(q, k, v, bias, sink)


CONV_ROWS = 64


def _convb_prompt_kernel(a_ref, b_ref, w_ref, bias_ref, g_ref, beta_ref, cb_ref, tail_ref, ubuf, shbuf, *, tl):
    @pl.when(pl.program_id(1) == 0)
    def _():
        ubuf[0:HALO_B, :] = jnp.zeros((HALO_B, C_B), F32)

    u = a_ref[...] * _sigmoid(b_ref[...])
    tail_ref[0] = u[tl - HALO_B:tl, :]
    ubuf[HALO_B:HALO_B + tl, :] = _round_bf16(u)
    off = HALO_B - (CONV_B - 1)
    span = tl + HALO_B - SUBLANE
    for s in range(1, SUBLANE):
        shbuf[s - 1, 0:span, :] = ubuf[s:s + span, :]

    def body(r, carry):
        base = pl.multiple_of(r * CONV_ROWS, CONV_ROWS)
        accs = []
        for c in range(C_B // LANE):
            cs = slice(c * LANE, (c + 1) * LANE)
            acc = jnp.zeros((CONV_ROWS, LANE), F32)
            for w in range(CONV_B):
                d = off + w
                start = pl.multiple_of(base + (d // SUBLANE) * SUBLANE, SUBLANE)
                if d % SUBLANE == 0:
                    win = ubuf[pl.ds(start, CONV_ROWS), cs]
                else:
                    win = shbuf[d % SUBLANE - 1, pl.ds(start, CONV_ROWS), cs]
                acc = acc + win * w_ref[w:w + 1, cs]
            accs.append(acc + bias_ref[:, cs])
        y = _layer_norm(jnp.concatenate(accs, -1), g_ref[...], beta_ref[...])
        cb_ref[pl.ds(base, CONV_ROWS), :] = (y * _sigmoid(y)).astype(cb_ref.dtype)
        return carry

    lax.fori_loop(0, tl // CONV_ROWS, body, 0)
    ubuf[0:HALO_B, :] = ubuf[tl:tl + HALO_B, :]


def _convb_prompt(h, w_dw, b_dw, ln_g, ln_b, nbatch, seq):
    tl = _tile(seq, 512, CONV_ROWS)
    nt = seq // tl
    acol = COL_GLU // C_B
    return pl.pallas_call(
        functools.partial(_convb_prompt_kernel, tl=tl),
        grid=(nbatch, nt),
        in_specs=[pl.BlockSpec((tl, C_B), lambda b, t: (b * nt + t, acol)),
                  pl.BlockSpec((tl, C_B), lambda b, t: (b * nt + t, acol + 1)),
                  pl.BlockSpec((HALO_B, C_B), lambda b, t: (0, 0)),
                  pl.BlockSpec((1, C_B), lambda b, t: (0, 0)),
                  pl.BlockSpec((1, C_B), lambda b, t: (0, 0)),
                  pl.BlockSpec((1, C_B), lambda b, t: (0, 0))],
        out_specs=[pl.BlockSpec((tl, C_B), lambda b, t: (b * nt + t, 0)),
                   pl.BlockSpec((1, HALO_B, C_B), lambda b, t: (b, 0, 0))],
        out_shape=[jax.ShapeDtypeStruct((nbatch * seq, C_B), BF16),
                   jax.ShapeDtypeStruct((nbatch, HALO_B, C_B), F32)],
        scratch_shapes=[pltpu.VMEM((HALO_B + tl, C_B), F32),
                        pltpu.VMEM((SUBLANE - 1, HALO_B + tl, C_B), F32)],
        compiler_params=_cparams(("arbitrary", "arbitrary")),
        name="convb_prompt",
    )(h, h, w_dw, b_dw, ln_g, ln_b)


def _convb_sample_kernel(a_ref, b_ref, st_ref, wt_ref, bias_ref, g_ref, beta_ref, cb_ref, nst_ref, ext, *, ls, wrows):
    ns = CONV_B - 1
    ext[:, ns - ns % SUBLANE:wrows, :] = jnp.zeros((ext.shape[0], wrows - (ns - ns % SUBLANE), C_B), F32)
    ext[:, 0:ns, :] = st_ref[...]
    ext[:, ns:ns + ls, :] = a_ref[...] * _sigmoid(b_ref[...])
    nst_ref[...] = ext[:, ls:ls + ns, :]
    win = _round_bf16(ext[...])
    for t in range(ls):
        y = jnp.sum(win * wt_ref[t][None], axis=1) + bias_ref[...]
        y = _layer_norm(y, g_ref[...], beta_ref[...])
        cb_ref[t] = (y * _sigmoid(y)).astype(cb_ref.dtype)


def _convb_sample(hs, state, wt, b_dw, ln_g, ln_b):
    s, ls, _ = hs.shape
    wrows = wt.shape[1]
    sb = _tile(s, 16, 1)
    acol = COL_GLU // C_B
    ns = CONV_B - 1
    return pl.pallas_call(
        functools.partial(_convb_sample_kernel, ls=ls, wrows=wrows),
        grid=(s // sb,),
        in_specs=[pl.BlockSpec((sb, ls, C_B), lambda i: (i, 0, acol)),
                  pl.BlockSpec((sb, ls, C_B), lambda i: (i, 0, acol + 1)),
                  pl.BlockSpec((sb, ns, C_B), lambda i: (i, 0, 0)),
                  pl.BlockSpec((ls, wrows, C_B), lambda i: (0, 0, 0)),
                  pl.BlockSpec((1, C_B), lambda i: (0, 0)),
                  pl.BlockSpec((1, C_B), lambda i: (0, 0)),
                  pl.BlockSpec((1, C_B), lambda i: (0, 0))],
        out_specs=[pl.BlockSpec((ls, sb, C_B), lambda i: (0, i, 0)),
                   pl.BlockSpec((sb, ns, C_B), lambda i: (i, 0, 0))],
        out_shape=[jax.ShapeDtypeStruct((ls, s, C_B), BF16),
                   jax.ShapeDtypeStruct((s, ns, C_B), F32)],
        scratch_shapes=[pltpu.VMEM((sb, wrows, C_B), F32)],
        compiler_params=_cparams(("arbitrary",)),
        name="convb_sample",
    )(hs, hs, state, wt, b_dw, ln_g, ln_b)


def _softplus(x):
    return jnp.maximum(x, 0.0) + jnp.log(1.0 + jnp.exp(-jnp.abs(x)))


DN_UNIT = DN_CHUNK
DN_STACK = N_HEADS_C * DN_UNIT
DN_PAIR = 2 * DN_UNIT
DN_GL_ROWS = 16


def _dn_prep_kernel(q_ref, k_ref, v_ref, qp_ref, kp_ref, vp_ref, sm_ref, wc_ref, alog_ref, dtb_ref,
                    u_ref, w_ref, qd_ref, kdt_ref, qk_ref, gl_ref, cbuf, *, c, lr, ups):
    rows = ups * DN_UNIT
    halo = jnp.concatenate([qp_ref[...], kp_ref[...], vp_ref[...]], -1)
    cbuf[0:HALO_C, :] = _round_bf16(jnp.where(pl.program_id(1) == 0, 0.0, halo))
    cbuf[HALO_C:HALO_C + rows, 0:V_C] = _round_bf16(q_ref[...])
    cbuf[HALO_C:HALO_C + rows, V_C:2 * V_C] = _round_bf16(k_ref[...])
    cbuf[HALO_C:HALO_C + rows, 2 * V_C:3 * V_C] = _round_bf16(v_ref[...])
    off = HALO_C - (CONV_C - 1)
    y = cbuf[off:off + rows, :] * wc_ref[0:1, :]
    for w in range(1, CONV_C):
        y = y + cbuf[off + w:off + w + rows, :] * wc_ref[w:w + 1, :]
    y = y * _sigmoid(y)
    outs = (u_ref, w_ref, qd_ref, kdt_ref, qk_ref, gl_ref)
    _dn_prep_tables(y, sm_ref[...], alog_ref[...], dtb_ref[...], outs, ups=ups, c=c, lr=lr)


def _dn_prep_tables(y, small, alog, dtb, outs, *, ups, c, lr):
    u_ref, w_ref, qd_ref, kdt_ref, qk_ref, gl_ref = outs
    nt = (((1,), (1,)), ((), ()))
    hd = HEAD_DIM_C
    beta_f = _sigmoid(small)
    g_f = -jnp.exp(alog) * _softplus(small + dtb)
    if lr < c:
        live = lax.broadcasted_iota(jnp.int32, beta_f.shape, 0) % c >= c - lr
        beta_f = jnp.where(live, beta_f, 0.0)
        g_f = jnp.where(live, g_f, 0.0)

    r2 = lax.broadcasted_iota(jnp.int32, (2 * DN_UNIT, DN_UNIT), 0)
    c2 = lax.broadcasted_iota(jnp.int32, (2 * DN_UNIT, DN_UNIT), 1)
    rr = jnp.where(r2 >= DN_UNIT, r2 - DN_UNIT, r2)
    same2 = (rr // c) == (c2 // c)
    summer = jnp.where(same2 & ((r2 >= DN_UNIT) | (rr >= c2)), 1.0, 0.0).astype(F32)

    ri = lax.broadcasted_iota(jnp.int32, (DN_PAIR, DN_PAIR), 0)
    ci = lax.broadcasted_iota(jnp.int32, (DN_PAIR, DN_PAIR), 1)
    same = (ri // c) == (ci // c)
    tri = same & (ri >= ci)
    stri = same & (ri > ci)
    eye = jnp.where(ri == ci, 1.0, 0.0).astype(F32)
    items = [(slot, pair) for slot in range(ups) for pair in range(N_HEADS_C // 2)]
    st = []
    for slot, pair in items:
        unit_rows = slice(slot * DN_UNIT, (slot + 1) * DN_UNIT)
        yu = y[unit_rows]
        if pair == 0:
            cg = jnp.dot(summer, g_f[unit_rows], precision=HIGHEST, preferred_element_type=F32)
        heads = (2 * pair, 2 * pair + 1)
        stack = lambda first: jnp.concatenate([yu[:, first + h * hd:first + (h + 1) * hd] for h in heads], 0)
        cols = lambda x, first: jnp.concatenate([x[:, first + h:first + h + 1] for h in heads], 0)
        qs, ks, vs = stack(0), stack(V_C), stack(2 * V_C)
        qn = qs * lax.rsqrt(jnp.sum(qs * qs, -1, keepdims=True) + RMS_EPS) * (hd ** -0.5)
        kn = ks * lax.rsqrt(jnp.sum(ks * ks, -1, keepdims=True) + RMS_EPS)
        beta_s = cols(beta_f[unit_rows], 0)
        gcs = cols(cg[0:DN_UNIT], N_HEADS_C)
        gls = cols(cg[DN_UNIT:2 * DN_UNIT], N_HEADS_C)
        grow = jnp.broadcast_to(gcs, (DN_PAIR, LANE)).T[0:1, :]
        decay = jnp.where(tri, jnp.exp(jnp.where(tri, gcs - grow, 0.0)), 0.0)
        kb = kn.astype(BF16)
        kk = lax.dot_general(kb, kb, nt, preferred_element_type=F32)
        nmat = jnp.where(stri, -(beta_s * kk * decay), 0.0)
        egc = jnp.exp(gcs)
        st.append(dict(qn=qn, kn=kn, kb=kb, decay=decay, nmat=nmat, egc=egc, gcs=gcs, gls=gls,
                       rhs=jnp.concatenate([vs * beta_s, kn * (beta_s * egc)], -1).astype(BF16)))

    invs = [eye + s["nmat"] for s in st]
    nbs = [s["nmat"].astype(BF16) for s in st]
    for _ in range(int(math.log2(c)) - 1):
        nbs = [jnp.dot(nb, nb, preferred_element_type=F32).astype(BF16) for nb in nbs]
        invs = [inv + jnp.dot(inv.astype(BF16), nb, preferred_element_type=F32) for inv, nb in zip(invs, nbs)]
    sols = [jnp.dot(inv.astype(BF16), s["rhs"], preferred_element_type=F32) for inv, s in zip(invs, st)]

    gl_rows = {slot: [] for slot in range(ups)}
    for (slot, pair), s, sol in zip(items, st, sols):
        rows = slice(pair * DN_PAIR, (pair + 1) * DN_PAIR)
        u_ref[slot, rows, :] = sol[:, :hd]
        w_ref[slot, rows, :] = sol[:, hd:].astype(BF16)
        qk = jnp.where(tri, lax.dot_general(s["qn"].astype(BF16), s["kb"], nt, preferred_element_type=F32)
                       * s["decay"], 0.0)
        qk_ref[slot, rows, :] = qk.astype(BF16)
        qd_ref[slot, rows, :] = (s["qn"] * s["egc"]).astype(BF16)
        kdt_ref[slot, :, rows] = (s["kn"] * jnp.exp(s["gls"] - s["gcs"])).T.astype(BF16)
        glb = jnp.broadcast_to(jnp.exp(s["gls"]), (DN_PAIR, LANE))
        gl_rows[slot] += [glb[g * c:g * c + 1, :] for g in range(DN_PAIR // c)]
    groups = DN_STACK // c
    for slot in range(ups):
        if groups < DN_GL_ROWS:
            gl_rows[slot].append(jnp.ones((DN_GL_ROWS - groups, LANE), F32))
        gl_ref[slot] = jnp.concatenate(gl_rows[slot], 0)


def _dn_prep(src, small_src, wc, alog_row, dtb_row, *, qcol, smcol, nbatch, nchunk, c, lr):
    nu = nbatch * nchunk
    split = nbatch if nchunk == 1 else nchunk
    ups = max(u for u in (1, 2, 4) if split % u == 0)
    if nchunk == 1:
        nbatch //= ups
    else:
        nchunk //= ups
    rows = ups * DN_UNIT
    per8 = rows // SUBLANE

    def cur(col):
        return pl.BlockSpec((rows, V_C), lambda b, n: (b * nchunk + n, col))

    def prev(col):
        return pl.BlockSpec((HALO_C, V_C), lambda b, n: (jnp.maximum((b * nchunk + n) * per8 - 1, 0), col))

    unit = lambda r, cdim: pl.BlockSpec((ups, r, cdim), lambda b, n: (b * nchunk + n, 0, 0))
    return pl.pallas_call(
        functools.partial(_dn_prep_kernel, c=c, lr=lr, ups=ups),
        grid=(nbatch, nchunk),
        in_specs=[cur(qcol), cur(qcol + 1), cur(qcol + 2), prev(qcol), prev(qcol + 1), prev(qcol + 2),
                  pl.BlockSpec((rows, LANE), lambda b, n: (b * nchunk + n, smcol)),
                  pl.BlockSpec((SUBLANE, QKV_C), lambda b, n: (0, 0)),
                  pl.BlockSpec((1, LANE), lambda b, n: (0, 0)),
                  pl.BlockSpec((1, LANE), lambda b, n: (0, 0))],
        out_specs=[unit(DN_STACK, HEAD_DIM_C), unit(DN_STACK, HEAD_DIM_C), unit(DN_STACK, HEAD_DIM_C),
                   unit(HEAD_DIM_C, DN_STACK), unit(DN_STACK, DN_PAIR), unit(DN_GL_ROWS, LANE)],
        out_shape=[jax.ShapeDtypeStruct((nu, DN_STACK, HEAD_DIM_C), F32),
                   jax.ShapeDtypeStruct((nu, DN_STACK, HEAD_DIM_C), BF16),
                   jax.ShapeDtypeStruct((nu, DN_STACK, HEAD_DIM_C), BF16),
                   jax.ShapeDtypeStruct((nu, HEAD_DIM_C, DN_STACK), BF16),
                   jax.ShapeDtypeStruct((nu, DN_STACK, DN_PAIR), BF16),
                   jax.ShapeDtypeStruct((nu, DN_GL_ROWS, LANE), F32)],
        scratch_shapes=[pltpu.VMEM((HALO_C + rows, QKV_C), F32)],
        compiler_params=_cparams(("arbitrary", "arbitrary")),
        name="deltanet_prep",
    )(src, src, src, src, src, src, small_src, wc, alog_row, dtb_row)


def _dn_seq_kernel(u_ref, w_ref, qd_ref, kdt_ref, qk_ref, gl_ref, *rest, c, nsq, streams, carry):
    z_refs = rest[:streams]
    nw_ref, s0_ref, o_ref, sout_ref, sbuf = rest[streams:]
    groups = DN_STACK // c
    hd = HEAD_DIM_C
    span = groups * hd
    step = pl.program_id(0)

    def load_state():
        for j in range(streams):
            for g in range(groups):
                sbuf[j * span + g * hd:j * span + (g + 1) * hd, :] = s0_ref[j, 0, g % nsq, g // nsq]

    def store_state():
        for j in range(streams):
            for g in range(groups):
                sout_ref[j, 0, g % nsq, g // nsq] = sbuf[j * span + g * hd:j * span + (g + 1) * hd, :]

    if carry:
        pl.when(step == 0)(load_state)
    else:
        load_state()

    col_group = lax.broadcasted_iota(jnp.int32, (hd, DN_STACK), 1) // c
    vbs, o_inter = [], []
    for j in range(streams):
        u = u_ref[j, 0]
        w = w_ref[j, 0]
        qd = qd_ref[j, 0]
        v_parts, o_parts = [], []
        for g in range(groups):
            rows = slice(g * c, (g + 1) * c)
            s_b = sbuf[j * span + g * hd:j * span + (g + 1) * hd, :].astype(BF16)
            r = jnp.dot(jnp.concatenate([w[rows], qd[rows]], 0), s_b, preferred_element_type=F32)
            v_parts.append(u[rows] - r[:c])
            o_parts.append(r[c:])
        vbs.append(jnp.concatenate(v_parts, 0).astype(BF16))
        o_inter.append(jnp.concatenate(o_parts, 0))

    o_all = []
    for j in range(streams):
        qk = qk_ref[j, 0]
        o_all.append(o_inter[j] + jnp.concatenate(
            [jnp.dot(qk[p * DN_PAIR:(p + 1) * DN_PAIR], vbs[j][p * DN_PAIR:(p + 1) * DN_PAIR],
                     preferred_element_type=F32) for p in range(DN_STACK // DN_PAIR)], 0))

    for j in range(streams):
        kdt = kdt_ref[j, 0]
        k_big = jnp.concatenate([jnp.where(col_group == g, kdt, jnp.zeros_like(kdt)) for g in range(groups)], 0)
        gl = gl_ref[j, 0]
        scale = jnp.concatenate([jnp.broadcast_to(gl[g:g + 1, :], (hd, LANE)) for g in range(groups)], 0)
        sbuf[j * span:(j + 1) * span, :] = (sbuf[j * span:(j + 1) * span, :] * scale
                                             + jnp.dot(k_big, vbs[j], preferred_element_type=F32))

    for j in range(streams):
        outs = []
        for h in range(N_HEADS_C):
            oh = o_all[j][h * DN_UNIT:(h + 1) * DN_UNIT]
            on = oh * lax.rsqrt(jnp.mean(oh * oh, -1, keepdims=True) + RMS_EPS) * nw_ref[...]
            zh = z_refs[j][:, h * hd:(h + 1) * hd]
            outs.append(on * (zh * _sigmoid(zh)))
        o_ref[j] = jnp.concatenate(outs, -1).astype(o_ref.dtype)

    if carry:
        pl.when(step == pl.num_programs(0) - 1)(store_state)
    else:
        store_state()


def _dn_seq(prep, z_src, nw, s0, *, zcol, streams, steps, c, nsq, carry, layer=None):
    groups = DN_STACK // c
    prep = [t.reshape((streams, steps) + t.shape[1:]) for t in prep]
    unit = lambda r, cdim: pl.BlockSpec((streams, 1, r, cdim), lambda i: (0, i, 0, 0))
    sdim = (streams, 1, nsq, N_HEADS_C, HEAD_DIM_C, HEAD_DIM_C)
    state = pl.BlockSpec(sdim, (lambda i: (0, 0, 0, 0, 0, 0)) if carry else (lambda i: (0, i, 0, 0, 0, 0)))
    state_in, state_shape = state, s0.shape
    if layer is not None:
        state_in = pl.BlockSpec((None,) + sdim, lambda i: (layer, 0, 0 if carry else i, 0, 0, 0, 0))
        state_shape = s0.shape[1:]
    z_specs = [pl.BlockSpec((DN_UNIT, V_C), functools.partial(lambda i, j: (j * steps + i, zcol), j=j))
               for j in range(streams)]
    o, s_out = pl.pallas_call(
        functools.partial(_dn_seq_kernel, c=c, nsq=nsq, streams=streams, carry=carry),
        grid=(steps,),
        in_specs=[unit(DN_STACK, HEAD_DIM_C), unit(DN_STACK, HEAD_DIM_C), unit(DN_STACK, HEAD_DIM_C),
                  unit(HEAD_DIM_C, DN_STACK), unit(DN_STACK, DN_PAIR), unit(DN_GL_ROWS, LANE)]
                 + z_specs + [pl.BlockSpec((1, HEAD_DIM_C), lambda i: (0, 0)), state_in],
        out_specs=[pl.BlockSpec((streams, DN_UNIT, V_C), lambda i: (0, i, 0)), state],
        out_shape=[jax.ShapeDtypeStruct((streams, steps * DN_UNIT, V_C), BF16),
                   jax.ShapeDtypeStruct(state_shape, F32)],
        scratch_shapes=[pltpu.VMEM((streams * groups * HEAD_DIM_C, HEAD_DIM_C), F32)],
        compiler_params=_cparams(("arbitrary",)),
        name="deltanet_seq",
    )(*prep, *([z_src] * streams), nw, s0)
    return o.reshape(streams * steps * DN_UNIT, V_C), s_out


def _merge_kernel(oap_ref, oas_ref, cbp_ref, cbs_ref, ocp_ref, ocs_ref, ga_ref, gb_ref, gc_ref, x_ref,
                  woa_ref, wob_ref, woc_ref, wout_ref, g1_ref, b1_ref, wr_ref, br_ref,
                  x1_ref, route_ref, gate_ref, cnt_ref, carry, *, alpha, prompt_tiles):
    @pl.when(pl.program_id(0) == 0)
    def _():
        carry[...] = jnp.zeros(carry.shape, F32)

    is_prompt = pl.program_id(0) < prompt_tiles
    pick = lambda p_ref, s_ref: jnp.where(is_prompt, p_ref[...], s_ref[...])
    br_a = jnp.dot(pick(oap_ref, oas_ref), woa_ref[...], preferred_element_type=F32)
    br_b = jnp.dot(pick(cbp_ref, cbs_ref), wob_ref[...], preferred_element_type=F32)
    br_c = jnp.dot(pick(ocp_ref, ocs_ref), woc_ref[...], preferred_element_type=F32)
    mixin = _sigmoid(ga_ref[...]) * br_a + _sigmoid(gb_ref[...]) * br_b + _sigmoid(gc_ref[...]) * br_c
    mix = jnp.dot(mixin.astype(BF16), wout_ref[...], preferred_element_type=F32)
    x1 = _layer_norm(alpha * x_ref[...] + mix, g1_ref[...], b1_ref[...])
    x1_ref[...] = x1
    logits = jnp.dot(x1.astype(BF16), wr_ref[...], preferred_element_type=F32) + br_ref[...]
    lane = lax.broadcasted_iota(jnp.int32, logits.shape, 1)
    logits = jnp.where(lane < N_EXPERTS, logits, -jnp.inf)
    route = jnp.zeros(logits.shape, jnp.int32)
    val_out = jnp.zeros(logits.shape, F32)
    member = jnp.zeros(logits.shape, F32)
    sels = []
    top0 = None
    den = None
    for k in range(TOP_K):
        m = jnp.max(logits, -1, keepdims=True)
        sel = jnp.min(jnp.where(logits == m, lane, LANE), -1, keepdims=True)
        if k == 0:
            top0 = m
            e = jnp.ones_like(m)
            den = e
        else:
            e = jnp.exp(m - top0)
            den = den + e
        hit = lane == sel
        route = jnp.where(lane == k, sel, route)
        val_out = jnp.where(lane == k, e, val_out)
        member = jnp.where(hit, 1.0, member)
        logits = jnp.where(hit, -jnp.inf, logits)
        sels.append(sel)
    gate_ref[...] = val_out / den
    tm = logits.shape[0]
    earlier = lax.broadcasted_iota(jnp.int32, (tm, tm), 0) > lax.broadcasted_iota(jnp.int32, (tm, tm), 1)
    cum = jnp.dot(jnp.where(earlier, 1.0, 0.0).astype(BF16), member.astype(BF16),
                  preferred_element_type=F32) + carry[...]
    for k in range(TOP_K):
        rank = jnp.sum(jnp.where(lane == sels[k], cum, 0.0), -1, keepdims=True)
        route = jnp.where(lane == TOP_K + k, rank.astype(jnp.int32), route)
    route_ref[...] = route
    carry[...] = carry[...] + jnp.sum(member, axis=0, keepdims=True)
    cnt_ref[...] = carry[...]


def _merge(branches, h, x, lw, alpha):
    m = x.shape[0]
    tp = branches[0][0].shape[0]
    tm = _tile(math.gcd(tp, m - tp), 512)
    np_tiles = tp // tm
    row = lambda i: (i, 0)
    full = lambda i: (0, 0)
    gcol = COL_GATE // D_MODEL
    p_spec = pl.BlockSpec((tm, Q_A), lambda i: (jnp.minimum(i, np_tiles - 1), 0))
    s_spec = pl.BlockSpec((tm, Q_A), lambda i: (jnp.maximum(i - np_tiles, 0), 0))
    return pl.pallas_call(
        functools.partial(_merge_kernel, alpha=alpha, prompt_tiles=np_tiles),
        grid=(m // tm,),
        in_specs=[p_spec, s_spec, p_spec, s_spec, p_spec, s_spec,
                  pl.BlockSpec((tm, D_MODEL), lambda i: (i, gcol)),
                  pl.BlockSpec((tm, D_MODEL), lambda i: (i, gcol + 1)),
                  pl.BlockSpec((tm, D_MODEL), lambda i: (i, gcol + 2)),
                  pl.BlockSpec((tm, D_MODEL), row),
                  pl.BlockSpec((Q_A, D_MODEL), full), pl.BlockSpec((C_B, D_MODEL), full),
                  pl.BlockSpec((V_C, D_MODEL), full), pl.BlockSpec((D_MODEL, D_MODEL), full),
                  pl.BlockSpec((1, D_MODEL), full), pl.BlockSpec((1, D_MODEL), full),
                  pl.BlockSpec((D_MODEL, LANE), full), pl.BlockSpec((1, LANE), full)],
        out_specs=[pl.BlockSpec((tm, D_MODEL), row), pl.BlockSpec((tm, LANE), row),
                   pl.BlockSpec((tm, LANE), row), pl.BlockSpec((1, LANE), full)],
        out_shape=[jax.ShapeDtypeStruct((m, D_MODEL), F32), jax.ShapeDtypeStruct((m, LANE), jnp.int32),
                   jax.ShapeDtypeStruct((m, LANE), F32), jax.ShapeDtypeStruct((1, LANE), F32)],
        scratch_shapes=[pltpu.VMEM((1, LANE), F32)],
        compiler_params=_cparams(("arbitrary",)),
        name="merge_router",
    )(*branches[0], *branches[1], *branches[2], h, h, h, x, lw["w_o_a"], lw["w_o_b"], lw["w_o_c"], lw["w_out"],
      lw["ln1_g"], lw["ln1_b"], lw["w_router"], lw["b_router"])


def _row_copy(src, src_row, dst, dst_row, sem):
    return pltpu.make_async_copy(src.at[pl.ds(src_row, 1)], dst.at[pl.ds(dst_row, 1)], sem)


def _dispatch_kernel(dest_ref, last_ref, x_ref, xs_ref, zbuf, sem, *, tt, n_blk):
    @pl.when(pl.program_id(0) == 0)
    def _():
        zbuf[...] = jnp.zeros(zbuf.shape, F32)

        def zero_block(first_row):
            return pltpu.make_async_copy(zbuf, xs_ref.at[pl.ds(pl.multiple_of(first_row, MOE_ROWS), MOE_ROWS)], sem)

        def start_tail(b, carry):
            zero_block(b * MOE_ROWS).start()
            return carry

        def wait_tail(b, carry):
            zero_block(0).wait()
            return carry

        n_active = last_ref[N_EXPERTS]
        for e in range(N_EXPERTS):
            zero_block(last_ref[e]).start()
        lax.fori_loop(n_active, n_blk, start_tail, 0)
        for e in range(N_EXPERTS):
            zero_block(0).wait()
        lax.fori_loop(n_active, n_blk, wait_tail, 0)

    def body(t, carry):
        for k in range(TOP_K):
            _row_copy(x_ref, t, xs_ref, dest_ref[t * TOP_K + k], sem).start(priority=k % 2)
        return carry

    lax.fori_loop(0, tt, body, 0, unroll=4)
    for k in range(TOP_K):
        pltpu.make_async_copy(x_ref, xs_ref.at[pl.ds(0, tt)], sem).wait()


def _dispatch(dest, last_blk, x1, n_rows):
    m = x1.shape[0]
    tt = _tile(m, 512)
    return pl.pallas_call(
        functools.partial(_dispatch_kernel, tt=tt, n_blk=n_rows // MOE_ROWS),
        grid=(m // tt,),
        in_specs=[pl.BlockSpec((tt * TOP_K,), lambda i: (i,), memory_space=pltpu.SMEM),
                  pl.BlockSpec(memory_space=pltpu.SMEM),
                  pl.BlockSpec((tt, D_MODEL), lambda i: (i, 0))],
        out_specs=pl.BlockSpec(memory_space=pl.ANY),
        out_shape=jax.ShapeDtypeStruct((n_rows, D_MODEL), F32),
        scratch_shapes=[pltpu.VMEM((MOE_ROWS, D_MODEL), F32), pltpu.SemaphoreType.DMA(())],
        compiler_params=pltpu.CompilerParams(dimension_semantics=("arbitrary",), vmem_limit_bytes=VMEM_LIMIT,
                                             disable_bounds_checks=True),
        name="moe_dispatch",
    )(dest, last_blk, x1)


def _moe_kernel(be_ref, na_ref, x_ref, wu_ref, bu_ref, wd_ref, bd_ref, o_ref, wu_b, wd_b):
    i = pl.program_id(0)
    active = i < na_ref[0]
    new_expert = jnp.logical_or(i == 0, be_ref[i] != be_ref[jnp.maximum(i - 1, 0)])

    @pl.when(jnp.logical_and(active, new_expert))
    def _():
        wu_b[...] = wu_ref[0, 0].astype(BF16)
        wd_b[...] = wd_ref[0, 0].astype(BF16)

    @pl.when(active)
    def _():
        x = x_ref[...].astype(BF16)
        cw = D_FF // MOE_CHUNKS
        acc = None
        for j in range(MOE_CHUNKS):
            gcols = slice(j * cw, (j + 1) * cw)
            ucols = slice(D_FF + j * cw, D_FF + (j + 1) * cw)
            gt = jnp.dot(x, wu_b[:, gcols], preferred_element_type=F32) + bu_ref[0, 0][:, gcols]
            up = jnp.dot(x, wu_b[:, ucols], preferred_element_type=F32) + bu_ref[0, 0][:, ucols]
            gt = jnp.minimum(gt, SWIGLU_LIMIT)
            up = jnp.clip(up, -SWIGLU_LIMIT, SWIGLU_LIMIT)
            act = ((up + 1.0) * gt * _sigmoid(SWIGLU_ALPHA * gt)).astype(BF16)
            part = jnp.dot(act, wd_b[gcols, :], preferred_element_type=F32)
            acc = part if acc is None else acc + part
        o_ref[...] = acc + bd_ref[0, 0]

    @pl.when(jnp.logical_not(active))
    def _():
        o_ref[...] = jnp.zeros(o_ref.shape, F32)


def _moe(xs, blk_e, n_active, w_up, b_up, w_down, b_down, layer):
    nblk = xs.shape[0] // MOE_ROWS

    def blk(i, na):
        return jnp.minimum(i, jnp.maximum(na[0] - 1, 0))

    weights = lambda i, be, na: (layer, be[blk(i, na)], 0, 0)
    grid_spec = pltpu.PrefetchScalarGridSpec(
        num_scalar_prefetch=2,
        grid=(nblk,),
        in_specs=[pl.BlockSpec((MOE_ROWS, D_MODEL), lambda i, be, na: (blk(i, na), 0)),
                  pl.BlockSpec((1, 1, D_MODEL, 2 * D_FF), weights),
                  pl.BlockSpec((1, 1, 1, 2 * D_FF), weights),
                  pl.BlockSpec((1, 1, D_FF, D_MODEL), weights),
                  pl.BlockSpec((1, 1, 1, D_MODEL), weights)],
        out_specs=pl.BlockSpec((MOE_ROWS, D_MODEL), lambda i, be, na: (i, 0)),
        scratch_shapes=[pltpu.VMEM((D_MODEL, 2 * D_FF), BF16), pltpu.VMEM((D_FF, D_MODEL), BF16)],
    )
    return pl.pallas_call(
        _moe_kernel,
        grid_spec=grid_spec,
        out_shape=jax.ShapeDtypeStruct((nblk * MOE_ROWS, D_MODEL), F32),
        compiler_params=_cparams(("arbitrary",)),
        name="moe_experts",
    )(blk_e, n_active, xs, w_up, b_up, w_down, b_down)

def _post_kernel(dest_ref, yb_ref, gate_ref, x1_ref, p_ref, wg_ref, wp_ref, g2_ref, b2_ref, x_ref, xb_ref,
                 ybuf, sem, *, alpha, tm):
    def body(t, carry):
        for k in range(TOP_K):
            _row_copy(yb_ref, dest_ref[t * TOP_K + k], ybuf.at[k], t, sem).start(priority=k % 2)
        return carry

    lax.fori_loop(0, tm, body, 0, unroll=4)
    for k in range(TOP_K):
        pltpu.make_async_copy(yb_ref.at[pl.ds(0, tm)], ybuf.at[k], sem).wait()

    gate = gate_ref[...]
    ffn = gate[:, 0:1] * ybuf[0]
    for k in range(1, TOP_K):
        ffn = ffn + gate[:, k:k + 1] * ybuf[k]
    x2 = _layer_norm(alpha * x1_ref[...] + ffn, g2_ref[...], b2_ref[...])
    gl = jnp.dot(x2.astype(BF16), wg_ref[...], preferred_element_type=F32)
    pe = jnp.dot(p_ref[...].astype(BF16), wp_ref[...], preferred_element_type=F32)
    xo = x2 + _sigmoid(gl) * pe
    x_ref[...] = xo
    xb_ref[...] = xo.astype(BF16)


def _post(dest, yb, gate, x1, p, lw, alpha):
    m = x1.shape[0]
    tm = _tile(m, 512)
    row = lambda i: (i, 0)
    full = lambda i: (0, 0)
    return pl.pallas_call(
        functools.partial(_post_kernel, alpha=alpha, tm=tm),
        grid=(m // tm,),
        in_specs=[pl.BlockSpec((tm * TOP_K,), lambda i: (i,), memory_space=pltpu.SMEM),
                  pl.BlockSpec(memory_space=pl.ANY),
                  pl.BlockSpec((tm, LANE), row),
                  pl.BlockSpec((tm, D_MODEL), row), pl.BlockSpec((tm, PLE_DIM), row),
                  pl.BlockSpec((D_MODEL, D_MODEL), full), pl.BlockSpec((PLE_DIM, D_MODEL), full),
                  pl.BlockSpec((1, D_MODEL), full), pl.BlockSpec((1, D_MODEL), full)],
        out_specs=[pl.BlockSpec((tm, D_MODEL), row), pl.BlockSpec((tm, D_MODEL), row)],
        out_shape=[jax.ShapeDtypeStruct((m, D_MODEL), F32), jax.ShapeDtypeStruct((m, D_MODEL), BF16)],
        scratch_shapes=[pltpu.VMEM((TOP_K, tm, D_MODEL), F32), pltpu.SemaphoreType.DMA(())],
        compiler_params=pltpu.CompilerParams(dimension_semantics=("arbitrary",), vmem_limit_bytes=VMEM_LIMIT,
                                             disable_bounds_checks=True),
        name="combine_ple",
    )(dest, yb, gate, x1, p, lw["w_ple_gate"], lw["w_ple_proj"], lw["ln2_g"], lw["ln2_b"])


def _post_next_kernel(dfirst_ref, dnext_ref, yb_ref, gate_ref, x1_ref, p_ref, wg_ref, wp_ref, g2_ref, b2_ref,
                      win_ref, bin_ref, x_ref, h_ref, ybuf, sem, *, alpha, tm, n_tiles, tn):
    i = pl.program_id(0)
    slot = i % 2

    def issue(dest_ref, to_slot):
        def body(t, carry):
            for k in range(TOP_K):
                _row_copy(yb_ref, dest_ref[t * TOP_K + k], ybuf.at[to_slot, k], t,
                          sem.at[to_slot]).start(priority=k % 2)
            return carry

        lax.fori_loop(0, tm, body, 0, unroll=4)

    @pl.when(i == 0)
    def _():
        issue(dfirst_ref, 0)

    @pl.when(i + 1 < n_tiles)
    def _():
        issue(dnext_ref, 1 - slot)

    for k in range(TOP_K):
        pltpu.make_async_copy(yb_ref.at[pl.ds(0, tm)], ybuf.at[slot, k], sem.at[slot]).wait()

    gate = gate_ref[...]
    ffn = gate[:, 0:1] * ybuf[slot, 0]
    for k in range(1, TOP_K):
        ffn = ffn + gate[:, k:k + 1] * ybuf[slot, k]
    x2 = _layer_norm(alpha * x1_ref[...] + ffn, g2_ref[...], b2_ref[...])
    gl = jnp.dot(x2.astype(BF16), wg_ref[...], preferred_element_type=F32)
    pe = jnp.dot(p_ref[...].astype(BF16), wp_ref[...], preferred_element_type=F32)
    xo = x2 + _sigmoid(gl) * pe
    x_ref[...] = xo
    xb = xo.astype(BF16)
    for j in range(N_IN_PAD // tn):
        cs = slice(j * tn, (j + 1) * tn)
        h_ref[:, cs] = jnp.dot(xb, win_ref[:, cs], preferred_element_type=F32) + bin_ref[:, cs]


def _post_next(dest, yb, gate, x1, p, lw, alpha, w_in_next, b_in_next):
    m = x1.shape[0]
    tm = _tile(m, 256)
    n_tiles = m // tm
    tn = _tile(N_IN_PAD, 1408, LANE)
    row = lambda i: (i, 0)
    full = lambda i: (0, 0)
    once = pl.Buffered(1)
    return pl.pallas_call(
        functools.partial(_post_next_kernel, alpha=alpha, tm=tm, n_tiles=n_tiles, tn=tn),
        grid=(n_tiles,),
        in_specs=[pl.BlockSpec((tm * TOP_K,), lambda i: (0,), memory_space=pltpu.SMEM),
                  pl.BlockSpec((tm * TOP_K,), lambda i: (jnp.minimum(i + 1, n_tiles - 1),), memory_space=pltpu.SMEM),
                  pl.BlockSpec(memory_space=pl.ANY),
                  pl.BlockSpec((tm, LANE), row),
                  pl.BlockSpec((tm, D_MODEL), row), pl.BlockSpec((tm, PLE_DIM), row),
                  pl.BlockSpec((D_MODEL, D_MODEL), full, pipeline_mode=once),
                  pl.BlockSpec((PLE_DIM, D_MODEL), full, pipeline_mode=once),
                  pl.BlockSpec((1, D_MODEL), full), pl.BlockSpec((1, D_MODEL), full),
                  pl.BlockSpec((D_MODEL, N_IN_PAD), full, pipeline_mode=once),
                  pl.BlockSpec((1, N_IN_PAD), full)],
        out_specs=[pl.BlockSpec((tm, D_MODEL), row), pl.BlockSpec((tm, N_IN_PAD), row)],
        out_shape=[jax.ShapeDtypeStruct((m, D_MODEL), F32), jax.ShapeDtypeStruct((m, N_IN_PAD), F32)],
        scratch_shapes=[pltpu.VMEM((2, TOP_K, tm, D_MODEL), F32), pltpu.SemaphoreType.DMA((2,))],
        compiler_params=pltpu.CompilerParams(dimension_semantics=("arbitrary",), vmem_limit_bytes=VMEM_LIMIT,
                                             disable_bounds_checks=True),
        name="combine_ple_in_proj",
    )(dest, dest, yb, gate, x1, p, lw["w_ple_gate"], lw["w_ple_proj"], lw["ln2_g"], lw["ln2_b"],
      w_in_next, b_in_next)


def _rel_bucket(n):
    max_exact = NUM_BUCKETS // 2
    nf = jnp.maximum(n, 1).astype(F32)
    large = max_exact + (jnp.log(nf / max_exact) / math.log(MAX_DISTANCE / max_exact)
                         * (NUM_BUCKETS - max_exact)).astype(jnp.int32)
    return jnp.where(n < max_exact, jnp.maximum(n, 0), jnp.minimum(large, NUM_BUCKETS - 1))


def _bias_table(rel_bias, dist, valid):
    hit = _rel_bucket(dist)[..., None] == jnp.arange(NUM_BUCKETS)
    b = jnp.sum(jnp.where(hit[..., None], rel_bias.astype(F32), 0.0), axis=-2)
    b = jnp.where(valid[..., None], b, NEG_INF)
    return jnp.moveaxis(b, -1, 0)


def _permute_in(w):
    o = np.concatenate([[0], np.cumsum(IN_SIZES)])
    seg = lambda i: w[..., o[i]:o[i + 1]]
    small = jnp.concatenate([seg(6), seg(7)], -1)
    small = jnp.pad(small, [(0, 0)] * (w.ndim - 1) + [(0, LANE - 2 * N_HEADS_C)])
    return jnp.concatenate([seg(8), seg(3), seg(4), seg(5), seg(0), seg(1), seg(2), small], -1)


def _route(route, counts, n_tok):
    n_blk = -(-n_tok * TOP_K // MOE_ROWS) + N_EXPERTS
    cnt = counts[0, :N_EXPERTS].astype(jnp.int32)
    padded = (cnt + MOE_ROWS - 1) // MOE_ROWS * MOE_ROWS
    pad_end = jnp.cumsum(padded)
    pad_start = pad_end - padded
    idx = route[:, 0:TOP_K]
    rank = route[:, TOP_K:2 * TOP_K]
    start_of = jnp.sum(jnp.where(idx[:, :, None] == jnp.arange(N_EXPERTS)[None, None, :],
                                 pad_start[None, None, :], 0), -1)
    dest = (start_of + rank).reshape(n_tok * TOP_K).astype(jnp.int32)
    blk_first = (jnp.arange(n_blk) * MOE_ROWS)[:, None]
    blk_e = jnp.minimum(jnp.sum((pad_end[None, :] <= blk_first).astype(jnp.int32), -1), N_EXPERTS - 1)
    n_active = (pad_end[-1:] // MOE_ROWS).astype(jnp.int32)
    last_blk = jnp.concatenate([jnp.maximum(pad_end - MOE_ROWS, 0).astype(jnp.int32), n_active])
    return dest, last_blk, blk_e.astype(jnp.int32), n_active, n_blk


def kernel(x_prompt, x_sample, cache_k_a, cache_v_a, state_conv_b, state_conv_c, state_s_c, p_prompt, p_sample,
           rel_bias, w_in, b_in, sink_a, w_o_a, w_dw_b, b_dw_b, ln_b_g, ln_b_b, w_o_b, w_conv_c, a_log_c,
           dt_bias_c, norm_c, w_o_c, w_out, ln1_g, ln1_b, w_router, b_router, w_up, b_up, w_down, b_down,
           ln2_g, ln2_b, w_ple_gate, w_ple_proj):
    depth = w_in.shape[0]
    nb_p, seq, _ = x_prompt.shape
    ns, ls, _ = x_sample.shape
    w_buf = cache_k_a.shape[2]
    alpha = (2 * depth) ** 0.25
    tp = nb_p * seq
    tsm = ns * ls
    n_tok = tp + tsm
    assert seq % DN_CHUNK == 0 and seq % Q_BLOCK == 0 and w_buf == WINDOW and ls <= 16

    w_in_p = _permute_in(w_in).astype(BF16)
    b_in_p = _permute_in(b_in)[:, None, :]
    w_o_a_b, w_o_b_b, w_o_c_b, w_out_b = (t.astype(BF16) for t in (w_o_a, w_o_b, w_o_c, w_out))
    w_ple_gate_b, w_ple_proj_b = w_ple_gate.astype(BF16), w_ple_proj.astype(BF16)
    b_up4, b_down4 = b_up[:, :, None, :], b_down[:, :, None, :]
    w_router_p = jnp.pad(w_router, ((0, 0), (0, 0), (0, LANE - N_EXPERTS))).astype(BF16)
    b_router_p = jnp.pad(b_router, ((0, 0), (0, LANE - N_EXPERTS)))[:, None, :]
    w_dw_r = _round_bf16(w_dw_b)
    w_dw_p = jnp.pad(w_dw_r, ((0, 0), (0, HALO_B - CONV_B), (0, 0)))
    w_cc_p = jnp.pad(_round_bf16(w_conv_c), ((0, 0), (0, SUBLANE - CONV_C), (0, 0)))
    lane_pad = lambda t: jnp.pad(t, ((0, 0), (N_HEADS_C, LANE - 2 * N_HEADS_C)))[:, None, :]
    alog_rows, dtb_rows = lane_pad(a_log_c), lane_pad(dt_bias_c)

    qi = jnp.arange(Q_BLOCK)[:, None]
    kj = jnp.arange(2 * Q_BLOCK)[None, :]
    dist = qi - kj + Q_BLOCK
    valid = (dist >= 0) & (dist < WINDOW)
    tab_rest = _bias_table(rel_bias, dist, valid)
    tab_first = _bias_table(rel_bias, dist, valid & (kj >= Q_BLOCK))
    bias_prompt = jnp.stack([tab_first, tab_rest])

    kwin = 2 * WINDOW
    ti = jnp.arange(ls)[:, None]
    kjs = jnp.arange(kwin)[None, :]
    dist_s = w_buf + ti - kjs
    valid_s = (dist_s >= 0) & (dist_s < WINDOW) & (kjs < w_buf + ls)
    tab_s = _bias_table(rel_bias, dist_s, valid_s)
    tab_s = tab_s.reshape(N_KV_A, GROUP_A * ls, kwin)
    gb = 2 * _tile(ns, 16, 1)
    bias_s = jnp.tile(tab_s, (gb // N_KV_A, 1, 1))

    wrows = -(-(CONV_B - 1 + ls) // SUBLANE) * SUBLANE
    jj = jnp.arange(wrows)[None, :] - jnp.arange(ls)[:, None]
    tap_ok = (jj >= 0) & (jj < CONV_B)

    x_all = jnp.concatenate([x_prompt.reshape(tp, D_MODEL), x_sample.reshape(tsm, D_MODEL)], 0)
    xb_all = x_all.astype(BF16)
    p_all = jnp.concatenate([p_prompt.reshape(depth, tp, PLE_DIM), p_sample.reshape(depth, tsm, PLE_DIM)], 1)
    zero_s = jnp.zeros((nb_p, 1, 1, N_HEADS_C, HEAD_DIM_C, HEAD_DIM_C), F32)
    c_s = 16
    nsq_s = DN_UNIT // c_s
    assert ls + CONV_C - 1 <= c_s and ns % nsq_s == 0

    st_p, st_s = [], []
    h = _in_proj(xb_all, w_in_p[0], b_in_p[0])
    for i in range(depth):
        lw = {"w_o_a": w_o_a_b[i], "w_o_b": w_o_b_b[i], "w_o_c": w_o_c_b[i], "w_out": w_out_b[i],
              "ln1_g": ln1_g[i][None], "ln1_b": ln1_b[i][None], "w_router": w_router_p[i],
              "b_router": b_router_p[i], "w_ple_gate": w_ple_gate_b[i], "w_ple_proj": w_ple_proj_b[i],
              "ln2_g": ln2_g[i][None], "ln2_b": ln2_b[i][None]}
        hs = h[tp:].reshape(ns, ls, N_IN_PAD)

        sinks = sink_a[i].astype(F32)
        oa_p = _attn_prompt(h, sinks, bias_prompt, nb_p, seq)
        q_s = hs[:, :, COL_QA:COL_QA + Q_A].reshape(ns, ls, N_KV_A, GROUP_A, HEAD_DIM_A)
        q_s = q_s.transpose(0, 2, 3, 1, 4).reshape(ns * N_KV_A, GROUP_A * ls, HEAD_DIM_A)
        k_new = hs[:, :, COL_KA:COL_KA + KV_A].reshape(ns, ls, N_KV_A, HEAD_DIM_A)
        v_new = hs[:, :, COL_VA:COL_VA + KV_A].reshape(ns, ls, N_KV_A, HEAD_DIM_A)
        k_all = jnp.concatenate([cache_k_a[i], k_new], 1)
        v_all = jnp.concatenate([cache_v_a[i], v_new], 1)

        def keywin(t):
            t = jnp.pad(t, ((0, 0), (0, kwin - w_buf - ls), (0, 0), (0, 0)))
            return t.transpose(0, 2, 1, 3).reshape(ns * N_KV_A, kwin, HEAD_DIM_A)

        sink_s = jnp.tile(jnp.repeat(sinks.reshape(N_KV_A, GROUP_A), ls, axis=1)[:, :, None], (gb // N_KV_A, 1, 1))
        oa_s = _attn_sample(q_s, keywin(k_all), keywin(v_all), bias_s, sink_s)
        oa_s = oa_s.reshape(ns, N_KV_A, GROUP_A, ls, HEAD_DIM_A).transpose(0, 3, 1, 2, 4).reshape(tsm, Q_A)
        oa = (oa_p, oa_s.astype(BF16))

        cb_p, tail_p = _convb_prompt(h, w_dw_p[i], b_dw_b[i][None], ln_b_g[i][None], ln_b_b[i][None], nb_p, seq)
        wt = jnp.where(tap_ok[:, :, None], w_dw_r[i][jnp.clip(jj, 0, CONV_B - 1)], 0.0)
        cb_s, nconv_b_s = _convb_sample(hs, state_conv_b[i], wt, b_dw_b[i][None], ln_b_g[i][None], ln_b_b[i][None])
        cb = (cb_p, cb_s.transpose(1, 0, 2).reshape(tsm, C_B))

        nw = norm_c[i][None]
        nchunk = seq // DN_CHUNK
        prep_p = _dn_prep(h, h, w_cc_p[i], alog_rows[i], dtb_rows[i], qcol=COL_QKVC // V_C,
                          smcol=COL_SMALL // LANE, nbatch=nb_p, nchunk=nchunk, c=DN_CHUNK, lr=DN_CHUNK)
        oc_p, s_p = _dn_seq(prep_p, h, nw, zero_s, zcol=COL_Z // V_C, streams=nb_p, steps=nchunk,
                            c=DN_CHUNK, nsq=1, carry=True)
        s_p = s_p.reshape(nb_p, N_HEADS_C, HEAD_DIM_C, HEAD_DIM_C)
        lead = c_s - ls
        hist = jnp.concatenate([jnp.zeros((ns, lead - (CONV_C - 1), QKV_C), F32), state_conv_c[i],
                                hs[:, :, COL_QKVC:COL_QKVC + QKV_C]], 1).reshape(ns * c_s, QKV_C)
        pad_s = lambda col, wid: jnp.pad(hs[:, :, col:col + wid], ((0, 0), (lead, 0), (0, 0))).reshape(ns * c_s, wid)
        prep_s = _dn_prep(hist, pad_s(COL_SMALL, LANE), w_cc_p[i], alog_rows[i], dtb_rows[i], qcol=0, smcol=0,
                          nbatch=ns // nsq_s, nchunk=1, c=c_s, lr=ls)
        units_s = ns // nsq_s
        streams_s = 2 if units_s % 2 == 0 else 1
        s0_s = state_s_c.astype(F32).reshape(depth, streams_s, units_s // streams_s, nsq_s, N_HEADS_C, HEAD_DIM_C,
                                             HEAD_DIM_C)
        oc_s, s_s = _dn_seq(prep_s, pad_s(COL_Z, V_C), nw, s0_s, zcol=0, streams=streams_s,
                            steps=units_s // streams_s, c=c_s, nsq=nsq_s, carry=False, layer=i)
        s_s = s_s.reshape(ns, N_HEADS_C, HEAD_DIM_C, HEAD_DIM_C)
        oc = (oc_p, oc_s.reshape(ns, c_s, V_C)[:, lead:].reshape(tsm, V_C))

        x1, route, gate, counts = _merge((oa, cb, oc), h, x_all, lw, alpha)

        dest, last_blk, blk_e, n_active, n_blk = _route(route, counts, n_tok)
        xs = _dispatch(dest, last_blk, x1, n_blk * MOE_ROWS)
        yb = _moe(xs, blk_e, n_active, w_up, b_up4, w_down, b_down4, i)

        if i + 1 < depth:
            x_all, h_next = _post_next(dest, yb, gate, x1, p_all[i], lw, alpha, w_in_p[i + 1], b_in_p[i + 1])
        else:
            x_all, _ = _post(dest, yb, gate, x1, p_all[i], lw, alpha)
            h_next = None

        def tail_rows(nrows, col, width):
            return jnp.stack([h[(b + 1) * seq - nrows:(b + 1) * seq, col:col + width] for b in range(nb_p)])

        kp_new = tail_rows(w_buf, COL_KA, KV_A)
        vp_new = tail_rows(w_buf, COL_VA, KV_A)
        ccp_new = tail_rows(CONV_C - 1, COL_QKVC, QKV_C)
        st_p.append((kp_new.reshape(nb_p, w_buf, N_KV_A, HEAD_DIM_A), vp_new.reshape(nb_p, w_buf, N_KV_A, HEAD_DIM_A),
                     tail_p[:, HALO_B - (CONV_B - 1):], ccp_new, s_p))
        ccs_ext = jnp.concatenate([state_conv_c[i], hs[:, :, COL_QKVC:COL_QKVC + QKV_C]], 1)
        st_s.append((k_all[:, ls:], v_all[:, ls:], nconv_b_s, ccs_ext[:, -(CONV_C - 1):], s_s))
        h = h_next

    yp = x_all[:tp].reshape(nb_p, seq, D_MODEL)
    ys = x_all[tp:].reshape(ns, ls, D_MODEL)
    kp, vp, cbp, ccp, scp = [jnp.stack(z) for z in zip(*st_p)]
    ks_, vs_, cbs, ccs, scs = [jnp.stack(z) for z in zip(*st_s)]
    return (yp, ys, kp, vp, cbp, ccp, scp, ks_, vs_, cbs, ccs, scs)
```
